```python
import math
import jax, jax.numpy as jnp
from jax import lax
import numpy as np

D_MODEL = 1024
BATCH = 8
SEQ = 2048
DEPTH = 1

CTX_LEN = 256
GRID_W = 64
HEAD_DIM = 64
ATTN_HEADS = 8
KV_HEADS = 2
Q_PER_KV = ATTN_HEADS // KV_HEADS
ATTN_WIDTH = ATTN_HEADS * HEAD_DIM
KV_WIDTH = KV_HEADS * HEAD_DIM
WINDOW = 128
BLOCK = 128
ROPE_BASE = 10000.0
SSM_WIDTH = D_MODEL - ATTN_WIDTH
SSM_HEAD_DIM = 64
SSM_HEADS = SSM_WIDTH // SSM_HEAD_DIM
SSM_GROUPS = 2
D_STATE = 128
XBC_WIDTH = SSM_WIDTH + 2 * SSM_GROUPS * D_STATE
SSM_CONV = 3
CHUNK = 128
D_FF = 2816
FFN_CONV = 3
IN_COLS = ATTN_WIDTH + 2 * KV_WIDTH + SSM_WIDTH + XBC_WIDTH + 2 * SSM_HEADS
EPS = 1e-6
NEG_INF = -1e30

kernel_name = 'hybrid_swa_ssd_dit_block'


def rmsnorm(x, g):
    xf = x.astype(jnp.float32)
    y = xf * lax.rsqrt(jnp.mean(xf * xf, axis=-1, keepdims=True) + EPS)
    return y.astype(x.dtype) * g


def modulate(h, shift, scale):
    return h * (1 + scale) + shift


def dwconv_centred(u, w, b):
    out = lax.conv_general_dilated(u, w[:, None, :], window_strides=(1,), padding='SAME',
                                   dimension_numbers=('NWC', 'WIO', 'NWC'),
                                   feature_group_count=u.shape[-1])
    return out + b


def axial_rope_tables(L):
    rows = L // GRID_W
    pos_r = jnp.repeat(jnp.arange(rows), GRID_W).astype(jnp.float32)
    pos_c = jnp.tile(jnp.arange(GRID_W), rows).astype(jnp.float32)
    quarter = HEAD_DIM // 4
    freqs = ROPE_BASE ** (-jnp.arange(quarter, dtype=jnp.float32) / quarter)
    ang_r = pos_r[:, None] * freqs[None, :]
    ang_c = pos_c[:, None] * freqs[None, :]
    return (jnp.cos(ang_r), jnp.sin(ang_r), jnp.cos(ang_c), jnp.sin(ang_c))


def _rotate(x, cos, sin):
    x1, x2 = jnp.split(x, 2, axis=-1)
    cos = cos[:, None, :].astype(x.dtype)
    sin = sin[:, None, :].astype(x.dtype)
    return jnp.concatenate([x1 * cos - x2 * sin, x2 * cos + x1 * sin], axis=-1)


def apply_axial_rope(x, tables):
    cos_r, sin_r, cos_c, sin_c = tables
    xr, xc = jnp.split(x, 2, axis=-1)
    return jnp.concatenate([_rotate(xr, cos_r, sin_r), _rotate(xc, cos_c, sin_c)], axis=-1)


def split_projection(p):
    sizes = [ATTN_WIDTH, KV_WIDTH, KV_WIDTH, SSM_WIDTH, XBC_WIDTH]
    points, acc = [], 0
    for s in sizes:
        acc += s
        points.append(acc)
    return jnp.split(p, points, axis=-1)


def to_heads(t, n):
    return t.reshape(t.shape[0], t.shape[1], n, HEAD_DIM)


def window_attention(q, k, v, k_ctx, v_ctx, sink):
    b, L = q.shape[0], q.shape[1]
    lc = k_ctx.shape[1]
    nb = L // BLOCK
    scale = HEAD_DIM ** -0.5
    qb = q.reshape(b, nb, BLOCK, KV_HEADS, Q_PER_KV, HEAD_DIM)

    def banded(t):
        tp = jnp.pad(t, ((0, 0), (BLOCK, BLOCK), (0, 0), (0, 0)))
        tp = tp.reshape(b, nb + 2, BLOCK, KV_HEADS, HEAD_DIM)
        return jnp.concatenate([tp[:, :-2], tp[:, 1:-1], tp[:, 2:]], axis=2)

    kw, vw = banded(k), banded(v)
    s_loc = jnp.einsum('bnqkgd,bnjkd->bnkgqj', qb, kw).astype(jnp.float32) * scale
    blk = jnp.arange(nb)[:, None, None]
    qi = jnp.arange(BLOCK)[None, :, None]
    kj = jnp.arange(3 * BLOCK)[None, None, :]
    q_pos = blk * BLOCK + qi
    k_pos = blk * BLOCK - BLOCK + kj
    valid = (jnp.abs(k_pos - q_pos) <= WINDOW) & (k_pos >= 0) & (k_pos < L)
    s_loc = jnp.where(valid[None, :, None, None], s_loc, NEG_INF)
    s_ctx = jnp.einsum('bnqkgd,bckd->bnkgqc', qb, k_ctx).astype(jnp.float32) * scale
    s_sink = jnp.broadcast_to(
        sink.astype(jnp.float32).reshape(KV_HEADS, Q_PER_KV)[None, None, :, :, None, None],
        s_loc.shape[:-1] + (1,))
    p = jax.nn.softmax(jnp.concatenate([s_loc, s_ctx, s_sink], axis=-1), axis=-1).astype(v.dtype)
    p_loc = p[..., :3 * BLOCK]
    p_ctx = p[..., 3 * BLOCK:3 * BLOCK + lc]
    out = (jnp.einsum('bnkgqj,bnjkd->bnqkgd', p_loc, vw)
           + jnp.einsum('bnkgqc,bckd->bnqkgd', p_ctx, v_ctx))
    return out.reshape(b, L, ATTN_WIDTH)


def context_attention(q_c, k_c, v_c, sink):
    b, lc = q_c.shape[0], q_c.shape[1]
    scale = HEAD_DIM ** -0.5
    qg = q_c.reshape(b, lc, KV_HEADS, Q_PER_KV, HEAD_DIM)
    s = jnp.einsum('bqkgd,bckd->bkgqc', qg, k_c).astype(jnp.float32) * scale
    s_sink = jnp.broadcast_to(
        sink.astype(jnp.float32).reshape(KV_HEADS, Q_PER_KV)[None, :, :, None, None], s.shape[:-1] + (1,))
    p = jax.nn.softmax(jnp.concatenate([s, s_sink], axis=-1), axis=-1)[..., :lc].astype(v_c.dtype)
    out = jnp.einsum('bkgqc,bckd->bqkgd', p, v_c)
    return out.reshape(b, lc, ATTN_WIDTH)


def ssd_scan(xs, dt, A, Bm, Cm, h0):
    b, L = xs.shape[0], xs.shape[1]
    nc = L // CHUNK
    xc = xs.reshape(b, nc, CHUNK, SSM_HEADS, SSM_HEAD_DIM)
    dtc = dt.reshape(b, nc, CHUNK, SSM_HEADS)
    Bc = Bm.reshape(b, nc, CHUNK, SSM_HEADS, D_STATE)
    Cc = Cm.reshape(b, nc, CHUNK, SSM_HEADS, D_STATE)
    acum = jnp.cumsum(dtc * A, axis=2)
    seg = acum[:, :, :, None, :] - acum[:, :, None, :, :]
    lower = jnp.tril(jnp.ones((CHUNK, CHUNK), dtype=bool))[None, None, :, :, None]
    Lmat = jnp.where(lower, jnp.exp(jnp.where(lower, seg, 0.0)), 0.0)
    xdt = xc * dtc[..., None]
    scores = jnp.einsum('bcihn,bcjhn->bcijh', Cc, Bc) * Lmat
    y_diag = jnp.einsum('bcijh,bcjhp->bcihp', scores, xdt)
    decay_end = jnp.exp(acum[:, :, -1:, :] - acum)
    states = jnp.einsum('bcjhn,bcjh,bcjhp->bchpn', Bc, decay_end, xdt)
    chunk_decay = jnp.exp(acum[:, :, -1, :])

    def step(h, inp):
        dec, st = inp
        return dec[:, :, None, None] * h + st, h

    h_final, h_start = lax.scan(step, h0, (jnp.moveaxis(chunk_decay, 1, 0), jnp.moveaxis(states, 1, 0)))
    h_start = jnp.moveaxis(h_start, 0, 1)
    y_off = jnp.einsum('bcihn,bchpn,bcih->bcihp', Cc, h_start, jnp.exp(acum))
    y = (y_diag + y_off).reshape(b, L, SSM_HEADS, SSM_HEAD_DIM)
    return y, h_final


def ssm_mixer(z, xbc, dt_raw, conv_w, conv_b, a_log, dt_bias, d_skip, g_ssm, h0_fwd, h0_bwd):
    b, L = z.shape[0], z.shape[1]
    xbc = jax.nn.silu(dwconv_centred(xbc, conv_w, conv_b)).astype(jnp.float32)
    xs, Bm, Cm = jnp.split(xbc, [SSM_WIDTH, SSM_WIDTH + SSM_GROUPS * D_STATE], axis=-1)
    xs = xs.reshape(b, L, SSM_HEADS, SSM_HEAD_DIM)
    hpg = SSM_HEADS // SSM_GROUPS
    Bm = jnp.repeat(Bm.reshape(b, L, SSM_GROUPS, D_STATE), hpg, axis=2)
    Cm = jnp.repeat(Cm.reshape(b, L, SSM_GROUPS, D_STATE), hpg, axis=2)
    dt_raw = dt_raw.astype(jnp.float32)
    dt_bias = dt_bias.astype(jnp.float32)
    dt_f = jax.nn.softplus(dt_raw[..., :SSM_HEADS] + dt_bias[0])
    dt_b = jax.nn.softplus(dt_raw[..., SSM_HEADS:] + dt_bias[1])
    A = -jnp.exp(a_log.astype(jnp.float32))
    y_f, h_f = ssd_scan(xs, dt_f, A[0], Bm, Cm, h0_fwd)
    flip = lambda t: jnp.flip(t, axis=1)
    y_b, h_b = ssd_scan(flip(xs), flip(dt_b), A[1], flip(Bm), flip(Cm), h0_bwd)
    y = y_f + flip(y_b) + d_skip.astype(jnp.float32)[:, None] * xs
    y = y.reshape(b, L, SSM_WIDTH) * jax.nn.silu(z.astype(jnp.float32))
    yg = y.reshape(b, L, SSM_GROUPS, SSM_WIDTH // SSM_GROUPS)
    yg = yg * lax.rsqrt(jnp.mean(yg * yg, axis=-1, keepdims=True) + EPS)
    y = yg.reshape(b, L, SSM_WIDTH).astype(z.dtype) * g_ssm
    return y, h_f, h_b


def conv_ffn(h, w_up, conv_w, conv_b, w_down):
    u = dwconv_centred(h @ w_up, conv_w, conv_b)
    a, g = jnp.split(u, 2, axis=-1)
    return (a * jax.nn.silu(g)) @ w_down


def setup_inputs(seed: int = 0) -> dict:
    key = jax.random.key(seed)
    ks = jax.random.split(key, 24)
    nrm = lambda k, shape, s: jax.random.normal(k, shape, jnp.float32) * s
    dt0 = jnp.exp(jax.random.uniform(ks[14], (DEPTH, 2, SSM_HEADS), jnp.float32,
                                     minval=math.log(1e-3), maxval=math.log(1e-1)))
    return {
        'x': nrm(ks[0], (BATCH, SEQ, D_MODEL), 1.0),
        'c': nrm(ks[1], (BATCH, D_MODEL), 1.0),
        'ctx': nrm(ks[2], (BATCH, CTX_LEN, D_MODEL), 1.0),
        'c_ctx': nrm(ks[3], (D_MODEL,), 1.0),
        'w_mod': nrm(ks[4], (DEPTH, D_MODEL, 6 * D_MODEL), 0.5 * D_MODEL ** -0.5),
        'b_mod': nrm(ks[5], (DEPTH, 6 * D_MODEL), 0.01),
        'g_mix': 1.0 + nrm(ks[6], (DEPTH, D_MODEL), 0.01),
        'w_in': nrm(ks[7], (DEPTH, D_MODEL, IN_COLS), D_MODEL ** -0.5),
        'g_q': 1.0 + nrm(ks[8], (DEPTH, HEAD_DIM), 0.01),
        'g_k': 1.0 + nrm(ks[9], (DEPTH, HEAD_DIM), 0.01),
        'sink': nrm(ks[10], (DEPTH, ATTN_HEADS), 0.5),
        'ssm_conv_w': nrm(ks[11], (DEPTH, SSM_CONV, XBC_WIDTH), SSM_CONV ** -0.5),
        'ssm_conv_b': nrm(ks[12], (DEPTH, XBC_WIDTH), 0.01),
        'a_log': jnp.log(jax.random.uniform(ks[13], (DEPTH, 2, SSM_HEADS), jnp.float32, minval=1.0, maxval=16.0)),
        'dt_bias': dt0 + jnp.log(-jnp.expm1(-dt0)),
        'd_skip': 1.0 + nrm(ks[15], (DEPTH, SSM_HEADS), 0.01),
        'g_ssm': 1.0 + nrm(ks[16], (DEPTH, SSM_WIDTH), 0.01),
        'w_out': nrm(ks[17], (DEPTH, D_MODEL, D_MODEL), D_MODEL ** -0.5),
        'g_ffn': 1.0 + nrm(ks[18], (DEPTH, D_MODEL), 0.01),
        'w_up': nrm(ks[19], (DEPTH, D_MODEL, 2 * D_FF), D_MODEL ** -0.5),
        'ffn_conv_w': nrm(ks[20], (DEPTH, FFN_CONV, 2 * D_FF), FFN_CONV ** -0.5),
        'ffn_conv_b': nrm(ks[21], (DEPTH, 2 * D_FF), 0.01),
        'w_down': nrm(ks[22], (DEPTH, D_FF, D_MODEL), D_FF ** -0.5),
    }


def reference(x, c, ctx, c_ctx, w_mod, b_mod, g_mix, w_in, g_q, g_k, sink, ssm_conv_w, ssm_conv_b,
              a_log, dt_bias, d_skip, g_ssm, w_out, g_ffn, w_up, ffn_conv_w, ffn_conv_b, w_down):
    b, L = x.shape[0], x.shape[1]
    rope = axial_rope_tables(L)
    for layer in range(DEPTH):
        mod = jax.nn.silu(c) @ w_mod[layer] + b_mod[layer]
        mod_c = jax.nn.silu(c_ctx) @ w_mod[layer] + b_mod[layer]
        sh_a, sc_a, ga_a, sh_f, sc_f, ga_f = jnp.split(mod[:, None, :], 6, axis=-1)
        csh_a, csc_a, cga_a, csh_f, csc_f, cga_f = jnp.split(mod_c[None, None, :], 6, axis=-1)

        h = modulate(rmsnorm(x, g_mix[layer]), sh_a, sc_a)
        hc = modulate(rmsnorm(ctx, g_mix[layer]), csh_a, csc_a)
        q, k, v, z, xbc, dt_raw = split_projection(h @ w_in[layer])
        qc, kc, vc, zc, xbcc, dt_rawc = split_projection(hc @ w_in[layer])

        q = apply_axial_rope(rmsnorm(to_heads(q, ATTN_HEADS), g_q[layer]), rope)
        k = apply_axial_rope(rmsnorm(to_heads(k, KV_HEADS), g_k[layer]), rope)
        v = to_heads(v, KV_HEADS)
        kc = rmsnorm(to_heads(kc, KV_HEADS), g_k[layer])
        vc = to_heads(vc, KV_HEADS)
        attn = window_attention(q, k, v, kc, vc, sink[layer])

        ssm_params = (ssm_conv_w[layer], ssm_conv_b[layer], a_log[layer], dt_bias[layer], d_skip[layer], g_ssm[layer])
        h0 = jnp.zeros((b, SSM_HEADS, SSM_HEAD_DIM, D_STATE), jnp.float32)
        y_ssm_c, hf_c, hb_c = ssm_mixer(zc, xbcc, dt_rawc, *ssm_params, h0, h0)
        y_ssm, _, _ = ssm_mixer(z, xbc, dt_raw, *ssm_params, hf_c, hb_c)

        x_new = x + ga_a * (jnp.concatenate([attn, y_ssm], axis=-1) @ w_out[layer])
        x_new = x_new + ga_f * conv_ffn(modulate(rmsnorm(x_new, g_ffn[layer]), sh_f, sc_f),
                                        w_up[layer], ffn_conv_w[layer], ffn_conv_b[layer], w_down[layer])

        if layer < DEPTH - 1:
            qc = rmsnorm(to_heads(qc, ATTN_HEADS), g_q[layer])
            attn_c = context_attention(qc, kc, vc, sink[layer])
            ctx = ctx + cga_a * (jnp.concatenate([attn_c, y_ssm_c], axis=-1) @ w_out[layer])
            ctx = ctx + cga_f * conv_ffn(modulate(rmsnorm(ctx, g_ffn[layer]), csh_f, csc_f),
                                         w_up[layer], ffn_conv_w[layer], ffn_conv_b[layer], w_down[layer])
        x = x_new
    return x
```

```python
import functools

import jax
import jax.numpy as jnp
from jax import lax
from jax.experimental import pallas as pl
from jax.experimental.pallas import tpu as pltpu

F32 = jnp.float32
BF16 = jnp.bfloat16

D_MODEL = 1024
CTX = 256
GRID_W = 64
HEAD_DIM = 64
ATTN_HEADS = 8
KV_HEADS = 2
ATTN_WIDTH = ATTN_HEADS * HEAD_DIM
KV_WIDTH = KV_HEADS * HEAD_DIM
WINDOW = 128
BLOCK = 128
ROPE_BASE = 10000.0
SSM_WIDTH = D_MODEL - ATTN_WIDTH
SSM_HEAD_DIM = 64
SSM_HEADS = SSM_WIDTH // SSM_HEAD_DIM
SSM_GROUPS = 2
D_STATE = 128
XBC_WIDTH = SSM_WIDTH + 2 * SSM_GROUPS * D_STATE
CHUNK = 128
D_FF = 2816
IN_COLS = ATTN_WIDTH + 2 * KV_WIDTH + SSM_WIDTH + XBC_WIDTH + 2 * SSM_HEADS
EPS = 1e-6
NEG_INF = -1e30

LANES = 128
BF16_SUBLANES = 16
IN_COLS_PAD = 19 * LANES
COL_Q, COL_K, COL_V = 0, ATTN_WIDTH, ATTN_WIDTH + KV_WIDTH
COL_Z = ATTN_WIDTH + 2 * KV_WIDTH
COL_XBC = COL_Z + SSM_WIDTH
COL_DT = COL_XBC + XBC_WIDTH
VMEM_LIMIT = 56 * 1024 * 1024

TM_IN = 256
TM_OUT = 512
TM_FFN = 512
TF_FFN = 1408
HALO = BF16_SUBLANES


def _params(*sem):
    return pltpu.CompilerParams(dimension_semantics=sem, vmem_limit_bytes=VMEM_LIMIT)


def _sigmoid(v):
    return 1.0 / (1.0 + jnp.exp(-v))


def _softplus(v):
    return jnp.maximum(v, 0.0) + jnp.log1p(jnp.exp(-jnp.abs(v)))


def _dot(a, b):
    return jnp.dot(a, b, preferred_element_type=F32)


def _dot_nt(a, b):
    return lax.dot_general(a, b, (((1,), (1,)), ((), ())), preferred_element_type=F32)


def _mod_kernel(c_ref, w_ref, b_ref, o_ref):
    cv = c_ref[...]
    s = (cv * _sigmoid(cv)).astype(BF16)
    o_ref[...] = _dot(s, w_ref[...].astype(BF16)) + b_ref[...]


def _modulation(cc, w_mod, b_mod):
    rows = cc.shape[0]
    tn = 1024
    return pl.pallas_call(
        _mod_kernel,
        grid=(6 * D_MODEL // tn,),
        in_specs=[pl.BlockSpec((rows, D_MODEL), lambda j: (0, 0)),
                  pl.BlockSpec((D_MODEL, tn), lambda j: (0, j)),
                  pl.BlockSpec((1, tn), lambda j: (0, j))],
        out_specs=pl.BlockSpec((rows, tn), lambda j: (0, j)),
        out_shape=jax.ShapeDtypeStruct((rows, 6 * D_MODEL), F32),
        compiler_params=_params("arbitrary"),
        name="mod",
    )(cc, w_mod, b_mod.reshape(1, -1))


def _inproj_kernel(x_ref, ctx_ref, mod_ref, gmix_ref, w_ref, gq_ref, gk_ref,
                   cos_ref, sina_ref, sinb_ref, bd_ref,
                   q_ref, k_ref, v_ref, z_ref, xbc_ref, dt_ref):
    s = pl.program_id(1)
    xin = jnp.where(s == 0, ctx_ref[0], x_ref[0])
    ms = jnp.mean(xin * xin, axis=-1, keepdims=True)
    hn = xin * lax.rsqrt(ms + EPS) * gmix_ref[...]
    shift = mod_ref[0, :, 0:D_MODEL]
    scale = mod_ref[0, :, D_MODEL:2 * D_MODEL]
    h = (hn * (1.0 + scale) + shift).astype(BF16)

    cos = cos_ref[...]
    sina = sina_ref[...]
    sinb = sinb_ref[...]
    bd = bd_ref[...]

    def norm_rope(pf, gain):
        ss = pf * pf
        hi = ss.astype(BF16)
        lo = (ss - hi.astype(F32)).astype(BF16)
        sums = _dot(hi, bd) + _dot(lo, bd)
        y = pf * lax.rsqrt(sums * (1.0 / HEAD_DIM) + EPS) * gain
        return y * cos + pltpu.roll(y, LANES - 16, 1) * sina + pltpu.roll(y, 16, 1) * sinb

    for cidx in range(ATTN_WIDTH // LANES):
        pf = _dot(h, w_ref[:, COL_Q + cidx * LANES:COL_Q + (cidx + 1) * LANES])
        q_ref[0, :, cidx * LANES:(cidx + 1) * LANES] = norm_rope(pf, gq_ref[...]).astype(BF16)

    kf = norm_rope(_dot(h, w_ref[:, COL_K:COL_K + KV_WIDTH]), gk_ref[...])
    k_ref[0, :, 0:LANES] = kf.astype(BF16)
    k_ref[0, :, LANES:2 * LANES] = pltpu.roll(kf, HEAD_DIM, 1).astype(BF16)
    vf = _dot(h, w_ref[:, COL_V:COL_V + KV_WIDTH])
    v_ref[0, :, 0:LANES] = vf.astype(BF16)
    v_ref[0, :, LANES:2 * LANES] = pltpu.roll(vf, HEAD_DIM, 1).astype(BF16)
    z_ref[0] = _dot(h, w_ref[:, COL_Z:COL_Z + SSM_WIDTH]).astype(BF16)
    xbc_ref[0] = _dot(h, w_ref[:, COL_XBC:COL_XBC + XBC_WIDTH]).astype(BF16)
    dt_ref[0] = _dot(h, w_ref[:, COL_DT:COL_DT + LANES])


def _rope_tables(seq):
    rows = seq // GRID_W
    pos_r = jnp.repeat(jnp.arange(rows), GRID_W).astype(F32)
    pos_c = jnp.tile(jnp.arange(GRID_W), rows).astype(F32)
    quarter = HEAD_DIM // 4
    freqs = ROPE_BASE ** (-jnp.arange(quarter, dtype=F32) / quarter)
    ang_r = pos_r[:, None] * freqs[None, :]
    ang_c = pos_c[:, None] * freqs[None, :]
    cr, sr, cc, sc = jnp.cos(ang_r), jnp.sin(ang_r), jnp.cos(ang_c), jnp.sin(ang_c)
    zero = jnp.zeros_like(sr)
    cos = jnp.concatenate([cr, cr, cc, cc], axis=-1)
    sina = jnp.concatenate([-sr, zero, -sc, zero], axis=-1)
    sinb = jnp.concatenate([zero, sr, zero, sc], axis=-1)
    reps = LANES // HEAD_DIM
    pad = lambda t, v: jnp.concatenate([jnp.full((CTX, LANES), v, F32), jnp.tile(t, (1, reps))], axis=0)
    return pad(cos, 1.0), pad(sina, 0.0), pad(sinb, 0.0)


def _inproj(x, ctx, mod3, g_mix, w_in_p, g_q, g_k):
    b, seq = x.shape[0], x.shape[1]
    tot = CTX + seq
    cos, sina, sinb = _rope_tables(seq)
    reps = LANES // HEAD_DIM
    gq = (jnp.tile(g_q, reps) * (HEAD_DIM ** -0.5)).reshape(1, LANES)
    gk = jnp.tile(g_k, reps).reshape(1, LANES)
    lane = jnp.arange(LANES)
    bd = (lane[:, None] // HEAD_DIM == lane[None, :] // HEAD_DIM).astype(BF16)
    nsteps = tot // TM_IN
    const = lambda *shape: pl.BlockSpec(shape, lambda bb, s: (0,) * len(shape))
    tab = pl.BlockSpec((TM_IN, LANES), lambda bb, s: (s, 0))
    out = lambda w: pl.BlockSpec((1, TM_IN, w), lambda bb, s: (bb, s, 0))
    shp = lambda w, dt: jax.ShapeDtypeStruct((b, tot, w), dt)
    return pl.pallas_call(
        _inproj_kernel,
        grid=(b, nsteps),
        in_specs=[pl.BlockSpec((1, TM_IN, D_MODEL), lambda bb, s: (bb, jnp.maximum(s - 1, 0), 0)),
                  pl.BlockSpec((1, CTX, D_MODEL), lambda bb, s: (bb, 0, 0)),
                  pl.BlockSpec((1, 1, 6 * D_MODEL), lambda bb, s: (jnp.where(s == 0, b, bb), 0, 0)),
                  const(1, D_MODEL), const(D_MODEL, IN_COLS_PAD), const(1, LANES), const(1, LANES),
                  tab, tab, tab, const(LANES, LANES)],
        out_specs=[out(ATTN_WIDTH), out(2 * LANES), out(2 * LANES), out(SSM_WIDTH), out(XBC_WIDTH), out(LANES)],
        out_shape=[shp(ATTN_WIDTH, BF16), shp(2 * LANES, BF16), shp(2 * LANES, BF16),
                   shp(SSM_WIDTH, BF16), shp(XBC_WIDTH, BF16), shp(LANES, F32)],
        compiler_params=_params("arbitrary", "arbitrary"),
        name="inproj",
    )(x, ctx, mod3, g_mix.reshape(1, -1), w_in_p, gq, gk, cos, sina, sinb, bd)


def _attn_kernel(sink_ref, q_ref, kp_ref, kc_ref, kn_ref, kx_ref, vp_ref, vc_ref, vn_ref, vx_ref, o_ref):
    nb = pl.program_id(1)
    nblk = pl.num_programs(1)
    kall = jnp.concatenate([kp_ref[0], kc_ref[0], kn_ref[0], kx_ref[0]], axis=0)
    vall = jnp.concatenate([vp_ref[0], vc_ref[0], vn_ref[0], vx_ref[0]], axis=0)
    nkeys = kall.shape[0]

    qi = lax.broadcasted_iota(jnp.int32, (BLOCK, nkeys), 0)
    kj = lax.broadcasted_iota(jnp.int32, (BLOCK, nkeys), 1)
    rel = kj - BLOCK - qi
    far = 4 * BLOCK
    lo = jnp.where(nb > 0, -WINDOW, far)
    hi = jnp.where(nb < nblk - 1, WINDOW, -far)
    zero = jnp.zeros((BLOCK, nkeys), F32)
    bias_prev = jnp.where(rel >= lo, zero, NEG_INF)
    bias_next = jnp.where(rel <= hi, zero, NEG_INF)
    bias = jnp.where(kj < BLOCK, bias_prev, jnp.where(kj < 2 * BLOCK, zero, jnp.where(kj < 3 * BLOCK, bias_next, zero)))

    lane = lax.broadcasted_iota(jnp.int32, (nkeys, LANES), 1)
    q = q_ref[0]
    for kh in range(KV_HEADS):
        q2 = jnp.concatenate([q[:, (2 * kh) * LANES:(2 * kh + 1) * LANES],
                              q[:, (2 * kh + 1) * LANES:(2 * kh + 2) * LANES]], axis=0)
        acc = [None, None]
        for par in range(2):
            src = 0 if par == kh else 1
            half = (lane >= HEAD_DIM) if par else (lane < HEAD_DIM)
            kpl = jnp.where(half, kall[:, src * LANES:(src + 1) * LANES], jnp.zeros_like(kall[:, :LANES]))
            vpl = jnp.where(half, vall[:, src * LANES:(src + 1) * LANES], jnp.zeros_like(vall[:, :LANES]))
            s2 = _dot_nt(q2, kpl)
            for r in range(2):
                head = 4 * kh + par + 2 * r
                sv = s2[r * BLOCK:(r + 1) * BLOCK] + bias
                sk = sink_ref[head]
                m = jnp.maximum(jnp.max(sv, axis=-1, keepdims=True), sk)
                e = jnp.exp(sv - m)
                den = jnp.sum(e, axis=-1, keepdims=True) + jnp.exp(sk - m)
                o = _dot(e.astype(BF16), vpl) * (1.0 / den)
                acc[r] = o if acc[r] is None else acc[r] + o
        for r in range(2):
            c = 2 * kh + r
            o_ref[0, :, c * LANES:(c + 1) * LANES] = acc[r].astype(BF16)


def _attention(q, kk, vv, sink):
    b, tot = q.shape[0], q.shape[1]
    seq = tot - CTX
    nblk = seq // BLOCK
    off = CTX // BLOCK
    kv = lambda f: pl.BlockSpec((1, BLOCK, 2 * LANES), lambda bb, n: (bb, f(n) + off, 0))
    prv = kv(lambda n: jnp.maximum(n - 1, 0))
    cur = kv(lambda n: n)
    nxt = kv(lambda n: jnp.minimum(n + 1, nblk - 1))
    cx = pl.BlockSpec((1, CTX, 2 * LANES), lambda bb, n: (bb, 0, 0))
    return pl.pallas_call(
        _attn_kernel,
        grid=(b, nblk),
        in_specs=[pl.BlockSpec(memory_space=pltpu.SMEM),
                  pl.BlockSpec((1, BLOCK, ATTN_WIDTH), lambda bb, n: (bb, n + off, 0)),
                  prv, cur, nxt, cx, prv, cur, nxt, cx],
        out_specs=pl.BlockSpec((1, BLOCK, ATTN_WIDTH), lambda bb, n: (bb, n, 0)),
        out_shape=jax.ShapeDtypeStruct((b, seq, ATTN_WIDTH), BF16),
        compiler_params=_params("arbitrary", "arbitrary"),
        name="attn",
    )(sink, q, kk, kk, kk, kk, vv, vv, vv, vv)


def _ssd_chunk_of_step(t, nchunks):
    nctx = CTX // CHUNK
    i = t % nchunks
    bwd = jnp.where(i < nctx, nctx - 1 - i, nchunks + nctx - 1 - i)
    return jnp.where(t < nchunks, i, bwd)


def _ssd_kernel(xm_ref, xp_ref, xn_ref, dt_ref, z_ref, cw_ref, cb_ref, dtb_ref, alog_ref, dsk_ref, g_ref,
                y_ref, h_scr, yacc_scr):
    t = pl.program_id(1)
    nchunks = pl.num_programs(1) // 2
    nctx = CTX // CHUNK
    i = t % nchunks
    chunk = _ssd_chunk_of_step(t, nchunks)
    is_bwd = t >= nchunks

    @pl.when(i == 0)
    def _():
        h_scr[...] = jnp.zeros_like(h_scr)

    x = xm_ref[0].astype(F32)
    has_prev = (chunk != 0) & (chunk != nctx)
    has_next = (chunk != nctx - 1) & (chunk != nchunks - 1)
    prow = jnp.where(has_prev, xp_ref[0, HALO - 1:HALO, :].astype(F32), 0.0)
    nrow = jnp.where(has_next, xn_ref[0, 0:1, :].astype(F32), 0.0)
    rows = lax.broadcasted_iota(jnp.int32, (CHUNK, 1), 0)
    x_up = jnp.where(rows == 0, prow, pltpu.roll(x, 1, 0))
    x_dn = jnp.where(rows == CHUNK - 1, nrow, pltpu.roll(x, CHUNK - 1, 0))
    u = cw_ref[0:1, :] * x_up + cw_ref[1:2, :] * x + cw_ref[2:3, :] * x_dn + cb_ref[...]
    u = u * _sigmoid(u)
    xs = u[:, 0:SSM_WIDTH]
    bmat = u[:, SSM_WIDTH:SSM_WIDTH + SSM_GROUPS * D_STATE]
    cmat = u[:, SSM_WIDTH + SSM_GROUPS * D_STATE:XBC_WIDTH]

    dt_raw = dt_ref[0]
    dt_raw = jnp.where(is_bwd, pltpu.roll(dt_raw, LANES - SSM_HEADS, 1), dt_raw)
    dt_bias = jnp.where(is_bwd, dtb_ref[1:2, :], dtb_ref[0:1, :])
    a_log = jnp.where(is_bwd, alog_ref[1:2, :], alog_ref[0:1, :])
    lane_row = lax.broadcasted_iota(jnp.int32, (1, LANES), 1)
    a_neg = jnp.where(lane_row < SSM_HEADS, -jnp.exp(a_log), 0.0)
    dtv = _softplus(dt_raw + dt_bias)
    a = dtv * a_neg
    ii = lax.broadcasted_iota(jnp.int32, (CHUNK, CHUNK), 0)
    jj = lax.broadcasted_iota(jnp.int32, (CHUNK, CHUNK), 1)
    causal = jnp.where(is_bwd, jj - ii, ii - jj) >= 0
    acum = jnp.dot(causal.astype(F32), a, preferred_element_type=F32, precision=lax.Precision.HIGHEST)
    acum_t = acum.T

    half0 = lax.broadcasted_iota(jnp.int32, (CHUNK, LANES), 1) < SSM_HEAD_DIM
    npairs = SSM_HEADS // 2
    arg_pair, dt_pair = [], []
    for m in range(npairs):
        cols = [jnp.broadcast_to(acum[:, h:h + 1], (CHUNK, LANES)) for h in (2 * m, 2 * m + 1)]
        dts = [jnp.broadcast_to(dtv[:, h:h + 1], (CHUNK, LANES)) for h in (2 * m, 2 * m + 1)]
        arg_pair.append(jnp.where(half0, cols[0], cols[1]))
        dt_pair.append(jnp.where(half0, dts[0], dts[1]))
    tot_pair = [jnp.where(is_bwd, ap[0:1, :], ap[CHUNK - 1:CHUNK, :]) for ap in arg_pair]
    xdt_pair = [xs[:, m * LANES:(m + 1) * LANES] * dt_pair[m] for m in range(npairs)]
    xdtd_pair = [xdt_pair[m] * jnp.exp(tot_pair[m] - arg_pair[m]) for m in range(npairs)]

    cb16 = [cmat[:, g * D_STATE:(g + 1) * D_STATE].astype(BF16) for g in range(SSM_GROUPS)]
    pairs_per_group = npairs // SSM_GROUPS
    group_cols = pairs_per_group * LANES
    h_start = [h_scr[:, g * group_cols:(g + 1) * group_cols] for g in range(SSM_GROUPS)]

    for g in range(SSM_GROUPS):
        bt = bmat[:, g * D_STATE:(g + 1) * D_STATE].T.astype(BF16)
        pr = range(g * pairs_per_group, (g + 1) * pairs_per_group)
        rhs = jnp.concatenate([xdtd_pair[m] for m in pr], axis=1).astype(BF16)
        decay = jnp.concatenate([jnp.exp(tot_pair[m]) for m in pr], axis=1)
        h_scr[:, g * group_cols:(g + 1) * group_cols] = h_start[g] * decay + _dot(bt, rhs)

    @pl.when(chunk >= nctx)
    def _():
        ys = []
        for g in range(SSM_GROUPS):
            bb16 = bmat[:, g * D_STATE:(g + 1) * D_STATE].astype(BF16)
            gmat = _dot_nt(cb16[g], bb16)
            yoff = _dot(cb16[g], h_start[g].astype(BF16))
            for mm in range(pairs_per_group):
                m = g * pairs_per_group + mm
                smats = []
                for h in (2 * m, 2 * m + 1):
                    seg = jnp.broadcast_to(acum[:, h:h + 1], (CHUNK, CHUNK)) - acum_t[h:h + 1, :]
                    lmat = jnp.where(causal, jnp.exp(jnp.where(causal, seg, 0.0)), 0.0)
                    smats.append((gmat * lmat).astype(BF16))
                lhs = jnp.concatenate(smats, axis=1)
                xd = xdt_pair[m]
                rhs = jnp.concatenate([jnp.where(half0, xd, 0.0), jnp.where(half0, 0.0, xd)], axis=0).astype(BF16)
                ydiag = _dot(lhs, rhs)
                ys.append(ydiag + yoff[:, mm * LANES:(mm + 1) * LANES] * jnp.exp(arg_pair[m]))
        y = jnp.concatenate(ys, axis=1)
        r0 = pl.multiple_of((chunk - nctx) * CHUNK, CHUNK)

        @pl.when(jnp.logical_not(is_bwd))
        def _():
            yacc_scr[pl.ds(r0, CHUNK), :] = y + dsk_ref[...] * xs

        @pl.when(is_bwd)
        def _():
            zf = z_ref[0].astype(F32)
            yt = (yacc_scr[pl.ds(r0, CHUNK), :] + y) * (zf * _sigmoid(zf))
            gw = SSM_WIDTH // SSM_GROUPS
            for g in range(SSM_GROUPS):
                yg = yt[:, g * gw:(g + 1) * gw]
                ms = jnp.mean(yg * yg, axis=-1, keepdims=True)
                y_ref[0, :, g * gw:(g + 1) * gw] = (yg * lax.rsqrt(ms + EPS)
                                                    * g_ref[:, g * gw:(g + 1) * gw]).astype(BF16)


def _ssd(xbc, dt, z, conv_w, conv_b, dtb2, alog2, dsk, g_ssm):
    b, tot = xbc.shape[0], xbc.shape[1]
    seq = tot - CTX
    nchunks = tot // CHUNK
    nctx = CTX // CHUNK
    hpc = CHUNK // HALO
    nhalo = tot // HALO
    ck = lambda t: _ssd_chunk_of_step(t, nchunks)
    last_lat = nchunks - 1
    zy_chunk = lambda t: jnp.where(t >= nchunks + nctx, ck(t), last_lat)
    const = lambda *shape: pl.BlockSpec(shape, lambda bb, t: (0,) * len(shape))
    return pl.pallas_call(
        _ssd_kernel,
        grid=(b, 2 * nchunks),
        in_specs=[pl.BlockSpec((1, CHUNK, XBC_WIDTH), lambda bb, t: (bb, ck(t), 0)),
                  pl.BlockSpec((1, HALO, XBC_WIDTH), lambda bb, t: (bb, jnp.maximum(ck(t) * hpc - 1, 0), 0)),
                  pl.BlockSpec((1, HALO, XBC_WIDTH), lambda bb, t: (bb, jnp.minimum(ck(t) * hpc + hpc, nhalo - 1), 0)),
                  pl.BlockSpec((1, CHUNK, LANES), lambda bb, t: (bb, ck(t), 0)),
                  pl.BlockSpec((1, CHUNK, SSM_WIDTH), lambda bb, t: (bb, zy_chunk(t), 0)),
                  const(3, XBC_WIDTH), const(1, XBC_WIDTH), const(2, LANES), const(2, LANES),
                  const(1, SSM_WIDTH), const(1, SSM_WIDTH)],
        out_specs=pl.BlockSpec((1, CHUNK, SSM_WIDTH), lambda bb, t: (bb, zy_chunk(t) - nctx, 0)),
        out_shape=jax.ShapeDtypeStruct((b, seq, SSM_WIDTH), BF16),
        scratch_shapes=[pltpu.VMEM((D_STATE, SSM_WIDTH), F32), pltpu.VMEM((seq, SSM_WIDTH), F32)],
        compiler_params=_params("arbitrary", "arbitrary"),
        name="ssd",
    )(xbc, xbc, xbc, dt, z, conv_w, conv_b, dtb2, alog2, dsk, g_ssm)


def _outproj_kernel(a_ref, y_ref, x_ref, w_ref, mod_ref, g_ref, xn_ref, h2_ref):
    o = _dot(a_ref[0], w_ref[0:ATTN_WIDTH, :]) + _dot(y_ref[0], w_ref[ATTN_WIDTH:D_MODEL, :])
    gate = mod_ref[0, :, 2 * D_MODEL:3 * D_MODEL]
    xn = x_ref[0] + gate * o
    xn_ref[0] = xn
    ms = jnp.mean(xn * xn, axis=-1, keepdims=True)
    hn = xn * lax.rsqrt(ms + EPS) * g_ref[...]
    shift = mod_ref[0, :, 3 * D_MODEL:4 * D_MODEL]
    scale = mod_ref[0, :, 4 * D_MODEL:5 * D_MODEL]
    h2_ref[0] = (hn * (1.0 + scale) + shift).astype(BF16)


def _outproj(attn, y, x, w_out16, mod3, g_ffn):
    b, seq = x.shape[0], x.shape[1]
    row = lambda w: pl.BlockSpec((1, TM_OUT, w), lambda bb, i: (bb, i, 0))
    return pl.pallas_call(
        _outproj_kernel,
        grid=(b, seq // TM_OUT),
        in_specs=[row(ATTN_WIDTH), row(SSM_WIDTH), row(D_MODEL),
                  pl.BlockSpec((D_MODEL, D_MODEL), lambda bb, i: (0, 0)),
                  pl.BlockSpec((1, 1, 6 * D_MODEL), lambda bb, i: (bb, 0, 0)),
                  pl.BlockSpec((1, D_MODEL), lambda bb, i: (0, 0))],
        out_specs=[row(D_MODEL), row(D_MODEL)],
        out_shape=[jax.ShapeDtypeStruct((b, seq, D_MODEL), F32), jax.ShapeDtypeStruct((b, seq, D_MODEL), BF16)],
        compiler_params=_params("arbitrary", "arbitrary"),
        name="outproj",
    )(attn, y, x, w_out16, mod3, g_ffn.reshape(1, -1))


def _ffn_kernel(hm_ref, hp_ref, hn_ref, xn_ref, wa_ref, wg_ref, cwa_ref, cwg_ref, cba_ref, cbg_ref, wd_ref,
                mod_ref, o_ref, lhs_scr, acc_scr):
    i = pl.program_id(1)
    j = pl.program_id(2)
    tm = hm_ref.shape[1]

    @pl.when(j == 0)
    def _():
        lhs_scr[0:HALO, :] = jnp.where(i > 0, hp_ref[0], jnp.zeros_like(hp_ref[0]))
        lhs_scr[HALO:HALO + tm, :] = hm_ref[0]
        lhs_scr[HALO + tm:, :] = jnp.where(i < pl.num_programs(1) - 1, hn_ref[0], jnp.zeros_like(hn_ref[0]))
        acc_scr[...] = jnp.zeros_like(acc_scr)

    lhs = lhs_scr[...]
    nrows = lhs.shape[0]

    def conv(w_ref, cw_ref, cb_ref):
        uu = _dot(lhs, w_ref[...])
        up = pltpu.roll(uu, 1, 0)[HALO:HALO + tm]
        dn = pltpu.roll(uu, nrows - 1, 0)[HALO:HALO + tm]
        return cw_ref[0:1, :] * up + cw_ref[1:2, :] * uu[HALO:HALO + tm] + cw_ref[2:3, :] * dn + cb_ref[...]

    ua = conv(wa_ref, cwa_ref, cba_ref)
    ug = conv(wg_ref, cwg_ref, cbg_ref)
    act = (ua * (ug * _sigmoid(ug))).astype(BF16)
    acc_scr[...] += _dot(act, wd_ref[...])

    @pl.when(j == pl.num_programs(2) - 1)
    def _():
        o_ref[0] = xn_ref[0] + mod_ref[0, :, 5 * D_MODEL:6 * D_MODEL] * acc_scr[...]


def _ffn(h2, xn, w_up16, conv_w, conv_b, w_down16, mod3):
    b, seq = xn.shape[0], xn.shape[1]
    nj = D_FF // TF_FFN
    ni = seq // TM_FFN
    hpt = TM_FFN // HALO
    nhalo = seq // HALO
    cb = conv_b.reshape(1, -1)
    return pl.pallas_call(
        _ffn_kernel,
        grid=(b, ni, nj),
        in_specs=[pl.BlockSpec((1, TM_FFN, D_MODEL), lambda bb, i, j: (bb, i, 0)),
                  pl.BlockSpec((1, HALO, D_MODEL), lambda bb, i, j: (bb, jnp.maximum(i * hpt - 1, 0), 0)),
                  pl.BlockSpec((1, HALO, D_MODEL), lambda bb, i, j: (bb, jnp.minimum(i * hpt + hpt, nhalo - 1), 0)),
                  pl.BlockSpec((1, TM_FFN, D_MODEL), lambda bb, i, j: (bb, i, 0)),
                  pl.BlockSpec((D_MODEL, TF_FFN), lambda bb, i, j: (0, j)),
                  pl.BlockSpec((D_MODEL, TF_FFN), lambda bb, i, j: (0, nj + j)),
                  pl.BlockSpec((3, TF_FFN), lambda bb, i, j: (0, j)),
                  pl.BlockSpec((3, TF_FFN), lambda bb, i, j: (0, nj + j)),
                  pl.BlockSpec((1, TF_FFN), lambda bb, i, j: (0, j)),
                  pl.BlockSpec((1, TF_FFN), lambda bb, i, j: (0, nj + j)),
                  pl.BlockSpec((TF_FFN, D_MODEL), lambda bb, i, j: (j, 0)),
                  pl.BlockSpec((1, 1, 6 * D_MODEL), lambda bb, i, j: (bb, 0, 0))],
        out_specs=pl.BlockSpec((1, TM_FFN, D_MODEL), lambda bb, i, j: (bb, i, 0)),
        out_shape=jax.ShapeDtypeStruct((b, seq, D_MODEL), F32),
        scratch_shapes=[pltpu.VMEM((TM_FFN + 2 * HALO, D_MODEL), BF16), pltpu.VMEM((TM_FFN, D_MODEL), F32)],
        compiler_params=_params("arbitrary", "arbitrary", "arbitrary"),
        name="ffn",
    )(h2, h2, h2, xn, w_up16, w_up16, conv_w, conv_w, cb, cb, w_down16, mod3)


def _layer(x, c, ctx, c_ctx, w_mod, b_mod, g_mix, w_in, g_q, g_k, sink, ssm_conv_w, ssm_conv_b,
           a_log, dt_bias, d_skip, g_ssm, w_out, g_ffn, w_up, ffn_conv_w, ffn_conv_b, w_down):
    b = x.shape[0]
    mod_rows = 8 * ((b + 1 + 7) // 8)
    cc = jnp.concatenate([c, c_ctx[None, :], jnp.zeros((mod_rows - b - 1, D_MODEL), F32)], axis=0)
    mod3 = _modulation(cc, w_mod, b_mod).reshape(mod_rows, 1, 6 * D_MODEL)

    w_in_p = jnp.pad(w_in, ((0, 0), (0, IN_COLS_PAD - IN_COLS))).astype(BF16)
    q, kk, vv, z, xbc, dt = _inproj(x, ctx, mod3, g_mix, w_in_p, g_q, g_k)
    attn = _attention(q, kk, vv, sink)

    pad_heads = lambda t: jnp.pad(t, ((0, 0), (0, LANES - SSM_HEADS)))
    y = _ssd(xbc, dt, z, ssm_conv_w, ssm_conv_b.reshape(1, -1), pad_heads(dt_bias), pad_heads(a_log),
             jnp.repeat(d_skip, SSM_HEAD_DIM).reshape(1, -1), g_ssm.reshape(1, -1))

    xn, h2 = _outproj(attn, y, x, w_out.astype(BF16), mod3, g_ffn)
    return _ffn(h2, xn, w_up.astype(BF16), ffn_conv_w, ffn_conv_b, w_down.astype(BF16), mod3)


def kernel(x, c, ctx, c_ctx, w_mod, b_mod, g_mix, w_in, g_q, g_k, sink, ssm_conv_w, ssm_conv_b,
           a_log, dt_bias, d_skip, g_ssm, w_out, g_ffn, w_up, ffn_conv_w, ffn_conv_b, w_down):
    assert w_mod.shape[0] == 1, "single-layer block"
    first = lambda t: t[0]
    return _layer(x, c, ctx, c_ctx, first(w_mod), first(b_mod), first(g_mix), first(w_in), first(g_q),
                  first(g_k), first(sink), first(ssm_conv_w), first(ssm_conv_b), first(a_log),
                  first(dt_bias), first(d_skip), first(g_ssm), first(w_out), first(g_ffn), first(w_up),
                  first(ffn_conv_w), first(ffn_conv_b), first(w_down))
```

```python
import functools

import jax
import jax.numpy as jnp
from jax import lax
from jax.experimental import pallas as pl
from jax.experimental.pallas import tpu as pltpu

F32 = jnp.float32
BF16 = jnp.bfloat16

D_MODEL = 1024
CTX = 256
GRID_W = 64
HEAD_DIM = 64
ATTN_HEADS = 8
KV_HEADS = 2
ATTN_WIDTH = ATTN_HEADS * HEAD_DIM
KV_WIDTH = KV_HEADS * HEAD_DIM
WINDOW = 128
BLOCK = 128
ROPE_BASE = 10000.0
SSM_WIDTH = D_MODEL - ATTN_WIDTH
SSM_HEAD_DIM = 64
SSM_HEADS = SSM_WIDTH // SSM_HEAD_DIM
SSM_GROUPS = 2
D_STATE = 128
XBC_WIDTH = SSM_WIDTH + 2 * SSM_GROUPS * D_STATE
CHUNK = 128
D_FF = 2816
IN_COLS = ATTN_WIDTH + 2 * KV_WIDTH + SSM_WIDTH + XBC_WIDTH + 2 * SSM_HEADS
EPS = 1e-6
NEG_INF = -1e30

LANES = 128
BF16_SUBLANES = 16
IN_COLS_PAD = 19 * LANES
COL_Q, COL_K, COL_V = 0, ATTN_WIDTH, ATTN_WIDTH + KV_WIDTH
COL_Z = ATTN_WIDTH + 2 * KV_WIDTH
COL_XBC = COL_Z + SSM_WIDTH
COL_DT = COL_XBC + XBC_WIDTH
VMEM_LIMIT = 56 * 1024 * 1024

TM_IN = 256
TM_OUT = 512
TM_FFN = 512
TF_FFN = 1408
HALO = BF16_SUBLANES


def _params(*sem):
    return pltpu.CompilerParams(dimension_semantics=sem, vmem_limit_bytes=VMEM_LIMIT)


def _sigmoid(v):
    return 1.0 / (1.0 + jnp.exp(-v))


def _softplus(v):
    return jnp.maximum(v, 0.0) + jnp.log1p(jnp.exp(-jnp.abs(v)))


def _dot(a, b):
    return jnp.dot(a, b, preferred_element_type=F32)


def _dot_nt(a, b):
    return lax.dot_general(a, b, (((1,), (1,)), ((), ())), preferred_element_type=F32)


def _mod_kernel(c_ref, w_ref, b_ref, o_ref):
    cv = c_ref[...]
    s = (cv * _sigmoid(cv)).astype(BF16)
    o_ref[...] = _dot(s, w_ref[...].astype(BF16)) + b_ref[...]


def _modulation(cc, w_mod, b_mod):
    rows = cc.shape[0]
    tn = 1024
    return pl.pallas_call(
        _mod_kernel,
        grid=(6 * D_MODEL // tn,),
        in_specs=[pl.BlockSpec((rows, D_MODEL), lambda j: (0, 0)),
                  pl.BlockSpec((D_MODEL, tn), lambda j: (0, j)),
                  pl.BlockSpec((1, tn), lambda j: (0, j))],
        out_specs=pl.BlockSpec((rows, tn), lambda j: (0, j)),
        out_shape=jax.ShapeDtypeStruct((rows, 6 * D_MODEL), F32),
        compiler_params=_params("arbitrary"),
        name="mod",
    )(cc, w_mod, b_mod.reshape(1, -1))


def _inproj_kernel(x_ref, ctx_ref, mod_ref, gmix_ref, w_ref, gq_ref, gk_ref,
                   cos_ref, sina_ref, sinb_ref, bd_ref,
                   q_ref, k_ref, v_ref, z_ref, xbc_ref, dt_ref):
    s = pl.program_id(1)
    xin = jnp.where(s == 0, ctx_ref[0], x_ref[0])
    ms = jnp.mean(xin * xin, axis=-1, keepdims=True)
    hn = xin * lax.rsqrt(ms + EPS) * gmix_ref[...]
    shift = mod_ref[0, :, 0:D_MODEL]
    scale = mod_ref[0, :, D_MODEL:2 * D_MODEL]
    h = (hn * (1.0 + scale) + shift).astype(BF16)

    cos = cos_ref[...]
    sina = sina_ref[...]
    sinb = sinb_ref[...]
    bd = bd_ref[...]

    def norm_rope(pf, gain):
        ss = pf * pf
        hi = ss.astype(BF16)
        lo = (ss - hi.astype(F32)).astype(BF16)
        sums = _dot(hi, bd) + _dot(lo, bd)
        y = pf * lax.rsqrt(sums * (1.0 / HEAD_DIM) + EPS) * gain
        return y * cos + pltpu.roll(y, LANES - 16, 1) * sina + pltpu.roll(y, 16, 1) * sinb

    for cidx in range(ATTN_WIDTH // LANES):
        pf = _dot(h, w_ref[:, COL_Q + cidx * LANES:COL_Q + (cidx + 1) * LANES])
        q_ref[0, :, cidx * LANES:(cidx + 1) * LANES] = norm_rope(pf, gq_ref[...]).astype(BF16)

    kf = norm_rope(_dot(h, w_ref[:, COL_K:COL_K + KV_WIDTH]), gk_ref[...])
    k_ref[0, :, 0:LANES] = kf.astype(BF16)
    k_ref[0, :, LANES:2 * LANES] = pltpu.roll(kf, HEAD_DIM, 1).astype(BF16)
    vf = _dot(h, w_ref[:, COL_V:COL_V + KV_WIDTH])
    v_ref[0, :, 0:LANES] = vf.astype(BF16)
    v_ref[0, :, LANES:2 * LANES] = pltpu.roll(vf, HEAD_DIM, 1).astype(BF16)
    z_ref[0] = _dot(h, w_ref[:, COL_Z:COL_Z + SSM_WIDTH]).astype(BF16)
    xbc_ref[0] = _dot(h, w_ref[:, COL_XBC:COL_XBC + XBC_WIDTH]).astype(BF16)
    dt_ref[0] = _dot(h, w_ref[:, COL_DT:COL_DT + LANES])


def _rope_tables(seq):
    rows = seq // GRID_W
    pos_r = jnp.repeat(jnp.arange(rows), GRID_W).astype(F32)
    pos_c = jnp.tile(jnp.arange(GRID_W), rows).astype(F32)
    quarter = HEAD_DIM // 4
    freqs = ROPE_BASE ** (-jnp.arange(quarter, dtype=F32) / quarter)
    ang_r = pos_r[:, None] * freqs[None, :]
    ang_c = pos_c[:, None] * freqs[None, :]
    cr, sr, cc, sc = jnp.cos(ang_r), jnp.sin(ang_r), jnp.cos(ang_c), jnp.sin(ang_c)
    zero = jnp.zeros_like(sr)
    cos = jnp.concatenate([cr, cr, cc, cc], axis=-1)
    sina = jnp.concatenate([-sr, zero, -sc, zero], axis=-1)
    sinb = jnp.concatenate([zero, sr, zero, sc], axis=-1)
    reps = LANES // HEAD_DIM
    pad = lambda t, v: jnp.concatenate([jnp.full((CTX, LANES), v, F32), jnp.tile(t, (1, reps))], axis=0)
    return pad(cos, 1.0), pad(sina, 0.0), pad(sinb, 0.0)


def _inproj(x, ctx, mod3, g_mix, w_in_p, g_q, g_k):
    b, seq = x.shape[0], x.shape[1]
    tot = CTX + seq
    cos, sina, sinb = _rope_tables(seq)
    reps = LANES // HEAD_DIM
    gq = (jnp.tile(g_q, reps) * (HEAD_DIM ** -0.5)).reshape(1, LANES)
    gk = jnp.tile(g_k, reps).reshape(1, LANES)
    lane = jnp.arange(LANES)
    bd = (lane[:, None] // HEAD_DIM == lane[None, :] // HEAD_DIM).astype(BF16)
    nsteps = tot // TM_IN
    const = lambda *shape: pl.BlockSpec(shape, lambda bb, s: (0,) * len(shape))
    tab = pl.BlockSpec((TM_IN, LANES), lambda bb, s: (s, 0))
    out = lambda w: pl.BlockSpec((1, TM_IN, w), lambda bb, s: (bb, s, 0))
    shp = lambda w, dt: jax.ShapeDtypeStruct((b, tot, w), dt)
    return pl.pallas_call(
        _inproj_kernel,
        grid=(b, nsteps),
        in_specs=[pl.BlockSpec((1, TM_IN, D_MODEL), lambda bb, s: (bb, jnp.maximum(s - 1, 0), 0)),
                  pl.BlockSpec((1, CTX, D_MODEL), lambda bb, s: (bb, 0, 0)),
                  pl.BlockSpec((1, 1, 6 * D_MODEL), lambda bb, s: (jnp.where(s == 0, b, bb), 0, 0)),
                  const(1, D_MODEL), const(D_MODEL, IN_COLS_PAD), const(1, LANES), const(1, LANES),
                  tab, tab, tab, const(LANES, LANES)],
        out_specs=[out(ATTN_WIDTH), out(2 * LANES), out(2 * LANES), out(SSM_WIDTH), out(XBC_WIDTH), out(LANES)],
        out_shape=[shp(ATTN_WIDTH, BF16), shp(2 * LANES, BF16), shp(2 * LANES, BF16),
                   shp(SSM_WIDTH, BF16), shp(XBC_WIDTH, BF16), shp(LANES, F32)],
        compiler_params=_params("arbitrary", "arbitrary"),
        name="inproj",
    )(x, ctx, mod3, g_mix.reshape(1, -1), w_in_p, gq, gk, cos, sina, sinb, bd)


def _attn_kernel(sink_ref, q_ref, kp_ref, kc_ref, kn_ref, kx_ref, vp_ref, vc_ref, vn_ref, vx_ref, o_ref):
    nb = pl.program_id(1)
    nblk = pl.num_programs(1)
    kall = jnp.concatenate([kp_ref[0], kc_ref[0], kn_ref[0], kx_ref[0]], axis=0)
    vall = jnp.concatenate([vp_ref[0], vc_ref[0], vn_ref[0], vx_ref[0]], axis=0)
    nkeys = kall.shape[0]

    qi = lax.broadcasted_iota(jnp.int32, (BLOCK, nkeys), 0)
    kj = lax.broadcasted_iota(jnp.int32, (BLOCK, nkeys), 1)
    rel = kj - BLOCK - qi
    far = 4 * BLOCK
    lo = jnp.where(nb > 0, -WINDOW, far)
    hi = jnp.where(nb < nblk - 1, WINDOW, -far)
    zero = jnp.zeros((BLOCK, nkeys), F32)
    bias_prev = jnp.where(rel >= lo, zero, NEG_INF)
    bias_next = jnp.where(rel <= hi, zero, NEG_INF)
    bias = jnp.where(kj < BLOCK, bias_prev, jnp.where(kj < 2 * BLOCK, zero, jnp.where(kj < 3 * BLOCK, bias_next, zero)))

    lane = lax.broadcasted_iota(jnp.int32, (nkeys, LANES), 1)
    q = q_ref[0]
    for kh in range(KV_HEADS):
        q2 = jnp.concatenate([q[:, (2 * kh) * LANES:(2 * kh + 1) * LANES],
                              q[:, (2 * kh + 1) * LANES:(2 * kh + 2) * LANES]], axis=0)
        acc = [None, None]
        for par in range(2):
            src = 0 if par == kh else 1
            half = (lane >= HEAD_DIM) if par else (lane < HEAD_DIM)
            kpl = jnp.where(half, kall[:, src * LANES:(src + 1) * LANES], jnp.zeros_like(kall[:, :LANES]))
            vpl = jnp.where(half, vall[:, src * LANES:(src + 1) * LANES], jnp.zeros_like(vall[:, :LANES]))
            s2 = _dot_nt(q2, kpl)
            for r in range(2):
                head = 4 * kh + par + 2 * r
                sv = s2[r * BLOCK:(r + 1) * BLOCK] + bias
                sk = sink_ref[head]
                m = jnp.maximum(jnp.max(sv, axis=-1, keepdims=True), sk)
                e = jnp.exp(sv - m)
                den = jnp.sum(e, axis=-1, keepdims=True) + jnp.exp(sk - m)
                o = _dot(e.astype(BF16), vpl) * (1.0 / den)
                acc[r] = o if acc[r] is None else acc[r] + o
        for r in range(2):
            c = 2 * kh + r
            o_ref[0, :, c * LANES:(c + 1) * LANES] = acc[r].astype(BF16)


def _attention(q, kk, vv, sink):
    b, tot = q.shape[0], q.shape[1]
    seq = tot - CTX
    nblk = seq // BLOCK
    off = CTX // BLOCK
    kv = lambda f: pl.BlockSpec((1, BLOCK, 2 * LANES), lambda bb, n: (bb, f(n) + off, 0))
    prv = kv(lambda n: jnp.maximum(n - 1, 0))
    cur = kv(lambda n: n)
    nxt = kv(lambda n: jnp.minimum(n + 1, nblk - 1))
    cx = pl.BlockSpec((1, CTX, 2 * LANES), lambda bb, n: (bb, 0, 0))
    return pl.pallas_call(
        _attn_kernel,
        grid=(b, nblk),
        in_specs=[pl.BlockSpec(memory_space=pltpu.SMEM),
                  pl.BlockSpec((1, BLOCK, ATTN_WIDTH), lambda bb, n: (bb, n + off, 0)),
                  prv, cur, nxt, cx, prv, cur, nxt, cx],
        out_specs=pl.BlockSpec((1, BLOCK, ATTN_WIDTH), lambda bb, n: (bb, n, 0)),
        out_shape=jax.ShapeDtypeStruct((b, seq, ATTN_WIDTH), BF16),
        compiler_params=_params("arbitrary", "arbitrary"),
        name="attn",
    )(sink, q, kk, kk, kk, kk, vv, vv, vv, vv)


def _ssd_chunk_of_step(t, nchunks):
    nctx = CTX // CHUNK
    i = t % nchunks
    bwd = jnp.where(i < nctx, nctx - 1 - i, nchunks + nctx - 1 - i)
    return jnp.where(t < nchunks, i, bwd)


def _ssd_kernel(xm_ref, xp_ref, xn_ref, dt_ref, z_ref, cw_ref, cb_ref, dtb_ref, alog_ref, dsk_ref, g_ref, tri_ref,
                y_ref, hf_scr, hb_scr, yacc_scr, stb_scr, decb_scr, eb_scr, cbuf_scr):
    t = pl.program_id(1)
    nchunks = pl.num_programs(1) // 2
    nctx = CTX // CHUNK
    npairs = SSM_HEADS // 2
    pairs_per_group = npairs // SSM_GROUPS
    group_cols = pairs_per_group * LANES

    @pl.when(t == 0)
    def _():
        hf_scr[...] = jnp.zeros_like(hf_scr)
        hb_scr[...] = jnp.zeros_like(hb_scr)

    @pl.when(t < nchunks)
    def _():
        chunk = t
        is_lat = chunk >= nctx
        lc = jnp.maximum(chunk - nctx, 0)

        x = xm_ref[0].astype(F32)
        has_prev = (chunk != 0) & (chunk != nctx)
        has_next = (chunk != nctx - 1) & (chunk != nchunks - 1)
        prow = jnp.where(has_prev, xp_ref[0, HALO - 1:HALO, :].astype(F32), 0.0)
        nrow = jnp.where(has_next, xn_ref[0, 0:1, :].astype(F32), 0.0)
        rows = lax.broadcasted_iota(jnp.int32, (CHUNK, 1), 0)
        x_up = jnp.where(rows == 0, prow, pltpu.roll(x, 1, 0))
        x_dn = jnp.where(rows == CHUNK - 1, nrow, pltpu.roll(x, CHUNK - 1, 0))
        u = cw_ref[0:1, :] * x_up + cw_ref[1:2, :] * x + cw_ref[2:3, :] * x_dn + cb_ref[...]
        u = u * _sigmoid(u)
        xs = u[:, 0:SSM_WIDTH]
        bmat = u[:, SSM_WIDTH:SSM_WIDTH + SSM_GROUPS * D_STATE]
        cmat = u[:, SSM_WIDTH + SSM_GROUPS * D_STATE:XBC_WIDTH]
        cb16 = [cmat[:, g * D_STATE:(g + 1) * D_STATE].astype(BF16) for g in range(SSM_GROUPS)]
        bb16 = [bmat[:, g * D_STATE:(g + 1) * D_STATE].astype(BF16) for g in range(SSM_GROUPS)]
        bt16 = [bmat[:, g * D_STATE:(g + 1) * D_STATE].T.astype(BF16) for g in range(SSM_GROUPS)]

        nh2 = 2 * SSM_HEADS
        dtv = _softplus(dt_ref[0].T[0:nh2, :] + dtb_ref[...])
        a = dtv * (-jnp.exp(alog_ref[...]))
        a_hi = a.astype(BF16)
        r1 = a - a_hi.astype(F32)
        a_mid = r1.astype(BF16)
        a_lo = (r1 - a_mid.astype(F32)).astype(BF16)
        parts = jnp.concatenate([a_hi, a_mid, a_lo], axis=0)
        fold = lambda p: p[0:nh2] + p[nh2:2 * nh2] + p[2 * nh2:3 * nh2]
        prefix = fold(_dot(parts, tri_ref[0]))
        suffix = fold(_dot(parts, tri_ref[1]))
        head_row = lax.broadcasted_iota(jnp.int32, (nh2, CHUNK), 0)
        ac_t = jnp.where(head_row < SSM_HEADS, prefix, suffix)
        cols = jnp.concatenate([ac_t, dtv, jnp.zeros((CHUNK - 2 * nh2, CHUNK), F32)], axis=0).T

        half0 = lax.broadcasted_iota(jnp.int32, (CHUNK, LANES), 1) < SSM_HEAD_DIM
        ii = lax.broadcasted_iota(jnp.int32, (CHUNK, CHUNK), 0)
        jj = lax.broadcasted_iota(jnp.int32, (CHUNK, CHUNK), 1)
        causal = [jj <= ii, jj >= ii]
        last = [CHUNK - 1, 0]

        bcast = [jnp.broadcast_to(cols[:, lane:lane + 1], (CHUNK, LANES)) for lane in range(2 * nh2)]
        arg_pair, tot_pair, xdt_pair, xdtd_pair = [], [], [], []
        for d in range(2):
            args, tots, xdts, xdtds = [], [], [], []
            for m in range(npairs):
                r = d * SSM_HEADS + 2 * m
                arg = jnp.where(half0, bcast[r], bcast[r + 1])
                dtp = jnp.where(half0, bcast[nh2 + r], bcast[nh2 + r + 1])
                tot = arg[last[d]:last[d] + 1, :]
                xdt = xs[:, m * LANES:(m + 1) * LANES] * dtp
                args.append(arg)
                tots.append(tot)
                xdts.append(xdt)
                xdtds.append(xdt * jnp.exp(tot - arg))
            arg_pair.append(args)
            tot_pair.append(tots)
            xdt_pair.append(xdts)
            xdtd_pair.append(xdtds)

        hf_start = [hf_scr[:, g * group_cols:(g + 1) * group_cols] for g in range(SSM_GROUPS)]

        for g in range(SSM_GROUPS):
            pr = range(g * pairs_per_group, (g + 1) * pairs_per_group)
            gs = slice(g * group_cols, (g + 1) * group_cols)
            st_f = _dot(bt16[g], jnp.concatenate([xdtd_pair[0][m] for m in pr], axis=1).astype(BF16))
            st_b = _dot(bt16[g], jnp.concatenate([xdtd_pair[1][m] for m in pr], axis=1).astype(BF16))
            dec_f = jnp.concatenate([jnp.exp(tot_pair[0][m]) for m in pr], axis=1)
            dec_b = jnp.concatenate([jnp.exp(tot_pair[1][m]) for m in pr], axis=1)
            hf_scr[:, gs] = hf_start[g] * dec_f + st_f
            stb_scr[chunk, :, gs] = st_b
            decb_scr[chunk, :, gs] = jnp.broadcast_to(dec_b, (8, group_cols))

        @pl.when(is_lat)
        def _():
            r0 = pl.multiple_of(lc * CHUNK, CHUNK)
            cbuf_scr[lc] = jnp.concatenate(cb16, axis=1)
            for g in range(SSM_GROUPS):
                gmat = _dot_nt(cb16[g], bb16[g])
                yoff = _dot(cb16[g], hf_start[g].astype(BF16))
                for mm in range(pairs_per_group):
                    m = g * pairs_per_group + mm
                    ms = slice(m * LANES, (m + 1) * LANES)
                    smats, rhs = [], []
                    for d in range(2):
                        for hh in range(2):
                            r = d * SSM_HEADS + 2 * m + hh
                            seg = bcast[r] - ac_t[r:r + 1, :]
                            lmat = jnp.where(causal[d], jnp.exp(jnp.where(causal[d], seg, 0.0)), 0.0)
                            smats.append((gmat * lmat).astype(BF16))
                        xd = xdt_pair[d][m]
                        rhs += [jnp.where(half0, xd, 0.0), jnp.where(half0, 0.0, xd)]
                    ydiag = _dot(jnp.concatenate(smats, axis=1), jnp.concatenate(rhs, axis=0).astype(BF16))
                    yacc_scr[pl.ds(r0, CHUNK), ms] = (ydiag + yoff[:, mm * LANES:(mm + 1) * LANES]
                                                      * jnp.exp(arg_pair[0][m]) + dsk_ref[:, ms] * xs[:, ms])
                    eb_scr[lc, :, ms] = jnp.exp(arg_pair[1][m])

    @pl.when(t >= nchunks)
    def _():
        i = t - nchunks
        chunk = jnp.where(i < nctx, nctx - 1 - i, nchunks + nctx - 1 - i)
        hb = hb_scr[...]

        @pl.when(chunk >= nctx)
        def _():
            lc = chunk - nctx
            r0 = pl.multiple_of(lc * CHUNK, CHUNK)
            cb = cbuf_scr[lc]
            yoff = jnp.concatenate(
                [_dot(cb[:, g * D_STATE:(g + 1) * D_STATE], hb[:, g * group_cols:(g + 1) * group_cols].astype(BF16))
                 for g in range(SSM_GROUPS)], axis=1)
            zf = z_ref[0].astype(F32)
            yt = (yacc_scr[pl.ds(r0, CHUNK), :] + yoff * eb_scr[lc]) * (zf * _sigmoid(zf))
            gw = SSM_WIDTH // SSM_GROUPS
            for g in range(SSM_GROUPS):
                yg = yt[:, g * gw:(g + 1) * gw]
                ms = jnp.mean(yg * yg, axis=-1, keepdims=True)
                y_ref[0, :, g * gw:(g + 1) * gw] = (yg * lax.rsqrt(ms + EPS)
                                                    * g_ref[:, g * gw:(g + 1) * gw]).astype(BF16)

        hb_scr[...] = hb * decb_scr[chunk, 0:1, :] + stb_scr[chunk]


def _ssd(xbc, dt, z, conv_w, conv_b, dtb16, alog16, dsk, g_ssm):
    b, tot = xbc.shape[0], xbc.shape[1]
    seq = tot - CTX
    nchunks = tot // CHUNK
    nctx = CTX // CHUNK
    nlat = nchunks - nctx
    hpc = CHUNK // HALO
    nhalo = tot // HALO
    ck1 = lambda t: jnp.minimum(t, nchunks - 1)
    last_lat = nchunks - 1
    zy_chunk = lambda t: jnp.where(t >= nchunks + nctx, _ssd_chunk_of_step(t, nchunks), last_lat)
    const = lambda *shape: pl.BlockSpec(shape, lambda bb, t: (0,) * len(shape))
    jdx = jnp.arange(CHUNK)
    tri = jnp.stack([jdx[:, None] <= jdx[None, :], jdx[:, None] >= jdx[None, :]]).astype(BF16)
    return pl.pallas_call(
        _ssd_kernel,
        grid=(b, 2 * nchunks),
        in_specs=[pl.BlockSpec((1, CHUNK, XBC_WIDTH), lambda bb, t: (bb, ck1(t), 0)),
                  pl.BlockSpec((1, HALO, XBC_WIDTH), lambda bb, t: (bb, jnp.maximum(ck1(t) * hpc - 1, 0), 0)),
                  pl.BlockSpec((1, HALO, XBC_WIDTH), lambda bb, t: (bb, jnp.minimum(ck1(t) * hpc + hpc, nhalo - 1), 0)),
                  pl.BlockSpec((1, CHUNK, LANES), lambda bb, t: (bb, ck1(t), 0)),
                  pl.BlockSpec((1, CHUNK, SSM_WIDTH), lambda bb, t: (bb, zy_chunk(t), 0)),
                  const(3, XBC_WIDTH), const(1, XBC_WIDTH), const(2 * SSM_HEADS, CHUNK), const(2 * SSM_HEADS, CHUNK),
                  const(1, SSM_WIDTH), const(1, SSM_WIDTH), const(2, CHUNK, CHUNK)],
        out_specs=pl.BlockSpec((1, CHUNK, SSM_WIDTH), lambda bb, t: (bb, zy_chunk(t) - nctx, 0)),
        out_shape=jax.ShapeDtypeStruct((b, seq, SSM_WIDTH), BF16),
        scratch_shapes=[pltpu.VMEM((D_STATE, SSM_WIDTH), F32),
                        pltpu.VMEM((D_STATE, SSM_WIDTH), F32),
                        pltpu.VMEM((seq, SSM_WIDTH), F32),
                        pltpu.VMEM((nchunks, D_STATE, SSM_WIDTH), F32),
                        pltpu.VMEM((nchunks, 8, SSM_WIDTH), F32),
                        pltpu.VMEM((nlat, CHUNK, SSM_WIDTH), F32),
                        pltpu.VMEM((nlat, CHUNK, SSM_GROUPS * D_STATE), BF16)],
        compiler_params=_params("arbitrary", "arbitrary"),
        name="ssd",
    )(xbc, xbc, xbc, dt, z, conv_w, conv_b, dtb16, alog16, dsk, g_ssm, tri)


def _outproj_kernel(a_ref, y_ref, x_ref, w_ref, mod_ref, g_ref, xn_ref, h2_ref):
    o = _dot(a_ref[0], w_ref[0:ATTN_WIDTH, :]) + _dot(y_ref[0], w_ref[ATTN_WIDTH:D_MODEL, :])
    gate = mod_ref[0, :, 2 * D_MODEL:3 * D_MODEL]
    xn = x_ref[0] + gate * o
    xn_ref[0] = xn
    ms = jnp.mean(xn * xn, axis=-1, keepdims=True)
    hn = xn * lax.rsqrt(ms + EPS) * g_ref[...]
    shift = mod_ref[0, :, 3 * D_MODEL:4 * D_MODEL]
    scale = mod_ref[0, :, 4 * D_MODEL:5 * D_MODEL]
    h2_ref[0] = (hn * (1.0 + scale) + shift).astype(BF16)


def _outproj(attn, y, x, w_out16, mod3, g_ffn):
    b, seq = x.shape[0], x.shape[1]
    row = lambda w: pl.BlockSpec((1, TM_OUT, w), lambda bb, i: (bb, i, 0))
    return pl.pallas_call(
        _outproj_kernel,
        grid=(b, seq // TM_OUT),
        in_specs=[row(ATTN_WIDTH), row(SSM_WIDTH), row(D_MODEL),
                  pl.BlockSpec((D_MODEL, D_MODEL), lambda bb, i: (0, 0)),
                  pl.BlockSpec((1, 1, 6 * D_MODEL), lambda bb, i: (bb, 0, 0)),
                  pl.BlockSpec((1, D_MODEL), lambda bb, i: (0, 0))],
        out_specs=[row(D_MODEL), row(D_MODEL)],
        out_shape=[jax.ShapeDtypeStruct((b, seq, D_MODEL), F32), jax.ShapeDtypeStruct((b, seq, D_MODEL), BF16)],
        compiler_params=_params("arbitrary", "arbitrary"),
        name="outproj",
    )(attn, y, x, w_out16, mod3, g_ffn.reshape(1, -1))


def _ffn_kernel(hm_ref, hp_ref, hn_ref, xn_ref, wa_ref, wg_ref, cwa_ref, cwg_ref, cba_ref, cbg_ref, wd_ref,
                mod_ref, o_ref, lhs_scr, acc_scr):
    i = pl.program_id(1)
    j = pl.program_id(2)
    tm = hm_ref.shape[1]

    @pl.when(j == 0)
    def _():
        lhs_scr[0:HALO, :] = jnp.where(i > 0, hp_ref[0], jnp.zeros_like(hp_ref[0]))
        lhs_scr[HALO:HALO + tm, :] = hm_ref[0]
        lhs_scr[HALO + tm:, :] = jnp.where(i < pl.num_programs(1) - 1, hn_ref[0], jnp.zeros_like(hn_ref[0]))
        acc_scr[...] = jnp.zeros_like(acc_scr)

    lhs = lhs_scr[...]
    nrows = lhs.shape[0]

    def conv(w_ref, cw_ref, cb_ref):
        uu = _dot(lhs, w_ref[...])
        up = pltpu.roll(uu, 1, 0)[HALO:HALO + tm]
        dn = pltpu.roll(uu, nrows - 1, 0)[HALO:HALO + tm]
        return cw_ref[0:1, :] * up + cw_ref[1:2, :] * uu[HALO:HALO + tm] + cw_ref[2:3, :] * dn + cb_ref[...]

    ua = conv(wa_ref, cwa_ref, cba_ref)
    ug = conv(wg_ref, cwg_ref, cbg_ref)
    act = (ua * (ug * _sigmoid(ug))).astype(BF16)
    acc_scr[...] += _dot(act, wd_ref[...])

    @pl.when(j == pl.num_programs(2) - 1)
    def _():
        o_ref[0] = xn_ref[0] + mod_ref[0, :, 5 * D_MODEL:6 * D_MODEL] * acc_scr[...]


def _ffn(h2, xn, w_up16, conv_w, conv_b, w_down16, mod3):
    b, seq = xn.shape[0], xn.shape[1]
    nj = D_FF // TF_FFN
    ni = seq // TM_FFN
    hpt = TM_FFN // HALO
    nhalo = seq // HALO
    cb = conv_b.reshape(1, -1)
    return pl.pallas_call(
        _ffn_kernel,
        grid=(b, ni, nj),
        in_specs=[pl.BlockSpec((1, TM_FFN, D_MODEL), lambda bb, i, j: (bb, i, 0)),
                  pl.BlockSpec((1, HALO, D_MODEL), lambda bb, i, j: (bb, jnp.maximum(i * hpt - 1, 0), 0)),
                  pl.BlockSpec((1, HALO, D_MODEL), lambda bb, i, j: (bb, jnp.minimum(i * hpt + hpt, nhalo - 1), 0)),
                  pl.BlockSpec((1, TM_FFN, D_MODEL), lambda bb, i, j: (bb, i, 0)),
                  pl.BlockSpec((D_MODEL, TF_FFN), lambda bb, i, j: (0, j)),
                  pl.BlockSpec((D_MODEL, TF_FFN), lambda bb, i, j: (0, nj + j)),
                  pl.BlockSpec((3, TF_FFN), lambda bb, i, j: (0, j)),
                  pl.BlockSpec((3, TF_FFN), lambda bb, i, j: (0, nj + j)),
                  pl.BlockSpec((1, TF_FFN), lambda bb, i, j: (0, j)),
                  pl.BlockSpec((1, TF_FFN), lambda bb, i, j: (0, nj + j)),
                  pl.BlockSpec((TF_FFN, D_MODEL), lambda bb, i, j: (j, 0)),
                  pl.BlockSpec((1, 1, 6 * D_MODEL), lambda bb, i, j: (bb, 0, 0))],
        out_specs=pl.BlockSpec((1, TM_FFN, D_MODEL), lambda bb, i, j: (bb, i, 0)),
        out_shape=jax.ShapeDtypeStruct((b, seq, D_MODEL), F32),
        scratch_shapes=[pltpu.VMEM((TM_FFN + 2 * HALO, D_MODEL), BF16), pltpu.VMEM((TM_FFN, D_MODEL), F32)],
        compiler_params=_params("arbitrary", "arbitrary", "arbitrary"),
        name="ffn",
    )(h2, h2, h2, xn, w_up16, w_up16, conv_w, conv_w, cb, cb, w_down16, mod3)


def _layer(x, c, ctx, c_ctx, w_mod, b_mod, g_mix, w_in, g_q, g_k, sink, ssm_conv_w, ssm_conv_b,
           a_log, dt_bias, d_skip, g_ssm, w_out, g_ffn, w_up, ffn_conv_w, ffn_conv_b, w_down):
    b = x.shape[0]
    mod_rows = 8 * ((b + 1 + 7) // 8)
    cc = jnp.concatenate([c, c_ctx[None, :], jnp.zeros((mod_rows - b - 1, D_MODEL), F32)], axis=0)
    mod3 = _modulation(cc, w_mod, b_mod).reshape(mod_rows, 1, 6 * D_MODEL)

    w_in_p = jnp.pad(w_in, ((0, 0), (0, IN_COLS_PAD - IN_COLS))).astype(BF16)
    q, kk, vv, z, xbc, dt = _inproj(x, ctx, mod3, g_mix, w_in_p, g_q, g_k)
    attn = _attention(q, kk, vv, sink)

    head_rows = lambda t: jnp.broadcast_to(t.reshape(2 * SSM_HEADS, 1), (2 * SSM_HEADS, CHUNK))
    y = _ssd(xbc, dt, z, ssm_conv_w, ssm_conv_b.reshape(1, -1), head_rows(dt_bias), head_rows(a_log),
             jnp.repeat(d_skip, SSM_HEAD_DIM).reshape(1, -1), g_ssm.reshape(1, -1))

    xn, h2 = _outproj(attn, y, x, w_out.astype(BF16), mod3, g_ffn)
    return _ffn(h2, xn, w_up.astype(BF16), ffn_conv_w, ffn_conv_b, w_down.astype(BF16), mod3)


def kernel(x, c, ctx, c_ctx, w_mod, b_mod, g_mix, w_in, g_q, g_k, sink, ssm_conv_w, ssm_conv_b,
           a_log, dt_bias, d_skip, g_ssm, w_out, g_ffn, w_up, ffn_conv_w, ffn_conv_b, w_down):
    assert w_mod.shape[0] == 1, "single-layer block"
    first = lambda t: t[0]
    return _layer(x, c, ctx, c_ctx, first(w_mod), first(b_mod), first(g_mix), first(w_in), first(g_q),
                  first(g_k), first(sink), first(ssm_conv_w), first(ssm_conv_b), first(a_log),
                  first(dt_bias), first(d_skip), first(g_ssm), first(w_out), first(g_ffn), first(w_up),
                  first(ffn_conv_w), first(ffn_conv_b), first(w_down))
```

```python
import functools

import jax
import jax.numpy as jnp
from jax import lax
from jax.experimental import pallas as pl
from jax.experimental.pallas import tpu as pltpu

F32 = jnp.float32
BF16 = jnp.bfloat16

D_MODEL = 1024
CTX = 256
GRID_W = 64
HEAD_DIM = 64
ATTN_HEADS = 8
KV_HEADS = 2
ATTN_WIDTH = ATTN_HEADS * HEAD_DIM
KV_WIDTH = KV_HEADS * HEAD_DIM
WINDOW = 128
BLOCK = 128
ROPE_BASE = 10000.0
SSM_WIDTH = D_MODEL - ATTN_WIDTH
SSM_HEAD_DIM = 64
SSM_HEADS = SSM_WIDTH // SSM_HEAD_DIM
SSM_GROUPS = 2
D_STATE = 128
XBC_WIDTH = SSM_WIDTH + 2 * SSM_GROUPS * D_STATE
CHUNK = 128
D_FF = 2816
IN_COLS = ATTN_WIDTH + 2 * KV_WIDTH + SSM_WIDTH + XBC_WIDTH + 2 * SSM_HEADS
EPS = 1e-6
NEG_INF = -1e30

LANES = 128
BF16_SUBLANES = 16
IN_COLS_PAD = 19 * LANES
COL_Q, COL_K, COL_V = 0, ATTN_WIDTH, ATTN_WIDTH + KV_WIDTH
COL_Z = ATTN_WIDTH + 2 * KV_WIDTH
COL_XBC = COL_Z + SSM_WIDTH
COL_DT = COL_XBC + XBC_WIDTH
VMEM_LIMIT = 56 * 1024 * 1024

TM_IN = 256
TM_OUT = 512
TM_FFN = 512
TF_FFN = 1408
HALO = BF16_SUBLANES


def _params(*sem):
    return pltpu.CompilerParams(dimension_semantics=sem, vmem_limit_bytes=VMEM_LIMIT)


def _sigmoid(v):
    return 1.0 / (1.0 + jnp.exp(-v))


def _softplus(v):
    return jnp.maximum(v, 0.0) + jnp.log1p(jnp.exp(-jnp.abs(v)))


def _dot(a, b):
    return jnp.dot(a, b, preferred_element_type=F32)


def _dot_nt(a, b):
    return lax.dot_general(a, b, (((1,), (1,)), ((), ())), preferred_element_type=F32)


def _mod_kernel(c_ref, w_ref, b_ref, o_ref):
    cv = c_ref[...]
    s = (cv * _sigmoid(cv)).astype(BF16)
    o_ref[...] = _dot(s, w_ref[...].astype(BF16)) + b_ref[...]


def _modulation(cc, w_mod, b_mod):
    rows = cc.shape[0]
    tn = 1024
    return pl.pallas_call(
        _mod_kernel,
        grid=(6 * D_MODEL // tn,),
        in_specs=[pl.BlockSpec((rows, D_MODEL), lambda j: (0, 0)),
                  pl.BlockSpec((D_MODEL, tn), lambda j: (0, j)),
                  pl.BlockSpec((1, tn), lambda j: (0, j))],
        out_specs=pl.BlockSpec((rows, tn), lambda j: (0, j)),
        out_shape=jax.ShapeDtypeStruct((rows, 6 * D_MODEL), F32),
        compiler_params=_params("arbitrary"),
        name="mod",
    )(cc, w_mod, b_mod.reshape(1, -1))


def _inproj_kernel(x_ref, ctx_ref, mod_ref, gmix_ref, w_ref, gq_ref, gk_ref,
                   cos_ref, sina_ref, sinb_ref, bd_ref,
                   q_ref, k_ref, v_ref, z_ref, xbc_ref, dt_ref):
    s = pl.program_id(1)
    xin = jnp.where(s == 0, ctx_ref[0], x_ref[0])
    ms = jnp.mean(xin * xin, axis=-1, keepdims=True)
    hn = xin * lax.rsqrt(ms + EPS) * gmix_ref[...]
    shift = mod_ref[0, :, 0:D_MODEL]
    scale = mod_ref[0, :, D_MODEL:2 * D_MODEL]
    h = (hn * (1.0 + scale) + shift).astype(BF16)

    cos = cos_ref[...]
    sina = sina_ref[...]
    sinb = sinb_ref[...]
    bd = bd_ref[...]

    def norm_rope(pf, gain):
        ss = pf * pf
        hi = ss.astype(BF16)
        lo = (ss - hi.astype(F32)).astype(BF16)
        sums = _dot(jnp.concatenate([hi, lo], axis=1), bd)
        y = pf * lax.rsqrt(sums * (1.0 / HEAD_DIM) + EPS) * gain
        return y * cos + pltpu.roll(y, LANES - 16, 1) * sina + pltpu.roll(y, 16, 1) * sinb

    qf = _dot(h, w_ref[:, COL_Q:COL_Q + ATTN_WIDTH])
    for cidx in range(ATTN_WIDTH // LANES):
        cs = slice(cidx * LANES, (cidx + 1) * LANES)
        q_ref[0, :, cs] = norm_rope(qf[:, cs], gq_ref[...]).astype(BF16)

    kvf = _dot(h, w_ref[:, COL_K:COL_K + 2 * KV_WIDTH])
    kf = norm_rope(kvf[:, 0:KV_WIDTH], gk_ref[...])
    k_ref[0, :, 0:LANES] = kf.astype(BF16)
    k_ref[0, :, LANES:2 * LANES] = pltpu.roll(kf, HEAD_DIM, 1).astype(BF16)
    vf = kvf[:, KV_WIDTH:2 * KV_WIDTH]
    v_ref[0, :, 0:LANES] = vf.astype(BF16)
    v_ref[0, :, LANES:2 * LANES] = pltpu.roll(vf, HEAD_DIM, 1).astype(BF16)
    z_ref[0] = _dot(h, w_ref[:, COL_Z:COL_Z + SSM_WIDTH]).astype(BF16)
    xbc_ref[0] = _dot(h, w_ref[:, COL_XBC:COL_XBC + XBC_WIDTH]).astype(BF16)
    dt_ref[0] = _dot(h, w_ref[:, COL_DT:COL_DT + LANES])


def _rope_tables(seq):
    rows = seq // GRID_W
    pos_r = jnp.repeat(jnp.arange(rows), GRID_W).astype(F32)
    pos_c = jnp.tile(jnp.arange(GRID_W), rows).astype(F32)
    quarter = HEAD_DIM // 4
    freqs = ROPE_BASE ** (-jnp.arange(quarter, dtype=F32) / quarter)
    ang_r = pos_r[:, None] * freqs[None, :]
    ang_c = pos_c[:, None] * freqs[None, :]
    cr, sr, cc, sc = jnp.cos(ang_r), jnp.sin(ang_r), jnp.cos(ang_c), jnp.sin(ang_c)
    zero = jnp.zeros_like(sr)
    cos = jnp.concatenate([cr, cr, cc, cc], axis=-1)
    sina = jnp.concatenate([-sr, zero, -sc, zero], axis=-1)
    sinb = jnp.concatenate([zero, sr, zero, sc], axis=-1)
    reps = LANES // HEAD_DIM
    pad = lambda t, v: jnp.concatenate([jnp.full((CTX, LANES), v, F32), jnp.tile(t, (1, reps))], axis=0)
    return pad(cos, 1.0), pad(sina, 0.0), pad(sinb, 0.0)


def _inproj(x, ctx, mod3, g_mix, w_in_p, g_q, g_k):
    b, seq = x.shape[0], x.shape[1]
    tot = CTX + seq
    cos, sina, sinb = _rope_tables(seq)
    reps = LANES // HEAD_DIM
    gq = (jnp.tile(g_q, reps) * (HEAD_DIM ** -0.5)).reshape(1, LANES)
    gk = jnp.tile(g_k, reps).reshape(1, LANES)
    lane = jnp.arange(LANES)
    bd = (lane[:, None] // HEAD_DIM == lane[None, :] // HEAD_DIM).astype(BF16)
    bd = jnp.concatenate([bd, bd], axis=0)
    nsteps = tot // TM_IN
    const = lambda *shape: pl.BlockSpec(shape, lambda bb, s: (0,) * len(shape))
    tab = pl.BlockSpec((TM_IN, LANES), lambda bb, s: (s, 0))
    out = lambda w: pl.BlockSpec((1, TM_IN, w), lambda bb, s: (bb, s, 0))
    shp = lambda w, dt: jax.ShapeDtypeStruct((b, tot, w), dt)
    return pl.pallas_call(
        _inproj_kernel,
        grid=(b, nsteps),
        in_specs=[pl.BlockSpec((1, TM_IN, D_MODEL), lambda bb, s: (bb, jnp.maximum(s - 1, 0), 0)),
                  pl.BlockSpec((1, CTX, D_MODEL), lambda bb, s: (bb, 0, 0)),
                  pl.BlockSpec((1, 1, 6 * D_MODEL), lambda bb, s: (jnp.where(s == 0, b, bb), 0, 0)),
                  const(1, D_MODEL), const(D_MODEL, IN_COLS_PAD), const(1, LANES), const(1, LANES),
                  tab, tab, tab, const(2 * LANES, LANES)],
        out_specs=[out(ATTN_WIDTH), out(2 * LANES), out(2 * LANES), out(SSM_WIDTH), out(XBC_WIDTH), out(LANES)],
        out_shape=[shp(ATTN_WIDTH, BF16), shp(2 * LANES, BF16), shp(2 * LANES, BF16),
                   shp(SSM_WIDTH, BF16), shp(XBC_WIDTH, BF16), shp(LANES, F32)],
        compiler_params=_params("arbitrary", "arbitrary"),
        name="inproj",
    )(x, ctx, mod3, g_mix.reshape(1, -1), w_in_p, gq, gk, cos, sina, sinb, bd)


def _place_kv_head(t, kh, par):
    src = 0 if par == kh else 1
    lane = lax.broadcasted_iota(jnp.int32, (t.shape[0], LANES), 1)
    half = (lane >= HEAD_DIM) if par else (lane < HEAD_DIM)
    tile = t[:, src * LANES:(src + 1) * LANES]
    return jnp.where(half, tile, jnp.zeros_like(tile))


def _attn_kernel(sink_ref, q_ref, k_ref, v_ref, o_ref, kx_scr, vx_scr):
    seq = o_ref.shape[1]
    nblk = seq // BLOCK
    qi = lax.broadcasted_iota(jnp.int32, (BLOCK, BLOCK), 0)
    kj = lax.broadcasted_iota(jnp.int32, (BLOCK, BLOCK), 1)
    zero = jnp.zeros((BLOCK, BLOCK), F32)
    tri_prev = jnp.where(kj >= qi, zero, NEG_INF)
    tri_next = jnp.where(kj <= qi, zero, NEG_INF)
    combos = [(kh, par) for kh in range(KV_HEADS) for par in range(2)]
    for idx, (kh, par) in enumerate(combos):
        kx_scr[idx] = _place_kv_head(k_ref[0, 0:CTX, :], kh, par)
        vx_scr[idx] = _place_kv_head(v_ref[0, 0:CTX, :], kh, par)

    def block(nb, carry):
        r0 = pl.multiple_of(CTX + nb * BLOCK, BLOCK)
        rn = pl.multiple_of(CTX + jnp.minimum(nb + 1, nblk - 1) * BLOCK, BLOCK)
        rp = pl.multiple_of(CTX - BLOCK + nb * BLOCK, BLOCK)
        kloc = jnp.concatenate([k_ref[0, pl.ds(rp, 2 * BLOCK), :], k_ref[0, pl.ds(rn, BLOCK), :]], axis=0)
        vloc = jnp.concatenate([v_ref[0, pl.ds(rp, 2 * BLOCK), :], v_ref[0, pl.ds(rn, BLOCK), :]], axis=0)
        bias_p = jnp.where(nb > 0, tri_prev, NEG_INF)
        bias_n = jnp.where(nb < nblk - 1, tri_next, NEG_INF)
        q = q_ref[0, pl.ds(r0, BLOCK), :]
        for kh in range(KV_HEADS):
            q2 = jnp.concatenate([q[:, (2 * kh) * LANES:(2 * kh + 1) * LANES],
                                  q[:, (2 * kh + 1) * LANES:(2 * kh + 2) * LANES]], axis=0)
            acc = [None, None]
            for par in range(2):
                idx = combos.index((kh, par))
                s_loc = _dot_nt(q2, _place_kv_head(kloc, kh, par))
                s_ctx = _dot_nt(q2, kx_scr[idx])
                vl = _place_kv_head(vloc, kh, par)
                for r in range(2):
                    head = 4 * kh + par + 2 * r
                    rs = slice(r * BLOCK, (r + 1) * BLOCK)
                    sv = jnp.concatenate([s_loc[rs, 0:BLOCK] + bias_p, s_loc[rs, BLOCK:2 * BLOCK],
                                          s_loc[rs, 2 * BLOCK:3 * BLOCK] + bias_n, s_ctx[rs]], axis=1)
                    sk = sink_ref[head]
                    m = jnp.maximum(jnp.max(sv, axis=-1, keepdims=True), sk)
                    e = jnp.exp(sv - m)
                    den = jnp.sum(e, axis=-1, keepdims=True) + jnp.exp(sk - m)
                    e16 = e.astype(BF16)
                    o = (_dot(e16[:, 0:3 * BLOCK], vl) + _dot(e16[:, 3 * BLOCK:], vx_scr[idx])) * (1.0 / den)
                    acc[r] = o if acc[r] is None else acc[r] + o
            for r in range(2):
                c = 2 * kh + r
                o_ref[0, pl.ds(pl.multiple_of(nb * BLOCK, BLOCK), BLOCK), c * LANES:(c + 1) * LANES] = acc[r].astype(BF16)
        return carry

    lax.fori_loop(0, nblk, block, 0)


def _attention(q, kk, vv, sink):
    b, tot = q.shape[0], q.shape[1]
    seq = tot - CTX
    whole = lambda w: pl.BlockSpec((1, tot, w), lambda bb: (bb, 0, 0))
    return pl.pallas_call(
        _attn_kernel,
        grid=(b,),
        in_specs=[pl.BlockSpec(memory_space=pltpu.SMEM), whole(ATTN_WIDTH), whole(2 * LANES), whole(2 * LANES)],
        out_specs=pl.BlockSpec((1, seq, ATTN_WIDTH), lambda bb: (bb, 0, 0)),
        out_shape=jax.ShapeDtypeStruct((b, seq, ATTN_WIDTH), BF16),
        scratch_shapes=[pltpu.VMEM((2 * KV_HEADS, CTX, LANES), BF16), pltpu.VMEM((2 * KV_HEADS, CTX, LANES), BF16)],
        compiler_params=_params("arbitrary"),
        name="attn",
    )(sink, q, kk, vv)


def _ssd_kernel(x_ref, dt_ref, z_ref, cw_ref, cb_ref, dtb_ref, alog_ref, dsk_ref, g_ref, tri_ref, shift_ref,
                y_ref, hf_scr, hb_scr, yacc_scr, stb_scr, decb_scr, eb_scr, cbuf_scr):
    tot = x_ref.shape[1]
    nchunks = tot // CHUNK
    nctx = CTX // CHUNK
    npairs = SSM_HEADS // 2
    pairs_per_group = npairs // SSM_GROUPS
    group_cols = pairs_per_group * LANES

    hf_scr[...] = jnp.zeros_like(hf_scr)
    hb_scr[...] = jnp.zeros_like(hb_scr)

    def chunk_pass(chunk, carry):
        is_lat = chunk >= nctx
        lc = jnp.maximum(chunk - nctx, 0)
        c0 = pl.multiple_of(chunk * CHUNK, CHUNK)
        p0 = pl.multiple_of(jnp.maximum(chunk * CHUNK - HALO, 0), HALO)
        n0 = pl.multiple_of(jnp.minimum(chunk * CHUNK + CHUNK, tot - HALO), HALO)

        x16 = x_ref[0, pl.ds(c0, CHUNK), :]
        has_prev = (chunk != 0) & (chunk != nctx)
        has_next = (chunk != nctx - 1) & (chunk != nchunks - 1)
        halo_p = x_ref[0, pl.ds(p0, HALO), :]
        halo_n = x_ref[0, pl.ds(n0, HALO), :]
        xext = jnp.concatenate([jnp.where(has_prev, halo_p, jnp.zeros_like(halo_p)), x16,
                                jnp.where(has_next, halo_n, jnp.zeros_like(halo_n))], axis=0)
        shifted = _dot(shift_ref[...], xext)
        u = (cw_ref[0:1, :] * shifted[0:CHUNK] + cw_ref[1:2, :] * x16.astype(F32)
             + cw_ref[2:3, :] * shifted[CHUNK:2 * CHUNK] + cb_ref[...])
        u = u * _sigmoid(u)
        xs = u[:, 0:SSM_WIDTH]
        bmat = u[:, SSM_WIDTH:SSM_WIDTH + SSM_GROUPS * D_STATE]
        cmat = u[:, SSM_WIDTH + SSM_GROUPS * D_STATE:XBC_WIDTH]
        cb16 = [cmat[:, g * D_STATE:(g + 1) * D_STATE].astype(BF16) for g in range(SSM_GROUPS)]
        bb16 = [bmat[:, g * D_STATE:(g + 1) * D_STATE].astype(BF16) for g in range(SSM_GROUPS)]
        bt16 = [bmat[:, g * D_STATE:(g + 1) * D_STATE].T.astype(BF16) for g in range(SSM_GROUPS)]

        nh2 = 2 * SSM_HEADS
        dtv = _softplus(dt_ref[0, pl.ds(c0, CHUNK), :].T[0:nh2, :] + dtb_ref[...])
        a = dtv * (-jnp.exp(alog_ref[...]))
        a_hi = a.astype(BF16)
        r1 = a - a_hi.astype(F32)
        a_mid = r1.astype(BF16)
        a_lo = (r1 - a_mid.astype(F32)).astype(BF16)
        parts = jnp.concatenate([a_hi, a_mid, a_lo], axis=0)
        fold = lambda p: p[0:nh2] + p[nh2:2 * nh2] + p[2 * nh2:3 * nh2]
        prefix = fold(_dot(parts, tri_ref[0]))
        suffix = fold(_dot(parts, tri_ref[1]))
        head_row = lax.broadcasted_iota(jnp.int32, (nh2, CHUNK), 0)
        ac_t = jnp.where(head_row < SSM_HEADS, prefix, suffix)
        cols = jnp.concatenate([ac_t, dtv, jnp.zeros((CHUNK - 2 * nh2, CHUNK), F32)], axis=0).T

        half0 = lax.broadcasted_iota(jnp.int32, (CHUNK, LANES), 1) < SSM_HEAD_DIM
        ii = lax.broadcasted_iota(jnp.int32, (CHUNK, CHUNK), 0)
        jj = lax.broadcasted_iota(jnp.int32, (CHUNK, CHUNK), 1)
        causal = [jj <= ii, jj >= ii]
        last = [CHUNK - 1, 0]

        bcast = [jnp.broadcast_to(cols[:, lane:lane + 1], (CHUNK, LANES)) for lane in range(2 * nh2)]
        arg_pair, tot_pair, xdt_pair, xdtd_pair = [], [], [], []
        for d in range(2):
            args, tots, xdts, xdtds = [], [], [], []
            for m in range(npairs):
                r = d * SSM_HEADS + 2 * m
                arg = jnp.where(half0, bcast[r], bcast[r + 1])
                dtp = jnp.where(half0, bcast[nh2 + r], bcast[nh2 + r + 1])
                total = arg[last[d]:last[d] + 1, :]
                xdt = xs[:, m * LANES:(m + 1) * LANES] * dtp
                args.append(arg)
                tots.append(total)
                xdts.append(xdt)
                xdtds.append(xdt * jnp.exp(total - arg))
            arg_pair.append(args)
            tot_pair.append(tots)
            xdt_pair.append(xdts)
            xdtd_pair.append(xdtds)

        hf_start = [hf_scr[:, g * group_cols:(g + 1) * group_cols] for g in range(SSM_GROUPS)]

        for g in range(SSM_GROUPS):
            pr = range(g * pairs_per_group, (g + 1) * pairs_per_group)
            gs = slice(g * group_cols, (g + 1) * group_cols)
            st_f = _dot(bt16[g], jnp.concatenate([xdtd_pair[0][m] for m in pr], axis=1).astype(BF16))
            st_b = _dot(bt16[g], jnp.concatenate([xdtd_pair[1][m] for m in pr], axis=1).astype(BF16))
            dec_f = jnp.concatenate([jnp.exp(tot_pair[0][m]) for m in pr], axis=1)
            dec_b = jnp.concatenate([jnp.exp(tot_pair[1][m]) for m in pr], axis=1)
            hf_scr[:, gs] = hf_start[g] * dec_f + st_f
            stb_scr[chunk, :, gs] = st_b
            decb_scr[chunk, :, gs] = jnp.broadcast_to(dec_b, (8, group_cols))

        @pl.when(is_lat)
        def _():
            r0 = pl.multiple_of(lc * CHUNK, CHUNK)
            cbuf_scr[lc] = jnp.concatenate(cb16, axis=1)
            for g in range(SSM_GROUPS):
                gmat = _dot_nt(cb16[g], bb16[g])
                yoff = _dot(cb16[g], hf_start[g].astype(BF16))
                for mm in range(pairs_per_group):
                    m = g * pairs_per_group + mm
                    ms = slice(m * LANES, (m + 1) * LANES)
                    smats, rhs = [], []
                    for d in range(2):
                        for hh in range(2):
                            r = d * SSM_HEADS + 2 * m + hh
                            seg = bcast[r] - ac_t[r:r + 1, :]
                            lmat = jnp.where(causal[d], jnp.exp(jnp.where(causal[d], seg, 0.0)), 0.0)
                            smats.append((gmat * lmat).astype(BF16))
                        xd = xdt_pair[d][m]
                        rhs += [jnp.where(half0, xd, 0.0), jnp.where(half0, 0.0, xd)]
                    ydiag = _dot(jnp.concatenate(smats, axis=1), jnp.concatenate(rhs, axis=0).astype(BF16))
                    yacc_scr[pl.ds(r0, CHUNK), ms] = (ydiag + yoff[:, mm * LANES:(mm + 1) * LANES]
                                                      * jnp.exp(arg_pair[0][m]) + dsk_ref[:, ms] * xs[:, ms])
                    eb_scr[lc, :, ms] = jnp.exp(arg_pair[1][m])
        return carry

    lax.fori_loop(0, nchunks, chunk_pass, 0)

    def carry_pass(i, carry):
        chunk = jnp.where(i < nctx, nctx - 1 - i, nchunks + nctx - 1 - i)
        hb = hb_scr[...]

        @pl.when(chunk >= nctx)
        def _():
            lc = chunk - nctx
            r0 = pl.multiple_of(lc * CHUNK, CHUNK)
            cb = cbuf_scr[lc]
            yoff = jnp.concatenate(
                [_dot(cb[:, g * D_STATE:(g + 1) * D_STATE], hb[:, g * group_cols:(g + 1) * group_cols].astype(BF16))
                 for g in range(SSM_GROUPS)], axis=1)
            zf = z_ref[0, pl.ds(pl.multiple_of(chunk * CHUNK, CHUNK), CHUNK), :].astype(F32)
            yt = (yacc_scr[pl.ds(r0, CHUNK), :] + yoff * eb_scr[lc]) * (zf * _sigmoid(zf))
            gw = SSM_WIDTH // SSM_GROUPS
            for g in range(SSM_GROUPS):
                yg = yt[:, g * gw:(g + 1) * gw]
                ms = jnp.mean(yg * yg, axis=-1, keepdims=True)
                y_ref[0, pl.ds(r0, CHUNK), g * gw:(g + 1) * gw] = (yg * lax.rsqrt(ms + EPS)
                                                                   * g_ref[:, g * gw:(g + 1) * gw]).astype(BF16)

        hb_scr[...] = hb * decb_scr[chunk, 0:1, :] + stb_scr[chunk]
        return carry

    lax.fori_loop(0, nchunks, carry_pass, 0)


def _ssd(xbc, dt, z, conv_w, conv_b, dtb16, alog16, dsk, g_ssm):
    b, tot = xbc.shape[0], xbc.shape[1]
    seq = tot - CTX
    nchunks = tot // CHUNK
    nctx = CTX // CHUNK
    nlat = nchunks - nctx
    const = lambda *shape: pl.BlockSpec(shape, lambda bb: (0,) * len(shape))
    whole = lambda rows, w: pl.BlockSpec((1, rows, w), lambda bb: (bb, 0, 0))
    jdx = jnp.arange(CHUNK)
    tri = jnp.stack([jdx[:, None] <= jdx[None, :], jdx[:, None] >= jdx[None, :]]).astype(BF16)
    kdx = jnp.arange(CHUNK + 2 * HALO)
    shift = jnp.concatenate([kdx[None, :] == jdx[:, None] + HALO - 1,
                             kdx[None, :] == jdx[:, None] + HALO + 1], axis=0).astype(BF16)
    return pl.pallas_call(
        _ssd_kernel,
        grid=(b,),
        in_specs=[whole(tot, XBC_WIDTH), whole(tot, LANES), whole(tot, SSM_WIDTH),
                  const(3, XBC_WIDTH), const(1, XBC_WIDTH), const(2 * SSM_HEADS, CHUNK), const(2 * SSM_HEADS, CHUNK),
                  const(1, SSM_WIDTH), const(1, SSM_WIDTH), const(2, CHUNK, CHUNK),
                  const(2 * CHUNK, CHUNK + 2 * HALO)],
        out_specs=whole(seq, SSM_WIDTH),
        out_shape=jax.ShapeDtypeStruct((b, seq, SSM_WIDTH), BF16),
        scratch_shapes=[pltpu.VMEM((D_STATE, SSM_WIDTH), F32),
                        pltpu.VMEM((D_STATE, SSM_WIDTH), F32),
                        pltpu.VMEM((seq, SSM_WIDTH), F32),
                        pltpu.VMEM((nchunks, D_STATE, SSM_WIDTH), F32),
                        pltpu.VMEM((nchunks, 8, SSM_WIDTH), F32),
                        pltpu.VMEM((nlat, CHUNK, SSM_WIDTH), F32),
                        pltpu.VMEM((nlat, CHUNK, SSM_GROUPS * D_STATE), BF16)],
        compiler_params=_params("arbitrary"),
        name="ssd",
    )(xbc, dt, z, conv_w, conv_b, dtb16, alog16, dsk, g_ssm, tri, shift)


def _outproj_kernel(a_ref, y_ref, x_ref, w_ref, mod_ref, g_ref, xn_ref, h2_ref):
    o = _dot(a_ref[0], w_ref[0:ATTN_WIDTH, :]) + _dot(y_ref[0], w_ref[ATTN_WIDTH:D_MODEL, :])
    gate = mod_ref[0, :, 2 * D_MODEL:3 * D_MODEL]
    xn = x_ref[0] + gate * o
    xn_ref[0] = xn
    ms = jnp.mean(xn * xn, axis=-1, keepdims=True)
    hn = xn * lax.rsqrt(ms + EPS) * g_ref[...]
    shift = mod_ref[0, :, 3 * D_MODEL:4 * D_MODEL]
    scale = mod_ref[0, :, 4 * D_MODEL:5 * D_MODEL]
    h2_ref[0] = (hn * (1.0 + scale) + shift).astype(BF16)


def _outproj(attn, y, x, w_out16, mod3, g_ffn):
    b, seq = x.shape[0], x.shape[1]
    row = lambda w: pl.BlockSpec((1, TM_OUT, w), lambda bb, i: (bb, i, 0))
    return pl.pallas_call(
        _outproj_kernel,
        grid=(b, seq // TM_OUT),
        in_specs=[row(ATTN_WIDTH), row(SSM_WIDTH), row(D_MODEL),
                  pl.BlockSpec((D_MODEL, D_MODEL), lambda bb, i: (0, 0)),
                  pl.BlockSpec((1, 1, 6 * D_MODEL), lambda bb, i: (bb, 0, 0)),
                  pl.BlockSpec((1, D_MODEL), lambda bb, i: (0, 0))],
        out_specs=[row(D_MODEL), row(D_MODEL)],
        out_shape=[jax.ShapeDtypeStruct((b, seq, D_MODEL), F32), jax.ShapeDtypeStruct((b, seq, D_MODEL), BF16)],
        compiler_params=_params("arbitrary", "arbitrary"),
        name="outproj",
    )(attn, y, x, w_out16, mod3, g_ffn.reshape(1, -1))


def _ffn_kernel(hm_ref, hp_ref, hn_ref, xn_ref, wa_ref, wg_ref, cwa_ref, cwg_ref, cba_ref, cbg_ref, wd_ref,
                mod_ref, o_ref, lhs_scr, acc_scr):
    i = pl.program_id(1)
    j = pl.program_id(2)
    tm = hm_ref.shape[1]

    @pl.when(j == 0)
    def _():
        lhs_scr[0:HALO, :] = jnp.where(i > 0, hp_ref[0], jnp.zeros_like(hp_ref[0]))
        lhs_scr[HALO:HALO + tm, :] = hm_ref[0]
        lhs_scr[HALO + tm:, :] = jnp.where(i < pl.num_programs(1) - 1, hn_ref[0], jnp.zeros_like(hn_ref[0]))
        acc_scr[...] = jnp.zeros_like(acc_scr)

    lhs = lhs_scr[...]
    nrows = lhs.shape[0]

    def conv(w_ref, cw_ref, cb_ref):
        uu = _dot(lhs, w_ref[...])
        up = pltpu.roll(uu, 1, 0)[HALO:HALO + tm]
        dn = pltpu.roll(uu, nrows - 1, 0)[HALO:HALO + tm]
        return cw_ref[0:1, :] * up + cw_ref[1:2, :] * uu[HALO:HALO + tm] + cw_ref[2:3, :] * dn + cb_ref[...]

    ua = conv(wa_ref, cwa_ref, cba_ref)
    ug = conv(wg_ref, cwg_ref, cbg_ref)
    act = (ua * (ug * _sigmoid(ug))).astype(BF16)
    acc_scr[...] += _dot(act, wd_ref[...])

    @pl.when(j == pl.num_programs(2) - 1)
    def _():
        o_ref[0] = xn_ref[0] + mod_ref[0, :, 5 * D_MODEL:6 * D_MODEL] * acc_scr[...]


def _ffn(h2, xn, w_up16, conv_w, conv_b, w_down16, mod3):
    b, seq = xn.shape[0], xn.shape[1]
    nj = D_FF // TF_FFN
    ni = seq // TM_FFN
    hpt = TM_FFN // HALO
    nhalo = seq // HALO
    cb = conv_b.reshape(1, -1)
    return pl.pallas_call(
        _ffn_kernel,
        grid=(b, ni, nj),
        in_specs=[pl.BlockSpec((1, TM_FFN, D_MODEL), lambda bb, i, j: (bb, i, 0)),
                  pl.BlockSpec((1, HALO, D_MODEL), lambda bb, i, j: (bb, jnp.maximum(i * hpt - 1, 0), 0)),
                  pl.BlockSpec((1, HALO, D_MODEL), lambda bb, i, j: (bb, jnp.minimum(i * hpt + hpt, nhalo - 1), 0)),
                  pl.BlockSpec((1, TM_FFN, D_MODEL), lambda bb, i, j: (bb, i, 0)),
                  pl.BlockSpec((D_MODEL, TF_FFN), lambda bb, i, j: (0, j)),
                  pl.BlockSpec((D_MODEL, TF_FFN), lambda bb, i, j: (0, nj + j)),
                  pl.BlockSpec((3, TF_FFN), lambda bb, i, j: (0, j)),
                  pl.BlockSpec((3, TF_FFN), lambda bb, i, j: (0, nj + j)),
                  pl.BlockSpec((1, TF_FFN), lambda bb, i, j: (0, j)),
                  pl.BlockSpec((1, TF_FFN), lambda bb, i, j: (0, nj + j)),
                  pl.BlockSpec((TF_FFN, D_MODEL), lambda bb, i, j: (j, 0)),
                  pl.BlockSpec((1, 1, 6 * D_MODEL), lambda bb, i, j: (bb, 0, 0))],
        out_specs=pl.BlockSpec((1, TM_FFN, D_MODEL), lambda bb, i, j: (bb, i, 0)),
        out_shape=jax.ShapeDtypeStruct((b, seq, D_MODEL), F32),
        scratch_shapes=[pltpu.VMEM((TM_FFN + 2 * HALO, D_MODEL), BF16), pltpu.VMEM((TM_FFN, D_MODEL), F32)],
        compiler_params=_params("arbitrary", "arbitrary", "arbitrary"),
        name="ffn",
    )(h2, h2, h2, xn, w_up16, w_up16, conv_w, conv_w, cb, cb, w_down16, mod3)


def _layer(x, c, ctx, c_ctx, w_mod, b_mod, g_mix, w_in, g_q, g_k, sink, ssm_conv_w, ssm_conv_b,
           a_log, dt_bias, d_skip, g_ssm, w_out, g_ffn, w_up, ffn_conv_w, ffn_conv_b, w_down):
    b = x.shape[0]
    mod_rows = 8 * ((b + 1 + 7) // 8)
    cc = jnp.concatenate([c, c_ctx[None, :], jnp.zeros((mod_rows - b - 1, D_MODEL), F32)], axis=0)
    mod3 = _modulation(cc, w_mod, b_mod).reshape(mod_rows, 1, 6 * D_MODEL)

    w_in_p = jnp.pad(w_in, ((0, 0), (0, IN_COLS_PAD - IN_COLS))).astype(BF16)
    q, kk, vv, z, xbc, dt = _inproj(x, ctx, mod3, g_mix, w_in_p, g_q, g_k)
    attn = _attention(q, kk, vv, sink)

    head_rows = lambda t: jnp.broadcast_to(t.reshape(2 * SSM_HEADS, 1), (2 * SSM_HEADS, CHUNK))
    y = _ssd(xbc, dt, z, ssm_conv_w, ssm_conv_b.reshape(1, -1), head_rows(dt_bias), head_rows(a_log),
             jnp.repeat(d_skip, SSM_HEAD_DIM).reshape(1, -1), g_ssm.reshape(1, -1))

    xn, h2 = _outproj(attn, y, x, w_out.astype(BF16), mod3, g_ffn)
    return _ffn(h2, xn, w_up.astype(BF16), ffn_conv_w, ffn_conv_b, w_down.astype(BF16), mod3)


def kernel(x, c, ctx, c_ctx, w_mod, b_mod, g_mix, w_in, g_q, g_k, sink, ssm_conv_w, ssm_conv_b,
           a_log, dt_bias, d_skip, g_ssm, w_out, g_ffn, w_up, ffn_conv_w, ffn_conv_b, w_down):
    assert w_mod.shape[0] == 1, "single-layer block"
    first = lambda t: t[0]
    return _layer(x, c, ctx, c_ctx, first(w_mod), first(b_mod), first(g_mix), first(w_in), first(g_q),
                  first(g_k), first(sink), first(ssm_conv_w), first(ssm_conv_b), first(a_log),
                  first(dt_bias), first(d_skip), first(g_ssm), first(w_out), first(g_ffn), first(w_up),
                  first(ffn_conv_w), first(ffn_conv_b), first(w_down))
```

```python
import jax
import jax.numpy as jnp
import numpy as np
from jax import lax
from jax.experimental import pallas as pl
from jax.experimental.pallas import tpu as pltpu

F32 = jnp.float32
BF16 = jnp.bfloat16

D_MODEL = 1024
CTX = 256
GRID_W = 64
HEAD_DIM = 64
ATTN_HEADS = 8
KV_HEADS = 2
ATTN_WIDTH = ATTN_HEADS * HEAD_DIM
KV_WIDTH = KV_HEADS * HEAD_DIM
WINDOW = 128
BLOCK = 128
ROPE_BASE = 10000.0
SSM_WIDTH = D_MODEL - ATTN_WIDTH
SSM_HEAD_DIM = 64
SSM_HEADS = SSM_WIDTH // SSM_HEAD_DIM
SSM_GROUPS = 2
D_STATE = 128
XBC_WIDTH = SSM_WIDTH + 2 * SSM_GROUPS * D_STATE
CHUNK = 128
D_FF = 2816
IN_COLS = ATTN_WIDTH + 2 * KV_WIDTH + SSM_WIDTH + XBC_WIDTH + 2 * SSM_HEADS
EPS = 1e-6
NEG_INF = -1e30
LOG2E = 1.4426950408889634

LANES = 128
BF16_SUBLANES = 16
IN_COLS_PAD = 19 * LANES
COL_Q, COL_K, COL_V = 0, ATTN_WIDTH, ATTN_WIDTH + KV_WIDTH
COL_Z = ATTN_WIDTH + 2 * KV_WIDTH
COL_XBC = COL_Z + SSM_WIDTH
COL_DT = COL_XBC + XBC_WIDTH
VMEM_LIMIT = 56 * 1024 * 1024

TM_IN = 3 * CTX
TM_OUT = 512
TM_FFN = 512
TF_FFN = 1408
HALO = BF16_SUBLANES


def _params(*sem, flags=None):
    return pltpu.CompilerParams(dimension_semantics=sem, vmem_limit_bytes=VMEM_LIMIT, flags=flags)


def _sigmoid(v):
    return 1.0 / (1.0 + jnp.exp(-v))


def _softplus(v):
    return jnp.maximum(v, 0.0) + jnp.log1p(jnp.exp(-jnp.abs(v)))


def _dot(a, b):
    return jnp.dot(a, b, preferred_element_type=F32)


def _dot_nt(a, b):
    return lax.dot_general(a, b, (((1,), (1,)), ((), ())), preferred_element_type=F32)


def _mod_kernel(c_ref, cctx_ref, w_ref, b_ref, o_ref):
    rows = o_ref.shape[0]
    b = c_ref.shape[0]
    cv = jnp.concatenate([c_ref[...], cctx_ref[...], jnp.zeros((rows - b - 1, D_MODEL), F32)], axis=0)
    s = (cv * _sigmoid(cv)).astype(BF16)
    res = _dot(s, w_ref[...].astype(BF16)) + b_ref[...]
    for r in range(b + 1):
        o_ref[r] = res[r:r + 1, :]
    o_ref[b + 1:rows] = jnp.zeros((rows - b - 1, 1, res.shape[1]), F32)


def _modulation(c, c_ctx, w_mod, b_mod):
    b = c.shape[0]
    rows = 8 * ((b + 1 + 7) // 8)
    tn = 1024
    return pl.pallas_call(
        _mod_kernel,
        grid=(6 * D_MODEL // tn,),
        in_specs=[pl.BlockSpec((b, D_MODEL), lambda j: (0, 0)),
                  pl.BlockSpec((1, D_MODEL), lambda j: (0, 0)),
                  pl.BlockSpec((D_MODEL, tn), lambda j: (0, j)),
                  pl.BlockSpec((1, tn), lambda j: (0, j))],
        out_specs=pl.BlockSpec((rows, 1, tn), lambda j: (0, 0, j)),
        out_shape=jax.ShapeDtypeStruct((rows, 1, 6 * D_MODEL), F32),
        compiler_params=_params("arbitrary"),
        name="mod",
    )(c, c_ctx.reshape(1, -1), w_mod, b_mod.reshape(1, -1))


def _inproj_kernel(xa_ref, xb_ref, xc_ref, ctx_ref, modb_ref, modc_ref, gmix_ref, w_ref, gq_ref, gk_ref,
                   cos_ref, sina_ref, sinb_ref, bd_ref,
                   q_ref, k_ref, v_ref, z_ref, xbc_ref, dt_ref):
    first = pl.program_id(1) == 0

    def normed(xin, mod):
        ms = jnp.mean(xin * xin, axis=-1, keepdims=True)
        hn = xin * lax.rsqrt(ms + EPS) * gmix_ref[...]
        return (hn * (1.0 + mod[:, D_MODEL:2 * D_MODEL]) + mod[:, 0:D_MODEL]).astype(BF16)

    modb = modb_ref[0]
    h = jnp.concatenate([normed(jnp.where(first, ctx_ref[0], xa_ref[0]), jnp.where(first, modc_ref[0], modb)),
                         normed(xb_ref[0], modb), normed(xc_ref[0], modb)], axis=0)

    cos = cos_ref[...]
    sina = sina_ref[...]
    sinb = sinb_ref[...]
    bd = bd_ref[...]

    def norm_rope(pf, gain):
        ss = pf * pf
        hi = ss.astype(BF16)
        lo = (ss - hi.astype(F32)).astype(BF16)
        sums = _dot(jnp.concatenate([hi, lo], axis=1), bd)
        y = pf * lax.rsqrt(sums * (1.0 / HEAD_DIM) + EPS) * gain
        return y * cos + pltpu.roll(y, LANES - 16, 1) * sina + pltpu.roll(y, 16, 1) * sinb

    qf = _dot(h, w_ref[:, COL_Q:COL_Q + ATTN_WIDTH])
    for cidx in range(ATTN_WIDTH // LANES):
        cs = slice(cidx * LANES, (cidx + 1) * LANES)
        q_ref[0, :, cs] = norm_rope(qf[:, cs], gq_ref[...]).astype(BF16)

    kvf = _dot(h, w_ref[:, COL_K:COL_K + 2 * KV_WIDTH])
    kf = norm_rope(kvf[:, 0:KV_WIDTH], gk_ref[...])
    k_ref[0, :, 0:LANES] = kf.astype(BF16)
    k_ref[0, :, LANES:2 * LANES] = pltpu.roll(kf, HEAD_DIM, 1).astype(BF16)
    vf = kvf[:, KV_WIDTH:2 * KV_WIDTH]
    v_ref[0, :, 0:LANES] = vf.astype(BF16)
    v_ref[0, :, LANES:2 * LANES] = pltpu.roll(vf, HEAD_DIM, 1).astype(BF16)
    z_ref[0] = _dot(h, w_ref[:, COL_Z:COL_Z + SSM_WIDTH]).astype(BF16)
    xbc_ref[0] = _dot(h, w_ref[:, COL_XBC:COL_XBC + XBC_WIDTH]).astype(BF16)
    dt_ref[0] = _dot(h, w_ref[:, COL_DT:COL_DT + LANES])


def _rope_tables(seq):
    rows = seq // GRID_W
    pos_r = np.repeat(np.arange(rows), GRID_W).astype(np.float32)
    pos_c = np.tile(np.arange(GRID_W), rows).astype(np.float32)
    quarter = HEAD_DIM // 4
    freqs = (ROPE_BASE ** (-np.arange(quarter, dtype=np.float32) / quarter)).astype(np.float32)
    ang_r = pos_r[:, None] * freqs[None, :]
    ang_c = pos_c[:, None] * freqs[None, :]
    cr, sr, cc, sc = np.cos(ang_r), np.sin(ang_r), np.cos(ang_c), np.sin(ang_c)
    zero = np.zeros_like(sr)
    cos = np.concatenate([cr, cr, cc, cc], axis=-1)
    sina = np.concatenate([-sr, zero, -sc, zero], axis=-1)
    sinb = np.concatenate([zero, sr, zero, sc], axis=-1)
    reps = LANES // HEAD_DIM
    pad = lambda t, v: np.concatenate([np.full((CTX, LANES), v, np.float32), np.tile(t, (1, reps))], axis=0)
    return tuple(jnp.asarray(pad(t, v), F32) for t, v in ((cos, 1.0), (sina, 0.0), (sinb, 0.0)))


def _inproj(x, ctx, mod3, g_mix, w_in_p, g_q, g_k):
    b, seq = x.shape[0], x.shape[1]
    tot = CTX + seq
    cos, sina, sinb = _rope_tables(seq)
    reps = LANES // HEAD_DIM
    gq = (jnp.tile(g_q, reps) * (HEAD_DIM ** -0.5 * LOG2E)).reshape(1, LANES)
    gk = jnp.tile(g_k, reps).reshape(1, LANES)
    lane = np.arange(LANES)
    bd = lane[:, None] // HEAD_DIM == lane[None, :] // HEAD_DIM
    bd = jnp.asarray(np.concatenate([bd, bd], axis=0), BF16)
    nsteps = tot // TM_IN
    sub = TM_IN // CTX
    const = lambda *shape: pl.BlockSpec(shape, lambda bb, s: (0,) * len(shape))
    tab = pl.BlockSpec((TM_IN, LANES), lambda bb, s: (s, 0))
    out = lambda w: pl.BlockSpec((1, TM_IN, w), lambda bb, s: (bb, s, 0))
    shp = lambda w, dt: jax.ShapeDtypeStruct((b, tot, w), dt)
    xsub = lambda j: pl.BlockSpec((1, CTX, D_MODEL), lambda bb, s: (bb, jnp.maximum(sub * s + j - 1, 0), 0))
    return pl.pallas_call(
        _inproj_kernel,
        grid=(b, nsteps),
        in_specs=[xsub(0), xsub(1), xsub(2),
                  pl.BlockSpec((1, CTX, D_MODEL), lambda bb, s: (bb, 0, 0)),
                  pl.BlockSpec((1, 1, 6 * D_MODEL), lambda bb, s: (bb, 0, 0)),
                  pl.BlockSpec((1, 1, 6 * D_MODEL), lambda bb, s: (b, 0, 0)),
                  const(1, D_MODEL), const(D_MODEL, IN_COLS_PAD), const(1, LANES), const(1, LANES),
                  tab, tab, tab, const(2 * LANES, LANES)],
        out_specs=[out(ATTN_WIDTH), out(2 * LANES), out(2 * LANES), out(SSM_WIDTH), out(XBC_WIDTH), out(LANES)],
        out_shape=[shp(ATTN_WIDTH, BF16), shp(2 * LANES, BF16), shp(2 * LANES, BF16),
                   shp(SSM_WIDTH, BF16), shp(XBC_WIDTH, BF16), shp(LANES, F32)],
        compiler_params=_params("arbitrary", "arbitrary"),
        name="inproj",
    )(x, x, x, ctx, mod3, mod3, g_mix.reshape(1, -1), w_in_p, gq, gk, cos, sina, sinb, bd)


def _place_kv_head(t, kh, par):
    src = 0 if par == kh else 1
    lane = lax.broadcasted_iota(jnp.int32, (t.shape[0], LANES), 1)
    half = (lane >= HEAD_DIM) if par else (lane < HEAD_DIM)
    tile = t[:, src * LANES:(src + 1) * LANES]
    return jnp.where(half, tile, jnp.zeros_like(tile))


def _attn_kernel(sink_ref, q_ref, k_ref, v_ref, o_ref, kx_scr, vx_scr):
    seq = o_ref.shape[1]
    nblk = seq // BLOCK
    qi = lax.broadcasted_iota(jnp.int32, (BLOCK, BLOCK), 0)
    kj = lax.broadcasted_iota(jnp.int32, (BLOCK, BLOCK), 1)
    zero = jnp.zeros((BLOCK, BLOCK), F32)
    tri_prev = jnp.where(kj >= qi, zero, NEG_INF)
    tri_next = jnp.where(kj <= qi, zero, NEG_INF)
    combos = [(kh, par) for kh in range(KV_HEADS) for par in range(2)]
    for idx, (kh, par) in enumerate(combos):
        kx_scr[idx] = _place_kv_head(k_ref[0, 0:CTX, :], kh, par)
        vx_scr[idx] = _place_kv_head(v_ref[0, 0:CTX, :], kh, par)

    def block(nb, carry):
        r0 = pl.multiple_of(CTX + nb * BLOCK, BLOCK)
        rn = pl.multiple_of(CTX + jnp.minimum(nb + 1, nblk - 1) * BLOCK, BLOCK)
        rp = pl.multiple_of(CTX - BLOCK + nb * BLOCK, BLOCK)
        kloc = jnp.concatenate([k_ref[0, pl.ds(rp, 2 * BLOCK), :], k_ref[0, pl.ds(rn, BLOCK), :]], axis=0)
        vloc = jnp.concatenate([v_ref[0, pl.ds(rp, 2 * BLOCK), :], v_ref[0, pl.ds(rn, BLOCK), :]], axis=0)
        bias_p = jnp.where(nb > 0, tri_prev, NEG_INF)
        bias_n = jnp.where(nb < nblk - 1, tri_next, NEG_INF)
        q = q_ref[0, pl.ds(r0, BLOCK), :]
        for kh in range(KV_HEADS):
            q2 = jnp.concatenate([q[:, (2 * kh) * LANES:(2 * kh + 1) * LANES],
                                  q[:, (2 * kh + 1) * LANES:(2 * kh + 2) * LANES]], axis=0)
            acc = [None, None]
            for par in range(2):
                idx = combos.index((kh, par))
                s_loc = _dot_nt(q2, _place_kv_head(kloc, kh, par))
                s_ctx = _dot_nt(q2, kx_scr[idx])
                vl = _place_kv_head(vloc, kh, par)
                es, inv = [], []
                for r in range(2):
                    head = 4 * kh + par + 2 * r
                    rs = slice(r * BLOCK, (r + 1) * BLOCK)
                    sv = jnp.concatenate([s_loc[rs, 0:BLOCK] + bias_p, s_loc[rs, BLOCK:2 * BLOCK],
                                          s_loc[rs, 2 * BLOCK:3 * BLOCK] + bias_n, s_ctx[rs]], axis=1)
                    sk = sink_ref[head] * LOG2E
                    m = jnp.maximum(jnp.max(sv, axis=-1, keepdims=True), sk)
                    e = jnp.exp2(sv - m)
                    inv.append(1.0 / (jnp.sum(e, axis=-1, keepdims=True) + jnp.exp2(sk - m)))
                    es.append(e.astype(BF16))
                e16 = jnp.concatenate(es, axis=0)
                o2 = _dot(e16[:, 0:3 * BLOCK], vl) + _dot(e16[:, 3 * BLOCK:], vx_scr[idx])
                for r in range(2):
                    o = o2[r * BLOCK:(r + 1) * BLOCK] * inv[r]
                    acc[r] = o if acc[r] is None else acc[r] + o
            for r in range(2):
                c = 2 * kh + r
                o_ref[0, pl.ds(pl.multiple_of(nb * BLOCK, BLOCK), BLOCK), c * LANES:(c + 1) * LANES] = acc[r].astype(BF16)
        return carry

    lax.fori_loop(0, nblk, block, 0)


def _attention(q, kk, vv, sink):
    b, tot = q.shape[0], q.shape[1]
    seq = tot - CTX
    whole = lambda w: pl.BlockSpec((1, tot, w), lambda bb: (bb, 0, 0))
    return pl.pallas_call(
        _attn_kernel,
        grid=(b,),
        in_specs=[pl.BlockSpec(memory_space=pltpu.SMEM), whole(ATTN_WIDTH), whole(2 * LANES), whole(2 * LANES)],
        out_specs=pl.BlockSpec((1, seq, ATTN_WIDTH), lambda bb: (bb, 0, 0)),
        out_shape=jax.ShapeDtypeStruct((b, seq, ATTN_WIDTH), BF16),
        scratch_shapes=[pltpu.VMEM((2 * KV_HEADS, CTX, LANES), BF16), pltpu.VMEM((2 * KV_HEADS, CTX, LANES), BF16)],
        compiler_params=_params("arbitrary"),
        name="attn",
    )(sink, q, kk, vv)


def _ssd_selectors():
    nh2 = 2 * SSM_HEADS
    npairs = SSM_HEADS // 2
    k = np.arange(LANES)
    part, q = k // (2 * nh2), k % (2 * nh2)
    live = part < 3
    col = np.arange(nh2 * LANES)
    tile, lane = col // LANES, col % LANES
    kind, blk = tile // (nh2 // 2), tile % (nh2 // 2)
    src = kind * nh2 + (blk // npairs) * SSM_HEADS + 2 * (blk % npairs) + (lane >= SSM_HEAD_DIM)
    pairsel = live[:, None] & (q[:, None] == src[None, :])
    segind = live[:, None] & (q[:, None] < nh2) & (q[:, None] == tile[None, :])
    return jnp.asarray(pairsel, BF16), jnp.asarray(segind, BF16)


def _ssd_kernel(x_ref, dt_ref, z_ref, cw_ref, cb_ref, dtb_ref, alog_ref, dsk_ref, g_ref, tri_ref, shift_ref,
                pairsel_ref, segind_ref,
                y_ref, hf_scr, hb_scr, yacc_scr, stb_scr, decb_scr, eb_scr, cbuf_scr, segrhs_scr):
    tot = x_ref.shape[1]
    nchunks = tot // CHUNK
    nctx = CTX // CHUNK
    npairs = SSM_HEADS // 2
    pairs_per_group = npairs // SSM_GROUPS
    group_cols = pairs_per_group * LANES

    hf_scr[...] = jnp.zeros_like(hf_scr)
    hb_scr[...] = jnp.zeros_like(hb_scr)
    segrhs_scr[...] = segind_ref[...]

    def chunk_pass(chunk, carry):
        is_lat = chunk >= nctx
        lc = jnp.maximum(chunk - nctx, 0)
        c0 = pl.multiple_of(chunk * CHUNK, CHUNK)
        p0 = pl.multiple_of(jnp.maximum(chunk * CHUNK - HALO, 0), HALO)
        n0 = pl.multiple_of(jnp.minimum(chunk * CHUNK + CHUNK, tot - HALO), HALO)

        x16 = x_ref[0, pl.ds(c0, CHUNK), :]
        has_prev = (chunk != 0) & (chunk != nctx)
        has_next = (chunk != nctx - 1) & (chunk != nchunks - 1)
        halo_p = x_ref[0, pl.ds(p0, HALO), :]
        halo_n = x_ref[0, pl.ds(n0, HALO), :]
        xext = jnp.concatenate([jnp.where(has_prev, halo_p, jnp.zeros_like(halo_p)), x16,
                                jnp.where(has_next, halo_n, jnp.zeros_like(halo_n))], axis=0)
        shifted = _dot(shift_ref[...], xext)
        u = (cw_ref[0:1, :] * shifted[0:CHUNK] + cw_ref[1:2, :] * x16.astype(F32)
             + cw_ref[2:3, :] * shifted[CHUNK:2 * CHUNK] + cb_ref[...])
        u = u * _sigmoid(u)
        xs = u[:, 0:SSM_WIDTH]
        bmat = u[:, SSM_WIDTH:SSM_WIDTH + SSM_GROUPS * D_STATE]
        cmat = u[:, SSM_WIDTH + SSM_GROUPS * D_STATE:XBC_WIDTH]
        cb16 = [cmat[:, g * D_STATE:(g + 1) * D_STATE].astype(BF16) for g in range(SSM_GROUPS)]
        bb16 = [bmat[:, g * D_STATE:(g + 1) * D_STATE].astype(BF16) for g in range(SSM_GROUPS)]
        bt16 = [bmat[:, g * D_STATE:(g + 1) * D_STATE].T.astype(BF16) for g in range(SSM_GROUPS)]

        nh2 = 2 * SSM_HEADS
        dtv = _softplus(dt_ref[0, pl.ds(c0, CHUNK), :].T[0:nh2, :] + dtb_ref[...])
        a = dtv * (-jnp.exp(alog_ref[...]) * LOG2E)

        def split3(v):
            hi = v.astype(BF16)
            r1 = v - hi.astype(F32)
            mid = r1.astype(BF16)
            return hi, mid, (r1 - mid.astype(F32)).astype(BF16)

        parts = jnp.concatenate(split3(a), axis=0)
        fold = lambda p: p[0:nh2] + p[nh2:2 * nh2] + p[2 * nh2:3 * nh2]
        prefix = fold(_dot(parts, tri_ref[0]))
        suffix = fold(_dot(parts, tri_ref[1]))
        head_row = lax.broadcasted_iota(jnp.int32, (nh2, CHUNK), 0)
        ac_t = jnp.where(head_row < SSM_HEADS, prefix, suffix)

        v32 = [p.astype(F32) for p in split3(jnp.concatenate([ac_t, dtv], axis=0))]
        ones3 = (lax.broadcasted_iota(jnp.int32, (CHUNK - 6 * nh2, CHUNK), 0) < 3).astype(F32)
        cols3 = jnp.concatenate(v32 + [ones3], axis=0).T.astype(BF16)
        pairs = _dot(cols3, pairsel_ref[...])
        for q in range(nh2):
            neg = [-p[q:q + 1, :] for p in v32]
            tile = jnp.concatenate(neg + [jnp.zeros((BF16_SUBLANES - 3, CHUNK), F32)], axis=0)
            segrhs_scr[6 * nh2:6 * nh2 + BF16_SUBLANES, q * CHUNK:(q + 1) * CHUNK] = tile.astype(BF16)
        seg_all = _dot(cols3, segrhs_scr[...])

        half0 = lax.broadcasted_iota(jnp.int32, (CHUNK, LANES), 1) < SSM_HEAD_DIM
        ii = lax.broadcasted_iota(jnp.int32, (CHUNK, CHUNK), 0)
        jj = lax.broadcasted_iota(jnp.int32, (CHUNK, CHUNK), 1)
        causal = [jj <= ii, jj >= ii]
        last = [CHUNK - 1, 0]

        arg_pair, tot_pair, xdt_pair, xdtd_pair = [], [], [], []
        for d in range(2):
            args, tots, xdts, xdtds = [], [], [], []
            for m in range(npairs):
                blk = d * npairs + m
                arg = pairs[:, blk * LANES:(blk + 1) * LANES]
                dtp = pairs[:, (nh2 // 2 + blk) * LANES:(nh2 // 2 + blk + 1) * LANES]
                total = arg[last[d]:last[d] + 1, :]
                xdt = xs[:, m * LANES:(m + 1) * LANES] * dtp
                args.append(arg)
                tots.append(total)
                xdts.append(xdt)
                xdtds.append(xdt * jnp.exp2(total - arg))
            arg_pair.append(args)
            tot_pair.append(tots)
            xdt_pair.append(xdts)
            xdtd_pair.append(xdtds)

        hf_start = [hf_scr[:, g * group_cols:(g + 1) * group_cols] for g in range(SSM_GROUPS)]

        for g in range(SSM_GROUPS):
            pr = range(g * pairs_per_group, (g + 1) * pairs_per_group)
            gs = slice(g * group_cols, (g + 1) * group_cols)
            st_f = _dot(bt16[g], jnp.concatenate([xdtd_pair[0][m] for m in pr], axis=1).astype(BF16))
            st_b = _dot(bt16[g], jnp.concatenate([xdtd_pair[1][m] for m in pr], axis=1).astype(BF16))
            dec_f = jnp.concatenate([jnp.exp2(tot_pair[0][m]) for m in pr], axis=1)
            dec_b = jnp.concatenate([jnp.exp2(tot_pair[1][m]) for m in pr], axis=1)
            hf_scr[:, gs] = hf_start[g] * dec_f + st_f
            stb_scr[chunk, :, gs] = st_b
            decb_scr[chunk, :, gs] = jnp.broadcast_to(dec_b, (8, group_cols))

        @pl.when(is_lat)
        def _():
            r0 = pl.multiple_of(lc * CHUNK, CHUNK)
            cbuf_scr[lc] = jnp.concatenate(cb16, axis=1)
            for g in range(SSM_GROUPS):
                gmat = _dot_nt(cb16[g], bb16[g])
                yoff = _dot(cb16[g], hf_start[g].astype(BF16))
                for mm in range(pairs_per_group):
                    m = g * pairs_per_group + mm
                    ms = slice(m * LANES, (m + 1) * LANES)
                    smats, rhs = [], []
                    for d in range(2):
                        for hh in range(2):
                            r = d * SSM_HEADS + 2 * m + hh
                            seg = seg_all[:, r * CHUNK:(r + 1) * CHUNK]
                            smats.append(jnp.where(causal[d], gmat * jnp.exp2(seg), 0.0).astype(BF16))
                        xd = xdt_pair[d][m]
                        rhs += [jnp.where(half0, xd, 0.0), jnp.where(half0, 0.0, xd)]
                    ydiag = _dot(jnp.concatenate(smats, axis=1), jnp.concatenate(rhs, axis=0).astype(BF16))
                    yacc_scr[pl.ds(r0, CHUNK), ms] = (ydiag + yoff[:, mm * LANES:(mm + 1) * LANES]
                                                      * jnp.exp2(arg_pair[0][m]) + dsk_ref[:, ms] * xs[:, ms])
                    eb_scr[lc, :, ms] = jnp.exp2(arg_pair[1][m])
        return carry

    lax.fori_loop(0, nchunks, chunk_pass, 0)

    def carry_pass(i, carry):
        chunk = jnp.where(i < nctx, nctx - 1 - i, nchunks + nctx - 1 - i)
        hb = hb_scr[...]

        @pl.when(chunk >= nctx)
        def _():
            lc = chunk - nctx
            r0 = pl.multiple_of(lc * CHUNK, CHUNK)
            cb = cbuf_scr[lc]
            yoff = jnp.concatenate(
                [_dot(cb[:, g * D_STATE:(g + 1) * D_STATE], hb[:, g * group_cols:(g + 1) * group_cols].astype(BF16))
                 for g in range(SSM_GROUPS)], axis=1)
            zf = z_ref[0, pl.ds(pl.multiple_of(chunk * CHUNK, CHUNK), CHUNK), :].astype(F32)
            yt = (yacc_scr[pl.ds(r0, CHUNK), :] + yoff * eb_scr[lc]) * (zf * _sigmoid(zf))
            gw = SSM_WIDTH // SSM_GROUPS
            for g in range(SSM_GROUPS):
                yg = yt[:, g * gw:(g + 1) * gw]
                ms = jnp.mean(yg * yg, axis=-1, keepdims=True)
                y_ref[0, pl.ds(r0, CHUNK), g * gw:(g + 1) * gw] = (yg * lax.rsqrt(ms + EPS)
                                                                   * g_ref[:, g * gw:(g + 1) * gw]).astype(BF16)

        hb_scr[...] = hb * decb_scr[chunk, 0:1, :] + stb_scr[chunk]
        return carry

    lax.fori_loop(0, nchunks, carry_pass, 0)


def _ssd(xbc, dt, z, conv_w, conv_b, dtb16, alog16, dsk, g_ssm):
    b, tot = xbc.shape[0], xbc.shape[1]
    seq = tot - CTX
    nchunks = tot // CHUNK
    nctx = CTX // CHUNK
    nlat = nchunks - nctx
    const = lambda *shape: pl.BlockSpec(shape, lambda bb: (0,) * len(shape))
    whole = lambda rows, w: pl.BlockSpec((1, rows, w), lambda bb: (bb, 0, 0))
    jdx = np.arange(CHUNK)
    tri = jnp.asarray(np.stack([jdx[:, None] <= jdx[None, :], jdx[:, None] >= jdx[None, :]]), BF16)
    kdx = np.arange(CHUNK + 2 * HALO)
    shift = jnp.asarray(np.concatenate([kdx[None, :] == jdx[:, None] + HALO - 1,
                                        kdx[None, :] == jdx[:, None] + HALO + 1], axis=0), BF16)
    return pl.pallas_call(
        _ssd_kernel,
        grid=(b,),
        in_specs=[whole(tot, XBC_WIDTH), whole(tot, LANES), whole(tot, SSM_WIDTH),
                  const(3, XBC_WIDTH), const(1, XBC_WIDTH), const(2 * SSM_HEADS, CHUNK), const(2 * SSM_HEADS, CHUNK),
                  const(1, SSM_WIDTH), const(1, SSM_WIDTH), const(2, CHUNK, CHUNK),
                  const(2 * CHUNK, CHUNK + 2 * HALO),
                  const(LANES, 2 * SSM_HEADS * LANES), const(LANES, 2 * SSM_HEADS * LANES)],
        out_specs=whole(seq, SSM_WIDTH),
        out_shape=jax.ShapeDtypeStruct((b, seq, SSM_WIDTH), BF16),
        scratch_shapes=[pltpu.VMEM((D_STATE, SSM_WIDTH), F32),
                        pltpu.VMEM((D_STATE, SSM_WIDTH), F32),
                        pltpu.VMEM((seq, SSM_WIDTH), F32),
                        pltpu.VMEM((nchunks, D_STATE, SSM_WIDTH), F32),
                        pltpu.VMEM((nchunks, 8, SSM_WIDTH), F32),
                        pltpu.VMEM((nlat, CHUNK, SSM_WIDTH), F32),
                        pltpu.VMEM((nlat, CHUNK, SSM_GROUPS * D_STATE), BF16),
                        pltpu.VMEM((LANES, 2 * SSM_HEADS * LANES), BF16)],
        compiler_params=_params("arbitrary"),
        name="ssd",
    )(xbc, dt, z, conv_w, conv_b, dtb16, alog16, dsk, g_ssm, tri, shift, *_ssd_selectors())


def _outproj_kernel(a_ref, y_ref, x_ref, w_ref, mod_ref, g_ref, xn_ref, h2_ref):
    o = _dot(a_ref[0], w_ref[0:ATTN_WIDTH, :]) + _dot(y_ref[0], w_ref[ATTN_WIDTH:D_MODEL, :])
    gate = mod_ref[0, :, 2 * D_MODEL:3 * D_MODEL]
    xn = x_ref[0] + gate * o
    xn_ref[0] = xn
    ms = jnp.mean(xn * xn, axis=-1, keepdims=True)
    hn = xn * lax.rsqrt(ms + EPS) * g_ref[...]
    shift = mod_ref[0, :, 3 * D_MODEL:4 * D_MODEL]
    scale = mod_ref[0, :, 4 * D_MODEL:5 * D_MODEL]
    h2_ref[0] = (hn * (1.0 + scale) + shift).astype(BF16)


def _outproj(attn, y, x, w_out16, mod3, g_ffn):
    b, seq = x.shape[0], x.shape[1]
    row = lambda w: pl.BlockSpec((1, TM_OUT, w), lambda bb, i: (bb, i, 0))
    return pl.pallas_call(
        _outproj_kernel,
        grid=(b, seq // TM_OUT),
        in_specs=[row(ATTN_WIDTH), row(SSM_WIDTH), row(D_MODEL),
                  pl.BlockSpec((D_MODEL, D_MODEL), lambda bb, i: (0, 0)),
                  pl.BlockSpec((1, 1, 6 * D_MODEL), lambda bb, i: (bb, 0, 0)),
                  pl.BlockSpec((1, D_MODEL), lambda bb, i: (0, 0))],
        out_specs=[row(D_MODEL), row(D_MODEL)],
        out_shape=[jax.ShapeDtypeStruct((b, seq, D_MODEL), F32), jax.ShapeDtypeStruct((b, seq, D_MODEL), BF16)],
        compiler_params=_params("arbitrary", "arbitrary"),
        name="outproj",
    )(attn, y, x, w_out16, mod3, g_ffn.reshape(1, -1))


def _ffn_kernel(hm_ref, hp_ref, hn_ref, xn_ref, wa_ref, wg_ref, cwa_ref, cwg_ref, cba_ref, cbg_ref, wd_ref,
                mod_ref, o_ref, lhs_scr, acc_scr):
    i = pl.program_id(1)
    j = pl.program_id(2)
    tm = hm_ref.shape[1]

    @pl.when(j == 0)
    def _():
        lhs_scr[0:HALO, :] = jnp.where(i > 0, hp_ref[0], jnp.zeros_like(hp_ref[0]))
        lhs_scr[HALO:HALO + tm, :] = hm_ref[0]
        lhs_scr[HALO + tm:, :] = jnp.where(i < pl.num_programs(1) - 1, hn_ref[0], jnp.zeros_like(hn_ref[0]))
        acc_scr[...] = jnp.zeros_like(acc_scr)

    lhs = lhs_scr[...]
    nrows = lhs.shape[0]

    def conv(w_ref, cw_ref, cb_ref):
        uu = _dot(lhs, w_ref[...])
        up = pltpu.roll(uu, 1, 0)[HALO:HALO + tm]
        dn = pltpu.roll(uu, nrows - 1, 0)[HALO:HALO + tm]
        return cw_ref[0:1, :] * up + cw_ref[1:2, :] * uu[HALO:HALO + tm] + cw_ref[2:3, :] * dn + cb_ref[...]

    ua = conv(wa_ref, cwa_ref, cba_ref)
    ug = conv(wg_ref, cwg_ref, cbg_ref)
    act = (ua * (ug * _sigmoid(ug))).astype(BF16)
    acc_scr[...] += _dot(act, wd_ref[...])

    @pl.when(j == pl.num_programs(2) - 1)
    def _():
        o_ref[0] = xn_ref[0] + mod_ref[0, :, 5 * D_MODEL:6 * D_MODEL] * acc_scr[...]


def _ffn(h2, xn, w_up16, conv_w, conv_b, w_down16, mod3):
    b, seq = xn.shape[0], xn.shape[1]
    nj = D_FF // TF_FFN
    ni = seq // TM_FFN
    hpt = TM_FFN // HALO
    nhalo = seq // HALO
    cb = conv_b.reshape(1, -1)
    return pl.pallas_call(
        _ffn_kernel,
        grid=(b, ni, nj),
        in_specs=[pl.BlockSpec((1, TM_FFN, D_MODEL), lambda bb, i, j: (bb, i, 0)),
                  pl.BlockSpec((1, HALO, D_MODEL), lambda bb, i, j: (bb, jnp.maximum(i * hpt - 1, 0), 0)),
                  pl.BlockSpec((1, HALO, D_MODEL), lambda bb, i, j: (bb, jnp.minimum(i * hpt + hpt, nhalo - 1), 0)),
                  pl.BlockSpec((1, TM_FFN, D_MODEL), lambda bb, i, j: (bb, i, 0)),
                  pl.BlockSpec((D_MODEL, TF_FFN), lambda bb, i, j: (0, j)),
                  pl.BlockSpec((D_MODEL, TF_FFN), lambda bb, i, j: (0, nj + j)),
                  pl.BlockSpec((3, TF_FFN), lambda bb, i, j: (0, j)),
                  pl.BlockSpec((3, TF_FFN), lambda bb, i, j: (0, nj + j)),
                  pl.BlockSpec((1, TF_FFN), lambda bb, i, j: (0, j)),
                  pl.BlockSpec((1, TF_FFN), lambda bb, i, j: (0, nj + j)),
                  pl.BlockSpec((TF_FFN, D_MODEL), lambda bb, i, j: (j, 0)),
                  pl.BlockSpec((1, 1, 6 * D_MODEL), lambda bb, i, j: (bb, 0, 0))],
        out_specs=pl.BlockSpec((1, TM_FFN, D_MODEL), lambda bb, i, j: (bb, i, 0)),
        out_shape=jax.ShapeDtypeStruct((b, seq, D_MODEL), F32),
        scratch_shapes=[pltpu.VMEM((TM_FFN + 2 * HALO, D_MODEL), BF16), pltpu.VMEM((TM_FFN, D_MODEL), F32)],
        compiler_params=_params("arbitrary", "arbitrary", "arbitrary"),
        name="ffn",
    )(h2, h2, h2, xn, w_up16, w_up16, conv_w, conv_w, cb, cb, w_down16, mod3)


def _layer(x, c, ctx, c_ctx, w_mod, b_mod, g_mix, w_in, g_q, g_k, sink, ssm_conv_w, ssm_conv_b,
           a_log, dt_bias, d_skip, g_ssm, w_out, g_ffn, w_up, ffn_conv_w, ffn_conv_b, w_down):
    mod3 = _modulation(c, c_ctx, w_mod, b_mod)

    w_in_p = jnp.pad(w_in, ((0, 0), (0, IN_COLS_PAD - IN_COLS))).astype(BF16)
    q, kk, vv, z, xbc, dt = _inproj(x, ctx, mod3, g_mix, w_in_p, g_q, g_k)
    attn = _attention(q, kk, vv, sink)

    head_rows = lambda t: jnp.broadcast_to(t.reshape(2 * SSM_HEADS, 1), (2 * SSM_HEADS, CHUNK))
    y = _ssd(xbc, dt, z, ssm_conv_w, ssm_conv_b.reshape(1, -1), head_rows(dt_bias), head_rows(a_log),
             jnp.repeat(d_skip, SSM_HEAD_DIM).reshape(1, -1), g_ssm.reshape(1, -1))

    xn, h2 = _outproj(attn, y, x, w_out.astype(BF16), mod3, g_ffn)
    return _ffn(h2, xn, w_up.astype(BF16), ffn_conv_w, ffn_conv_b, w_down.astype(BF16), mod3)


def kernel(x, c, ctx, c_ctx, w_mod, b_mod, g_mix, w_in, g_q, g_k, sink, ssm_conv_w, ssm_conv_b,
           a_log, dt_bias, d_skip, g_ssm, w_out, g_ffn, w_up, ffn_conv_w, ffn_conv_b, w_down):
    assert w_mod.shape[0] == 1, "single-layer block"
    first = lambda t: t.reshape(t.shape[1:])
    return _layer(x, c, ctx, c_ctx, first(w_mod), first(b_mod), first(g_mix), first(w_in), first(g_q),
                  first(g_k), first(sink), first(ssm_conv_w), first(ssm_conv_b), first(a_log),
                  first(dt_bias), first(d_skip), first(g_ssm), first(w_out), first(g_ffn), first(w_up),
                  first(ffn_conv_w), first(ffn_conv_b), first(w_down))
```

```python
import jax
import jax.numpy as jnp
import numpy as np
from jax import lax
from jax.experimental import pallas as pl
from jax.experimental.pallas import tpu as pltpu

F32 = jnp.float32
BF16 = jnp.bfloat16

D_MODEL = 1024
CTX = 256
GRID_W = 64
HEAD_DIM = 64
ATTN_HEADS = 8
KV_HEADS = 2
ATTN_WIDTH = ATTN_HEADS * HEAD_DIM
KV_WIDTH = KV_HEADS * HEAD_DIM
WINDOW = 128
BLOCK = 128
ROPE_BASE = 10000.0
SSM_WIDTH = D_MODEL - ATTN_WIDTH
SSM_HEAD_DIM = 64
SSM_HEADS = SSM_WIDTH // SSM_HEAD_DIM
SSM_GROUPS = 2
D_STATE = 128
XBC_WIDTH = SSM_WIDTH + 2 * SSM_GROUPS * D_STATE
CHUNK = 128
D_FF = 2816
IN_COLS = ATTN_WIDTH + 2 * KV_WIDTH + SSM_WIDTH + XBC_WIDTH + 2 * SSM_HEADS
EPS = 1e-6
NEG_INF = -1e30
LOG2E = 1.4426950408889634

LANES = 128
BF16_SUBLANES = 16
IN_COLS_PAD = 19 * LANES
COL_Q, COL_K, COL_V = 0, ATTN_WIDTH, ATTN_WIDTH + KV_WIDTH
COL_Z = ATTN_WIDTH + 2 * KV_WIDTH
COL_XBC = COL_Z + SSM_WIDTH
COL_DT = COL_XBC + XBC_WIDTH
VMEM_LIMIT = 56 * 1024 * 1024

TM_IN = 3 * CTX
TM_OUT = 512
TM_FFN = 512
TF_FFN = 768
HALO = BF16_SUBLANES


def _params(*sem, flags=None):
    return pltpu.CompilerParams(dimension_semantics=sem, vmem_limit_bytes=VMEM_LIMIT, flags=flags)


def _sigmoid(v):
    return 1.0 / (1.0 + jnp.exp(-v))


def _softplus(v):
    return jnp.maximum(v, 0.0) + jnp.log1p(jnp.exp(-jnp.abs(v)))


def _dot(a, b):
    return jnp.dot(a, b, preferred_element_type=F32)


def _dot_nt(a, b):
    return lax.dot_general(a, b, (((1,), (1,)), ((), ())), preferred_element_type=F32)


def _mod_kernel(c_ref, cctx_ref, w_ref, b_ref, o_ref):
    rows = o_ref.shape[0]
    b = c_ref.shape[0]
    cv = jnp.concatenate([c_ref[...], cctx_ref[...], jnp.zeros((rows - b - 1, D_MODEL), F32)], axis=0)
    s = (cv * _sigmoid(cv)).astype(BF16)
    res = _dot(s, w_ref[...].astype(BF16)) + b_ref[...]
    for r in range(b + 1):
        o_ref[r] = res[r:r + 1, :]
    o_ref[b + 1:rows] = jnp.zeros((rows - b - 1, 1, res.shape[1]), F32)


def _modulation(c, c_ctx, w_mod, b_mod):
    b = c.shape[0]
    rows = 8 * ((b + 1 + 7) // 8)
    tn = 1024
    return pl.pallas_call(
        _mod_kernel,
        grid=(6 * D_MODEL // tn,),
        in_specs=[pl.BlockSpec((b, D_MODEL), lambda j: (0, 0)),
                  pl.BlockSpec((1, D_MODEL), lambda j: (0, 0)),
                  pl.BlockSpec((D_MODEL, tn), lambda j: (0, j)),
                  pl.BlockSpec((1, tn), lambda j: (0, j))],
        out_specs=pl.BlockSpec((rows, 1, tn), lambda j: (0, 0, j)),
        out_shape=jax.ShapeDtypeStruct((rows, 1, 6 * D_MODEL), F32),
        compiler_params=_params("arbitrary"),
        name="mod",
    )(c, c_ctx.reshape(1, -1), w_mod, b_mod.reshape(1, -1))


def _inproj_kernel(xa_ref, xb_ref, xc_ref, ctx_ref, modb_ref, modc_ref, gmix_ref, w_ref, gq_ref, gk_ref,
                   cos_ref, sina_ref, sinb_ref, bd_ref,
                   q_ref, k_ref, v_ref, z_ref, xbc_ref, dt_ref):
    first = pl.program_id(1) == 0

    def normed(xin, mod):
        ms = jnp.mean(xin * xin, axis=-1, keepdims=True)
        hn = xin * lax.rsqrt(ms + EPS) * gmix_ref[...]
        return (hn * (1.0 + mod[:, D_MODEL:2 * D_MODEL]) + mod[:, 0:D_MODEL]).astype(BF16)

    modb = modb_ref[0]
    h = jnp.concatenate([normed(jnp.where(first, ctx_ref[0], xa_ref[0]), jnp.where(first, modc_ref[0], modb)),
                         normed(xb_ref[0], modb), normed(xc_ref[0], modb)], axis=0)

    cos = cos_ref[...]
    sina = sina_ref[...]
    sinb = sinb_ref[...]
    bd = bd_ref[...]

    def norm_rope(pf, gain):
        ss = pf * pf
        hi = ss.astype(BF16)
        lo = (ss - hi.astype(F32)).astype(BF16)
        sums = _dot(jnp.concatenate([hi, lo], axis=1), bd)
        y = pf * lax.rsqrt(sums * (1.0 / HEAD_DIM) + EPS) * gain
        return y * cos + pltpu.roll(y, LANES - 16, 1) * sina + pltpu.roll(y, 16, 1) * sinb

    qf = _dot(h, w_ref[:, COL_Q:COL_Q + ATTN_WIDTH])
    for cidx in range(ATTN_WIDTH // LANES):
        cs = slice(cidx * LANES, (cidx + 1) * LANES)
        q_ref[0, :, cs] = norm_rope(qf[:, cs], gq_ref[...]).astype(BF16)

    kvf = _dot(h, w_ref[:, COL_K:COL_K + 2 * KV_WIDTH])
    kf = norm_rope(kvf[:, 0:KV_WIDTH], gk_ref[...])
    k_ref[0, :, 0:LANES] = kf.astype(BF16)
    k_ref[0, :, LANES:2 * LANES] = pltpu.roll(kf, HEAD_DIM, 1).astype(BF16)
    vf = kvf[:, KV_WIDTH:2 * KV_WIDTH]
    v_ref[0, :, 0:LANES] = vf.astype(BF16)
    v_ref[0, :, LANES:2 * LANES] = pltpu.roll(vf, HEAD_DIM, 1).astype(BF16)
    z_ref[0] = _dot(h, w_ref[:, COL_Z:COL_Z + SSM_WIDTH]).astype(BF16)
    xbc_ref[0] = _dot(h, w_ref[:, COL_XBC:COL_XBC + XBC_WIDTH]).astype(BF16)
    dt_ref[0] = _dot(h, w_ref[:, COL_DT:COL_DT + LANES])


def _rope_tables(seq):
    rows = seq // GRID_W
    pos_r = np.repeat(np.arange(rows), GRID_W).astype(np.float32)
    pos_c = np.tile(np.arange(GRID_W), rows).astype(np.float32)
    quarter = HEAD_DIM // 4
    freqs = (ROPE_BASE ** (-np.arange(quarter, dtype=np.float32) / quarter)).astype(np.float32)
    ang_r = pos_r[:, None] * freqs[None, :]
    ang_c = pos_c[:, None] * freqs[None, :]
    cr, sr, cc, sc = np.cos(ang_r), np.sin(ang_r), np.cos(ang_c), np.sin(ang_c)
    zero = np.zeros_like(sr)
    cos = np.concatenate([cr, cr, cc, cc], axis=-1)
    sina = np.concatenate([-sr, zero, -sc, zero], axis=-1)
    sinb = np.concatenate([zero, sr, zero, sc], axis=-1)
    reps = LANES // HEAD_DIM
    pad = lambda t, v: np.concatenate([np.full((CTX, LANES), v, np.float32), np.tile(t, (1, reps))], axis=0)
    return tuple(jnp.asarray(pad(t, v), F32) for t, v in ((cos, 1.0), (sina, 0.0), (sinb, 0.0)))


def _inproj(x, ctx, mod3, g_mix, w_in_p, g_q, g_k):
    b, seq = x.shape[0], x.shape[1]
    tot = CTX + seq
    cos, sina, sinb = _rope_tables(seq)
    reps = LANES // HEAD_DIM
    gq = (jnp.tile(g_q, reps) * (HEAD_DIM ** -0.5 * LOG2E)).reshape(1, LANES)
    gk = jnp.tile(g_k, reps).reshape(1, LANES)
    lane = np.arange(LANES)
    bd = lane[:, None] // HEAD_DIM == lane[None, :] // HEAD_DIM
    bd = jnp.asarray(np.concatenate([bd, bd], axis=0), BF16)
    nsteps = tot // TM_IN
    sub = TM_IN // CTX
    const = lambda *shape: pl.BlockSpec(shape, lambda bb, s: (0,) * len(shape))
    tab = pl.BlockSpec((TM_IN, LANES), lambda bb, s: (s, 0))
    out = lambda w: pl.BlockSpec((1, TM_IN, w), lambda bb, s: (bb, s, 0))
    shp = lambda w, dt: jax.ShapeDtypeStruct((b, tot, w), dt)
    xsub = lambda j: pl.BlockSpec((1, CTX, D_MODEL), lambda bb, s: (bb, jnp.maximum(sub * s + j - 1, 0), 0))
    return pl.pallas_call(
        _inproj_kernel,
        grid=(b, nsteps),
        in_specs=[xsub(0), xsub(1), xsub(2),
                  pl.BlockSpec((1, CTX, D_MODEL), lambda bb, s: (bb, 0, 0)),
                  pl.BlockSpec((1, 1, 6 * D_MODEL), lambda bb, s: (bb, 0, 0)),
                  pl.BlockSpec((1, 1, 6 * D_MODEL), lambda bb, s: (b, 0, 0)),
                  const(1, D_MODEL), const(D_MODEL, IN_COLS_PAD), const(1, LANES), const(1, LANES),
                  tab, tab, tab, const(2 * LANES, LANES)],
        out_specs=[out(ATTN_WIDTH), out(2 * LANES), out(2 * LANES), out(SSM_WIDTH), out(XBC_WIDTH), out(LANES)],
        out_shape=[shp(ATTN_WIDTH, BF16), shp(2 * LANES, BF16), shp(2 * LANES, BF16),
                   shp(SSM_WIDTH, BF16), shp(XBC_WIDTH, BF16), shp(LANES, F32)],
        compiler_params=_params("arbitrary", "arbitrary"),
        name="inproj",
    )(x, x, x, ctx, mod3, mod3, g_mix.reshape(1, -1), w_in_p, gq, gk, cos, sina, sinb, bd)


def _place_kv_head(t, kh, par):
    src = 0 if par == kh else 1
    lane = lax.broadcasted_iota(jnp.int32, (t.shape[0], LANES), 1)
    half = (lane >= HEAD_DIM) if par else (lane < HEAD_DIM)
    tile = t[:, src * LANES:(src + 1) * LANES]
    return jnp.where(half, tile, jnp.zeros_like(tile))


def _attn_kernel(sink_ref, q_ref, k_ref, v_ref, o_ref, kx_scr, vx_scr):
    seq = o_ref.shape[1]
    nblk = seq // BLOCK
    qi = lax.broadcasted_iota(jnp.int32, (BLOCK, BLOCK), 0)
    kj = lax.broadcasted_iota(jnp.int32, (BLOCK, BLOCK), 1)
    zero = jnp.zeros((BLOCK, BLOCK), F32)
    tri_prev = jnp.where(kj >= qi, zero, NEG_INF)
    tri_next = jnp.where(kj <= qi, zero, NEG_INF)
    combos = [(kh, par) for kh in range(KV_HEADS) for par in range(2)]
    for idx, (kh, par) in enumerate(combos):
        kx_scr[idx] = _place_kv_head(k_ref[0, 0:CTX, :], kh, par)
        vx_scr[idx] = _place_kv_head(v_ref[0, 0:CTX, :], kh, par)

    def block(nb, carry):
        r0 = pl.multiple_of(CTX + nb * BLOCK, BLOCK)
        rn = pl.multiple_of(CTX + jnp.minimum(nb + 1, nblk - 1) * BLOCK, BLOCK)
        rp = pl.multiple_of(CTX - BLOCK + nb * BLOCK, BLOCK)
        kloc = jnp.concatenate([k_ref[0, pl.ds(rp, 2 * BLOCK), :], k_ref[0, pl.ds(rn, BLOCK), :]], axis=0)
        vloc = jnp.concatenate([v_ref[0, pl.ds(rp, 2 * BLOCK), :], v_ref[0, pl.ds(rn, BLOCK), :]], axis=0)
        bias_p = jnp.where(nb > 0, tri_prev, NEG_INF)
        bias_n = jnp.where(nb < nblk - 1, tri_next, NEG_INF)
        q = q_ref[0, pl.ds(r0, BLOCK), :]
        for kh in range(KV_HEADS):
            q2 = jnp.concatenate([q[:, (2 * kh) * LANES:(2 * kh + 1) * LANES],
                                  q[:, (2 * kh + 1) * LANES:(2 * kh + 2) * LANES]], axis=0)
            acc = [None, None]
            for par in range(2):
                idx = combos.index((kh, par))
                s_loc = _dot_nt(q2, _place_kv_head(kloc, kh, par))
                s_ctx = _dot_nt(q2, kx_scr[idx])
                vl = _place_kv_head(vloc, kh, par)
                es, inv = [], []
                for r in range(2):
                    head = 4 * kh + par + 2 * r
                    rs = slice(r * BLOCK, (r + 1) * BLOCK)
                    sv = jnp.concatenate([s_loc[rs, 0:BLOCK] + bias_p, s_loc[rs, BLOCK:2 * BLOCK],
                                          s_loc[rs, 2 * BLOCK:3 * BLOCK] + bias_n, s_ctx[rs]], axis=1)
                    sk = sink_ref[head] * LOG2E
                    m = jnp.maximum(jnp.max(sv, axis=-1, keepdims=True), sk)
                    e = jnp.exp2(sv - m)
                    inv.append(1.0 / (jnp.sum(e, axis=-1, keepdims=True) + jnp.exp2(sk - m)))
                    es.append(e.astype(BF16))
                e16 = jnp.concatenate(es, axis=0)
                o2 = _dot(e16[:, 0:3 * BLOCK], vl) + _dot(e16[:, 3 * BLOCK:], vx_scr[idx])
                for r in range(2):
                    o = o2[r * BLOCK:(r + 1) * BLOCK] * inv[r]
                    acc[r] = o if acc[r] is None else acc[r] + o
            for r in range(2):
                c = 2 * kh + r
                o_ref[0, pl.ds(pl.multiple_of(nb * BLOCK, BLOCK), BLOCK), c * LANES:(c + 1) * LANES] = acc[r].astype(BF16)
        return carry

    lax.fori_loop(0, nblk, block, 0)


def _attention(q, kk, vv, sink):
    b, tot = q.shape[0], q.shape[1]
    seq = tot - CTX
    whole = lambda w: pl.BlockSpec((1, tot, w), lambda bb: (bb, 0, 0))
    return pl.pallas_call(
        _attn_kernel,
        grid=(b,),
        in_specs=[pl.BlockSpec(memory_space=pltpu.SMEM), whole(ATTN_WIDTH), whole(2 * LANES), whole(2 * LANES)],
        out_specs=pl.BlockSpec((1, seq, ATTN_WIDTH), lambda bb: (bb, 0, 0)),
        out_shape=jax.ShapeDtypeStruct((b, seq, ATTN_WIDTH), BF16),
        scratch_shapes=[pltpu.VMEM((2 * KV_HEADS, CTX, LANES), BF16), pltpu.VMEM((2 * KV_HEADS, CTX, LANES), BF16)],
        compiler_params=_params("arbitrary"),
        name="attn",
    )(sink, q, kk, vv)


def _ssd_selectors():
    nh2 = 2 * SSM_HEADS
    npairs = SSM_HEADS // 2
    k = np.arange(LANES)
    part, q = k // (2 * nh2), k % (2 * nh2)
    live = part < 3
    col = np.arange(nh2 * LANES)
    tile, lane = col // LANES, col % LANES
    kind, blk = tile // (nh2 // 2), tile % (nh2 // 2)
    src = kind * nh2 + (blk // npairs) * SSM_HEADS + 2 * (blk % npairs) + (lane >= SSM_HEAD_DIM)
    pairsel = live[:, None] & (q[:, None] == src[None, :])
    segind = live[:, None] & (q[:, None] < nh2) & (q[:, None] == tile[None, :])
    return jnp.asarray(pairsel, BF16), jnp.asarray(segind, BF16)


def _ssd_kernel(x_ref, dt_ref, z_ref, cw_ref, cb_ref, dtb_ref, alog_ref, dsk_ref, g_ref, tri_ref, shift_ref,
                pairsel_ref, segind_ref,
                y_ref, hf_scr, hb_scr, yacc_scr, stb_scr, decb_scr, eb_scr, cbuf_scr, segrhs_scr):
    tot = x_ref.shape[1]
    nchunks = tot // CHUNK
    nctx = CTX // CHUNK
    npairs = SSM_HEADS // 2
    pairs_per_group = npairs // SSM_GROUPS
    group_cols = pairs_per_group * LANES

    hf_scr[...] = jnp.zeros_like(hf_scr)
    hb_scr[...] = jnp.zeros_like(hb_scr)
    segrhs_scr[...] = segind_ref[...]

    def chunk_pass(chunk, carry):
        is_lat = chunk >= nctx
        lc = jnp.maximum(chunk - nctx, 0)
        c0 = pl.multiple_of(chunk * CHUNK, CHUNK)
        p0 = pl.multiple_of(jnp.maximum(chunk * CHUNK - HALO, 0), HALO)
        n0 = pl.multiple_of(jnp.minimum(chunk * CHUNK + CHUNK, tot - HALO), HALO)

        x16 = x_ref[0, pl.ds(c0, CHUNK), :]
        has_prev = (chunk != 0) & (chunk != nctx)
        has_next = (chunk != nctx - 1) & (chunk != nchunks - 1)
        halo_p = x_ref[0, pl.ds(p0, HALO), :]
        halo_n = x_ref[0, pl.ds(n0, HALO), :]
        xext = jnp.concatenate([jnp.where(has_prev, halo_p, jnp.zeros_like(halo_p)), x16,
                                jnp.where(has_next, halo_n, jnp.zeros_like(halo_n))], axis=0)
        shifted = _dot(shift_ref[...], xext)
        u = (cw_ref[0:1, :] * shifted[0:CHUNK] + cw_ref[1:2, :] * x16.astype(F32)
             + cw_ref[2:3, :] * shifted[CHUNK:2 * CHUNK] + cb_ref[...])
        u = u * _sigmoid(u)
        xs = u[:, 0:SSM_WIDTH]
        bmat = u[:, SSM_WIDTH:SSM_WIDTH + SSM_GROUPS * D_STATE]
        cmat = u[:, SSM_WIDTH + SSM_GROUPS * D_STATE:XBC_WIDTH]
        cb16 = [cmat[:, g * D_STATE:(g + 1) * D_STATE].astype(BF16) for g in range(SSM_GROUPS)]
        bb16 = [bmat[:, g * D_STATE:(g + 1) * D_STATE].astype(BF16) for g in range(SSM_GROUPS)]
        bt16 = [bmat[:, g * D_STATE:(g + 1) * D_STATE].T.astype(BF16) for g in range(SSM_GROUPS)]

        nh2 = 2 * SSM_HEADS
        dtv = _softplus(dt_ref[0, pl.ds(c0, CHUNK), :].T[0:nh2, :] + dtb_ref[...])
        a = dtv * (-jnp.exp(alog_ref[...]) * LOG2E)

        def split3(v):
            hi = v.astype(BF16)
            r1 = v - hi.astype(F32)
            mid = r1.astype(BF16)
            return hi, mid, (r1 - mid.astype(F32)).astype(BF16)

        parts = jnp.concatenate(split3(a), axis=0)
        fold = lambda p: p[0:nh2] + p[nh2:2 * nh2] + p[2 * nh2:3 * nh2]
        prefix = fold(_dot(parts, tri_ref[0]))
        suffix = fold(_dot(parts, tri_ref[1]))
        head_row = lax.broadcasted_iota(jnp.int32, (nh2, CHUNK), 0)
        ac_t = jnp.where(head_row < SSM_HEADS, prefix, suffix)

        v32 = [p.astype(F32) for p in split3(jnp.concatenate([ac_t, dtv], axis=0))]
        ones3 = (lax.broadcasted_iota(jnp.int32, (CHUNK - 6 * nh2, CHUNK), 0) < 3).astype(F32)
        cols3 = jnp.concatenate(v32 + [ones3], axis=0).T.astype(BF16)
        pairs = _dot(cols3, pairsel_ref[...])
        for q in range(nh2):
            neg = [-p[q:q + 1, :] for p in v32]
            tile = jnp.concatenate(neg + [jnp.zeros((BF16_SUBLANES - 3, CHUNK), F32)], axis=0)
            segrhs_scr[6 * nh2:6 * nh2 + BF16_SUBLANES, q * CHUNK:(q + 1) * CHUNK] = tile.astype(BF16)
        seg_all = _dot(cols3, segrhs_scr[...])

        half0 = lax.broadcasted_iota(jnp.int32, (CHUNK, LANES), 1) < SSM_HEAD_DIM
        ii = lax.broadcasted_iota(jnp.int32, (CHUNK, CHUNK), 0)
        jj = lax.broadcasted_iota(jnp.int32, (CHUNK, CHUNK), 1)
        causal = [jj <= ii, jj >= ii]
        last = [CHUNK - 1, 0]

        arg_pair, tot_pair, xdt_pair, xdtd_pair = [], [], [], []
        for d in range(2):
            args, tots, xdts, xdtds = [], [], [], []
            for m in range(npairs):
                blk = d * npairs + m
                arg = pairs[:, blk * LANES:(blk + 1) * LANES]
                dtp = pairs[:, (nh2 // 2 + blk) * LANES:(nh2 // 2 + blk + 1) * LANES]
                total = arg[last[d]:last[d] + 1, :]
                xdt = xs[:, m * LANES:(m + 1) * LANES] * dtp
                args.append(arg)
                tots.append(total)
                xdts.append(xdt)
                xdtds.append(xdt * jnp.exp2(total - arg))
            arg_pair.append(args)
            tot_pair.append(tots)
            xdt_pair.append(xdts)
            xdtd_pair.append(xdtds)

        hf_start = [hf_scr[:, g * group_cols:(g + 1) * group_cols] for g in range(SSM_GROUPS)]

        for g in range(SSM_GROUPS):
            pr = range(g * pairs_per_group, (g + 1) * pairs_per_group)
            gs = slice(g * group_cols, (g + 1) * group_cols)
            st_f = _dot(bt16[g], jnp.concatenate([xdtd_pair[0][m] for m in pr], axis=1).astype(BF16))
            st_b = _dot(bt16[g], jnp.concatenate([xdtd_pair[1][m] for m in pr], axis=1).astype(BF16))
            dec_f = jnp.concatenate([jnp.exp2(tot_pair[0][m]) for m in pr], axis=1)
            dec_b = jnp.concatenate([jnp.exp2(tot_pair[1][m]) for m in pr], axis=1)
            hf_scr[:, gs] = hf_start[g] * dec_f + st_f
            stb_scr[chunk, :, gs] = st_b
            decb_scr[chunk, :, gs] = jnp.broadcast_to(dec_b, (8, group_cols))

        @pl.when(is_lat)
        def _():
            r0 = pl.multiple_of(lc * CHUNK, CHUNK)
            cbuf_scr[lc] = jnp.concatenate(cb16, axis=1)
            for g in range(SSM_GROUPS):
                gmat = _dot_nt(cb16[g], bb16[g])
                yoff = _dot(cb16[g], hf_start[g].astype(BF16))
                for mm in range(pairs_per_group):
                    m = g * pairs_per_group + mm
                    ms = slice(m * LANES, (m + 1) * LANES)
                    smats, rhs = [], []
                    for d in range(2):
                        for hh in range(2):
                            r = d * SSM_HEADS + 2 * m + hh
                            seg = seg_all[:, r * CHUNK:(r + 1) * CHUNK]
                            smats.append(jnp.where(causal[d], gmat * jnp.exp2(seg), 0.0).astype(BF16))
                        xd = xdt_pair[d][m]
                        rhs += [jnp.where(half0, xd, 0.0), jnp.where(half0, 0.0, xd)]
                    ydiag = _dot(jnp.concatenate(smats, axis=1), jnp.concatenate(rhs, axis=0).astype(BF16))
                    yacc_scr[pl.ds(r0, CHUNK), ms] = (ydiag + yoff[:, mm * LANES:(mm + 1) * LANES]
                                                      * jnp.exp2(arg_pair[0][m]) + dsk_ref[:, ms] * xs[:, ms])
                    eb_scr[lc, :, ms] = jnp.exp2(arg_pair[1][m])
        return carry

    lax.fori_loop(0, nchunks, chunk_pass, 0)

    def carry_pass(i, carry):
        chunk = jnp.where(i < nctx, nctx - 1 - i, nchunks + nctx - 1 - i)
        hb = hb_scr[...]

        @pl.when(chunk >= nctx)
        def _():
            lc = chunk - nctx
            r0 = pl.multiple_of(lc * CHUNK, CHUNK)
            cb = cbuf_scr[lc]
            yoff = jnp.concatenate(
                [_dot(cb[:, g * D_STATE:(g + 1) * D_STATE], hb[:, g * group_cols:(g + 1) * group_cols].astype(BF16))
                 for g in range(SSM_GROUPS)], axis=1)
            zf = z_ref[0, pl.ds(pl.multiple_of(chunk * CHUNK, CHUNK), CHUNK), :].astype(F32)
            yt = (yacc_scr[pl.ds(r0, CHUNK), :] + yoff * eb_scr[lc]) * (zf * _sigmoid(zf))
            gw = SSM_WIDTH // SSM_GROUPS
            for g in range(SSM_GROUPS):
                yg = yt[:, g * gw:(g + 1) * gw]
                ms = jnp.mean(yg * yg, axis=-1, keepdims=True)
                y_ref[0, pl.ds(r0, CHUNK), g * gw:(g + 1) * gw] = (yg * lax.rsqrt(ms + EPS)
                                                                   * g_ref[:, g * gw:(g + 1) * gw]).astype(BF16)

        hb_scr[...] = hb * decb_scr[chunk, 0:1, :] + stb_scr[chunk]
        return carry

    lax.fori_loop(0, nchunks, carry_pass, 0)


def _ssd(xbc, dt, z, conv_w, conv_b, dtb16, alog16, dsk, g_ssm):
    b, tot = xbc.shape[0], xbc.shape[1]
    seq = tot - CTX
    nchunks = tot // CHUNK
    nctx = CTX // CHUNK
    nlat = nchunks - nctx
    const = lambda *shape: pl.BlockSpec(shape, lambda bb: (0,) * len(shape))
    whole = lambda rows, w: pl.BlockSpec((1, rows, w), lambda bb: (bb, 0, 0))
    jdx = np.arange(CHUNK)
    tri = jnp.asarray(np.stack([jdx[:, None] <= jdx[None, :], jdx[:, None] >= jdx[None, :]]), BF16)
    kdx = np.arange(CHUNK + 2 * HALO)
    shift = jnp.asarray(np.concatenate([kdx[None, :] == jdx[:, None] + HALO - 1,
                                        kdx[None, :] == jdx[:, None] + HALO + 1], axis=0), BF16)
    return pl.pallas_call(
        _ssd_kernel,
        grid=(b,),
        in_specs=[whole(tot, XBC_WIDTH), whole(tot, LANES), whole(tot, SSM_WIDTH),
                  const(3, XBC_WIDTH), const(1, XBC_WIDTH), const(2 * SSM_HEADS, CHUNK), const(2 * SSM_HEADS, CHUNK),
                  const(1, SSM_WIDTH), const(1, SSM_WIDTH), const(2, CHUNK, CHUNK),
                  const(2 * CHUNK, CHUNK + 2 * HALO),
                  const(LANES, 2 * SSM_HEADS * LANES), const(LANES, 2 * SSM_HEADS * LANES)],
        out_specs=whole(seq, SSM_WIDTH),
        out_shape=jax.ShapeDtypeStruct((b, seq, SSM_WIDTH), BF16),
        scratch_shapes=[pltpu.VMEM((D_STATE, SSM_WIDTH), F32),
                        pltpu.VMEM((D_STATE, SSM_WIDTH), F32),
                        pltpu.VMEM((seq, SSM_WIDTH), F32),
                        pltpu.VMEM((nchunks, D_STATE, SSM_WIDTH), F32),
                        pltpu.VMEM((nchunks, 8, SSM_WIDTH), F32),
                        pltpu.VMEM((nlat, CHUNK, SSM_WIDTH), F32),
                        pltpu.VMEM((nlat, CHUNK, SSM_GROUPS * D_STATE), BF16),
                        pltpu.VMEM((LANES, 2 * SSM_HEADS * LANES), BF16)],
        compiler_params=_params("arbitrary"),
        name="ssd",
    )(xbc, dt, z, conv_w, conv_b, dtb16, alog16, dsk, g_ssm, tri, shift, *_ssd_selectors())


def _outproj_kernel(a_ref, y_ref, x_ref, w_ref, mod_ref, g_ref, xn_ref, h2_ref):
    o = _dot(a_ref[0], w_ref[0:ATTN_WIDTH, :]) + _dot(y_ref[0], w_ref[ATTN_WIDTH:D_MODEL, :])
    gate = mod_ref[0, :, 2 * D_MODEL:3 * D_MODEL]
    xn = x_ref[0] + gate * o
    xn_ref[0] = xn
    ms = jnp.mean(xn * xn, axis=-1, keepdims=True)
    hn = xn * lax.rsqrt(ms + EPS) * g_ref[...]
    shift = mod_ref[0, :, 3 * D_MODEL:4 * D_MODEL]
    scale = mod_ref[0, :, 4 * D_MODEL:5 * D_MODEL]
    h2_ref[0] = (hn * (1.0 + scale) + shift).astype(BF16)


def _outproj(attn, y, x, w_out16, mod3, g_ffn):
    b, seq = x.shape[0], x.shape[1]
    row = lambda w: pl.BlockSpec((1, TM_OUT, w), lambda bb, i: (bb, i, 0))
    return pl.pallas_call(
        _outproj_kernel,
        grid=(b, seq // TM_OUT),
        in_specs=[row(ATTN_WIDTH), row(SSM_WIDTH), row(D_MODEL),
                  pl.BlockSpec((D_MODEL, D_MODEL), lambda bb, i: (0, 0)),
                  pl.BlockSpec((1, 1, 6 * D_MODEL), lambda bb, i: (bb, 0, 0)),
                  pl.BlockSpec((1, D_MODEL), lambda bb, i: (0, 0))],
        out_specs=[row(D_MODEL), row(D_MODEL)],
        out_shape=[jax.ShapeDtypeStruct((b, seq, D_MODEL), F32), jax.ShapeDtypeStruct((b, seq, D_MODEL), BF16)],
        compiler_params=_params("arbitrary", "arbitrary"),
        name="outproj",
    )(attn, y, x, w_out16, mod3, g_ffn.reshape(1, -1))


def _ffn_kernel(hm_ref, hp_ref, hn_ref, xn_ref, wa_ref, wg_ref, cwa_ref, cwg_ref, cba_ref, cbg_ref, wd_ref,
                mod_ref, o_ref, lhs_scr, acc_scr):
    i = pl.program_id(1)
    j = pl.program_id(2)
    tm = hm_ref.shape[1]

    @pl.when(j == 0)
    def _():
        lhs_scr[0:HALO, :] = jnp.where(i > 0, hp_ref[0], jnp.zeros_like(hp_ref[0]))
        lhs_scr[HALO:HALO + tm, :] = hm_ref[0]
        lhs_scr[HALO + tm:, :] = jnp.where(i < pl.num_programs(1) - 1, hn_ref[0], jnp.zeros_like(hn_ref[0]))
        acc_scr[...] = jnp.zeros_like(acc_scr)

    lhs = lhs_scr[...]
    nrows = lhs.shape[0]

    def conv(w_ref, cw_ref, cb_ref):
        uu = _dot(lhs, w_ref[...])
        up = pltpu.roll(uu, 1, 0)[HALO:HALO + tm]
        dn = pltpu.roll(uu, nrows - 1, 0)[HALO:HALO + tm]
        return cw_ref[0:1, :] * up + cw_ref[1:2, :] * uu[HALO:HALO + tm] + cw_ref[2:3, :] * dn + cb_ref[...]

    ua = conv(wa_ref, cwa_ref, cba_ref)
    ug = conv(wg_ref, cwg_ref, cbg_ref)
    act = (ua * (ug * _sigmoid(ug))).astype(BF16)
    acc_scr[...] += _dot(act, wd_ref[...])

    @pl.when(j == pl.num_programs(2) - 1)
    def _():
        o_ref[0] = xn_ref[0] + mod_ref[0, :, 5 * D_MODEL:6 * D_MODEL] * acc_scr[...]


def _ffn(h2, xn, w_up16, conv_w, conv_b, w_down16, mod3):
    b, seq = xn.shape[0], xn.shape[1]
    nj = D_FF // TF_FFN
    ni = seq // TM_FFN
    hpt = TM_FFN // HALO
    nhalo = seq // HALO
    cb = conv_b.reshape(1, -1)
    return pl.pallas_call(
        _ffn_kernel,
        grid=(b, ni, nj),
        in_specs=[pl.BlockSpec((1, TM_FFN, D_MODEL), lambda bb, i, j: (bb, i, 0)),
                  pl.BlockSpec((1, HALO, D_MODEL), lambda bb, i, j: (bb, jnp.maximum(i * hpt - 1, 0), 0)),
                  pl.BlockSpec((1, HALO, D_MODEL), lambda bb, i, j: (bb, jnp.minimum(i * hpt + hpt, nhalo - 1), 0)),
                  pl.BlockSpec((1, TM_FFN, D_MODEL), lambda bb, i, j: (bb, i, 0)),
                  pl.BlockSpec((D_MODEL, TF_FFN), lambda bb, i, j: (0, j)),
                  pl.BlockSpec((D_MODEL, TF_FFN), lambda bb, i, j: (0, nj + j)),
                  pl.BlockSpec((3, TF_FFN), lambda bb, i, j: (0, j)),
                  pl.BlockSpec((3, TF_FFN), lambda bb, i, j: (0, nj + j)),
                  pl.BlockSpec((1, TF_FFN), lambda bb, i, j: (0, j)),
                  pl.BlockSpec((1, TF_FFN), lambda bb, i, j: (0, nj + j)),
                  pl.BlockSpec((TF_FFN, D_MODEL), lambda bb, i, j: (j, 0)),
                  pl.BlockSpec((1, 1, 6 * D_MODEL), lambda bb, i, j: (bb, 0, 0))],
        out_specs=pl.BlockSpec((1, TM_FFN, D_MODEL), lambda bb, i, j: (bb, i, 0)),
        out_shape=jax.ShapeDtypeStruct((b, seq, D_MODEL), F32),
        scratch_shapes=[pltpu.VMEM((TM_FFN + 2 * HALO, D_MODEL), BF16), pltpu.VMEM((TM_FFN, D_MODEL), F32)],
        compiler_params=_params("arbitrary", "arbitrary", "arbitrary"),
        name="ffn",
    )(h2, h2, h2, xn, w_up16, w_up16, conv_w, conv_w, cb, cb, w_down16, mod3)


def _mixffn_kernel(am_ref, ap_ref, an_ref, ym_ref, yp_ref, yn_ref, xm_ref, xp_ref, xn_ref,
                   wo_ref, wup_ref, cw_ref, cb_ref, wd_ref, mod_ref, g_ref, o_ref):
    i = pl.program_id(1)
    tm = am_ref.shape[1]
    nrows = tm + 2 * HALO
    ext = lambda p, m, n: jnp.concatenate([p[0], m[0], n[0]], axis=0)
    o = (_dot(ext(ap_ref, am_ref, an_ref), wo_ref[0:ATTN_WIDTH, :])
         + _dot(ext(yp_ref, ym_ref, yn_ref), wo_ref[ATTN_WIDTH:D_MODEL, :]))
    mod = mod_ref[0]
    xnew = ext(xp_ref, xm_ref, xn_ref) + mod[:, 2 * D_MODEL:3 * D_MODEL] * o
    ms = jnp.mean(xnew * xnew, axis=-1, keepdims=True)
    hn = xnew * lax.rsqrt(ms + EPS) * g_ref[...]
    h2 = hn * (1.0 + mod[:, 4 * D_MODEL:5 * D_MODEL]) + mod[:, 3 * D_MODEL:4 * D_MODEL]
    row = lax.broadcasted_iota(jnp.int32, (nrows, 1), 0)
    lo = jnp.where(i > 0, 0, HALO)
    hi = jnp.where(i < pl.num_programs(1) - 1, nrows, nrows - HALO)
    lhs = jnp.where(row >= lo, jnp.where(row < hi, h2, 0.0), 0.0).astype(BF16)

    def conv(c0, c1):
        uu = _dot(lhs, wup_ref[:, c0:c1])
        up = pltpu.roll(uu, 1, 0)[HALO:HALO + tm]
        dn = pltpu.roll(uu, nrows - 1, 0)[HALO:HALO + tm]
        return (cw_ref[0:1, c0:c1] * up + cw_ref[1:2, c0:c1] * uu[HALO:HALO + tm]
                + cw_ref[2:3, c0:c1] * dn + cb_ref[:, c0:c1])

    acc = None
    for c0 in range(0, D_FF, TF_FFN):
        c1 = min(c0 + TF_FFN, D_FF)
        ua = conv(c0, c1)
        ug = conv(D_FF + c0, D_FF + c1)
        act = (ua * (ug * _sigmoid(ug))).astype(BF16)
        part = _dot(act, wd_ref[c0:c1, :])
        acc = part if acc is None else acc + part
    o_ref[0] = xnew[HALO:HALO + tm] + mod[:, 5 * D_MODEL:6 * D_MODEL] * acc


def _mixffn(attn, y, x, w_out16, w_up16, conv_w, conv_b, w_down16, mod3, g_ffn):
    b, seq = x.shape[0], x.shape[1]
    ni = seq // TM_FFN
    hpt = TM_FFN // HALO
    nhalo = seq // HALO
    main = lambda w: pl.BlockSpec((1, TM_FFN, w), lambda bb, i: (bb, i, 0))
    prev = lambda w: pl.BlockSpec((1, HALO, w), lambda bb, i: (bb, jnp.maximum(i * hpt - 1, 0), 0))
    nxt = lambda w: pl.BlockSpec((1, HALO, w), lambda bb, i: (bb, jnp.minimum(i * hpt + hpt, nhalo - 1), 0))
    resident = lambda *shape: pl.BlockSpec(shape, lambda bb, i: (0,) * len(shape), pipeline_mode=pl.Buffered(1))
    trio = lambda w: [main(w), prev(w), nxt(w)]
    return pl.pallas_call(
        _mixffn_kernel,
        grid=(b, ni),
        in_specs=trio(ATTN_WIDTH) + trio(SSM_WIDTH) + trio(D_MODEL) + [
            resident(D_MODEL, D_MODEL), resident(D_MODEL, 2 * D_FF), resident(3, 2 * D_FF), resident(1, 2 * D_FF),
            resident(D_FF, D_MODEL),
            pl.BlockSpec((1, 1, 6 * D_MODEL), lambda bb, i: (bb, 0, 0)),
            resident(1, D_MODEL)],
        out_specs=main(D_MODEL),
        out_shape=jax.ShapeDtypeStruct((b, seq, D_MODEL), F32),
        compiler_params=_params("arbitrary", "arbitrary"),
        name="mixffn",
    )(attn, attn, attn, y, y, y, x, x, x, w_out16, w_up16, conv_w, conv_b.reshape(1, -1), w_down16, mod3,
      g_ffn.reshape(1, -1))


def _layer(x, c, ctx, c_ctx, w_mod, b_mod, g_mix, w_in, g_q, g_k, sink, ssm_conv_w, ssm_conv_b,
           a_log, dt_bias, d_skip, g_ssm, w_out, g_ffn, w_up, ffn_conv_w, ffn_conv_b, w_down):
    mod3 = _modulation(c, c_ctx, w_mod, b_mod)

    w_in_p = jnp.pad(w_in, ((0, 0), (0, IN_COLS_PAD - IN_COLS))).astype(BF16)
    q, kk, vv, z, xbc, dt = _inproj(x, ctx, mod3, g_mix, w_in_p, g_q, g_k)
    attn = _attention(q, kk, vv, sink)

    head_rows = lambda t: jnp.broadcast_to(t.reshape(2 * SSM_HEADS, 1), (2 * SSM_HEADS, CHUNK))
    y = _ssd(xbc, dt, z, ssm_conv_w, ssm_conv_b.reshape(1, -1), head_rows(dt_bias), head_rows(a_log),
             jnp.repeat(d_skip, SSM_HEAD_DIM).reshape(1, -1), g_ssm.reshape(1, -1))

    return _mixffn(attn, y, x, w_out.astype(BF16), w_up.astype(BF16), ffn_conv_w, ffn_conv_b, w_down.astype(BF16),
                   mod3, g_ffn)


def kernel(x, c, ctx, c_ctx, w_mod, b_mod, g_mix, w_in, g_q, g_k, sink, ssm_conv_w, ssm_conv_b,
           a_log, dt_bias, d_skip, g_ssm, w_out, g_ffn, w_up, ffn_conv_w, ffn_conv_b, w_down):
    assert w_mod.shape[0] == 1, "single-layer block"
    first = lambda t: t.reshape(t.shape[1:])
    return _layer(x, c, ctx, c_ctx, first(w_mod), first(b_mod), first(g_mix), first(w_in), first(g_q),
                  first(g_k), first(sink), first(ssm_conv_w), first(ssm_conv_b), first(a_log),
                  first(dt_bias), first(d_skip), first(g_ssm), first(w_out), first(g_ffn), first(w_up),
                  first(ffn_conv_w), first(ffn_conv_b), first(w_down))
```

```python
import jax
import jax.numpy as jnp
import numpy as np
from jax import lax
from jax.experimental import pallas as pl
from jax.experimental.pallas import tpu as pltpu

F32 = jnp.float32
BF16 = jnp.bfloat16

D_MODEL = 1024
CTX = 256
GRID_W = 64
HEAD_DIM = 64
ATTN_HEADS = 8
KV_HEADS = 2
ATTN_WIDTH = ATTN_HEADS * HEAD_DIM
KV_WIDTH = KV_HEADS * HEAD_DIM
WINDOW = 128
BLOCK = 128
ROPE_BASE = 10000.0
SSM_WIDTH = D_MODEL - ATTN_WIDTH
SSM_HEAD_DIM = 64
SSM_HEADS = SSM_WIDTH // SSM_HEAD_DIM
SSM_GROUPS = 2
D_STATE = 128
XBC_WIDTH = SSM_WIDTH + 2 * SSM_GROUPS * D_STATE
CHUNK = 128
D_FF = 2816
IN_COLS = ATTN_WIDTH + 2 * KV_WIDTH + SSM_WIDTH + XBC_WIDTH + 2 * SSM_HEADS
EPS = 1e-6
NEG_INF = -1e30
LOG2E = 1.4426950408889634

LANES = 128
BF16_SUBLANES = 16
IN_COLS_PAD = 19 * LANES
COL_Q, COL_K, COL_V = 0, ATTN_WIDTH, ATTN_WIDTH + KV_WIDTH
COL_Z = ATTN_WIDTH + 2 * KV_WIDTH
COL_XBC = COL_Z + SSM_WIDTH
COL_DT = COL_XBC + XBC_WIDTH
VMEM_LIMIT = 56 * 1024 * 1024

TM_IN = 3 * CTX
TM_FFN = 512
TF_FFN = 1024
HALO = BF16_SUBLANES


def _params(*sem):
    return pltpu.CompilerParams(dimension_semantics=sem, vmem_limit_bytes=VMEM_LIMIT)


def _sigmoid(v):
    return 1.0 / (1.0 + jnp.exp(-v))


def _softplus(v):
    return jnp.maximum(v, 0.0) + jnp.log1p(jnp.exp(-jnp.abs(v)))


def _dot(a, b):
    return jnp.dot(a, b, preferred_element_type=F32)


def _dot_nt(a, b):
    return lax.dot_general(a, b, (((1,), (1,)), ((), ())), preferred_element_type=F32)


def _mod_kernel(c_ref, cctx_ref, w_ref, b_ref, o_ref):
    rows = o_ref.shape[0]
    b = c_ref.shape[0]
    cv = jnp.concatenate([c_ref[...], cctx_ref[...], jnp.zeros((rows - b - 1, D_MODEL), F32)], axis=0)
    s = (cv * _sigmoid(cv)).astype(BF16)
    res = _dot(s, w_ref[...].astype(BF16)) + b_ref[...]
    for r in range(b + 1):
        o_ref[r] = res[r:r + 1, :]
    o_ref[b + 1:rows] = jnp.zeros((rows - b - 1, 1, res.shape[1]), F32)


def _modulation(c, c_ctx, w_mod, b_mod):
    b = c.shape[0]
    rows = 8 * ((b + 1 + 7) // 8)
    tn = 1024
    return pl.pallas_call(
        _mod_kernel,
        grid=(6 * D_MODEL // tn,),
        in_specs=[pl.BlockSpec((b, D_MODEL), lambda j: (0, 0)),
                  pl.BlockSpec((1, D_MODEL), lambda j: (0, 0)),
                  pl.BlockSpec((D_MODEL, tn), lambda j: (0, j)),
                  pl.BlockSpec((1, tn), lambda j: (0, j))],
        out_specs=pl.BlockSpec((rows, 1, tn), lambda j: (0, 0, j)),
        out_shape=jax.ShapeDtypeStruct((rows, 1, 6 * D_MODEL), F32),
        compiler_params=_params("arbitrary"),
        name="mod",
    )(c, c_ctx.reshape(1, -1), w_mod, b_mod.reshape(1, -1))


def _inproj_kernel(xa_ref, xb_ref, xc_ref, ctx_ref, modb_ref, modc_ref, gmix_ref, w_ref, gq_ref, gk_ref,
                   cos_ref, sina_ref, sinb_ref, bd_ref,
                   q_ref, k_ref, v_ref, z_ref, xbc_ref, dt_ref):
    first = pl.program_id(1) == 0

    def normed(xin, mod):
        ms = jnp.mean(xin * xin, axis=-1, keepdims=True)
        hn = xin * lax.rsqrt(ms + EPS) * gmix_ref[...]
        return (hn * (1.0 + mod[:, D_MODEL:2 * D_MODEL]) + mod[:, 0:D_MODEL]).astype(BF16)

    modb = modb_ref[0]
    h = jnp.concatenate([normed(jnp.where(first, ctx_ref[0], xa_ref[0]), jnp.where(first, modc_ref[0], modb)),
                         normed(xb_ref[0], modb), normed(xc_ref[0], modb)], axis=0)

    cos = cos_ref[...]
    sina = sina_ref[...]
    sinb = sinb_ref[...]
    bd = bd_ref[...]

    def norm_rope(pf, gain):
        ss = pf * pf
        hi = ss.astype(BF16)
        lo = (ss - hi.astype(F32)).astype(BF16)
        sums = _dot(jnp.concatenate([hi, lo], axis=1), bd)
        y = pf * lax.rsqrt(sums * (1.0 / HEAD_DIM) + EPS) * gain
        return y * cos + pltpu.roll(y, LANES - 16, 1) * sina + pltpu.roll(y, 16, 1) * sinb

    def q_chunks(lo, hi):
        for cidx in range(lo, hi):
            cs = slice(cidx * LANES, (cidx + 1) * LANES)
            q_ref[0, :, cs] = norm_rope(qf[:, cs], gq_ref[...]).astype(BF16)

    qf = _dot(h, w_ref[:, COL_Q:COL_Q + ATTN_WIDTH])
    kvf = _dot(h, w_ref[:, COL_K:COL_K + 2 * KV_WIDTH])
    zf = _dot(h, w_ref[:, COL_Z:COL_Z + SSM_WIDTH])
    q_chunks(0, 2)
    xf = _dot(h, w_ref[:, COL_XBC:COL_XBC + XBC_WIDTH])
    q_chunks(2, ATTN_WIDTH // LANES)
    dtf = _dot(h, w_ref[:, COL_DT:COL_DT + LANES])
    kf = norm_rope(kvf[:, 0:KV_WIDTH], gk_ref[...])
    k_ref[0, :, 0:LANES] = kf.astype(BF16)
    k_ref[0, :, LANES:2 * LANES] = pltpu.roll(kf, HEAD_DIM, 1).astype(BF16)
    vf = kvf[:, KV_WIDTH:2 * KV_WIDTH]
    v_ref[0, :, 0:LANES] = vf.astype(BF16)
    v_ref[0, :, LANES:2 * LANES] = pltpu.roll(vf, HEAD_DIM, 1).astype(BF16)
    z_ref[0] = zf.astype(BF16)
    xbc_ref[0] = xf.astype(BF16)
    dt_ref[0] = dtf


def _rope_tables(seq):
    rows = seq // GRID_W
    pos_r = np.repeat(np.arange(rows), GRID_W).astype(np.float32)
    pos_c = np.tile(np.arange(GRID_W), rows).astype(np.float32)
    quarter = HEAD_DIM // 4
    freqs = (ROPE_BASE ** (-np.arange(quarter, dtype=np.float32) / quarter)).astype(np.float32)
    ang_r = pos_r[:, None] * freqs[None, :]
    ang_c = pos_c[:, None] * freqs[None, :]
    cr, sr, cc, sc = np.cos(ang_r), np.sin(ang_r), np.cos(ang_c), np.sin(ang_c)
    zero = np.zeros_like(sr)
    cos = np.concatenate([cr, cr, cc, cc], axis=-1)
    sina = np.concatenate([-sr, zero, -sc, zero], axis=-1)
    sinb = np.concatenate([zero, sr, zero, sc], axis=-1)
    reps = LANES // HEAD_DIM
    pad = lambda t, v: np.concatenate([np.full((CTX, LANES), v, np.float32), np.tile(t, (1, reps))], axis=0)
    return tuple(jnp.asarray(pad(t, v), F32) for t, v in ((cos, 1.0), (sina, 0.0), (sinb, 0.0)))


def _inproj(x, ctx, mod3, g_mix, w_in_p, g_q, g_k):
    b, seq = x.shape[0], x.shape[1]
    tot = CTX + seq
    cos, sina, sinb = _rope_tables(seq)
    reps = LANES // HEAD_DIM
    gq = (jnp.tile(g_q, reps) * (HEAD_DIM ** -0.5 * LOG2E)).reshape(1, LANES)
    gk = jnp.tile(g_k, reps).reshape(1, LANES)
    lane = np.arange(LANES)
    bd = lane[:, None] // HEAD_DIM == lane[None, :] // HEAD_DIM
    bd = jnp.asarray(np.concatenate([bd, bd], axis=0), BF16)
    nsteps = tot // TM_IN
    sub = TM_IN // CTX
    const = lambda *shape: pl.BlockSpec(shape, lambda bb, s: (0,) * len(shape))
    tab = pl.BlockSpec((TM_IN, LANES), lambda bb, s: (s, 0))
    out = lambda w: pl.BlockSpec((1, TM_IN, w), lambda bb, s: (bb, s, 0))
    shp = lambda w, dt: jax.ShapeDtypeStruct((b, tot, w), dt)
    xsub = lambda j: pl.BlockSpec((1, CTX, D_MODEL), lambda bb, s: (bb, jnp.maximum(sub * s + j - 1, 0), 0))
    return pl.pallas_call(
        _inproj_kernel,
        grid=(b, nsteps),
        in_specs=[xsub(0), xsub(1), xsub(2),
                  pl.BlockSpec((1, CTX, D_MODEL), lambda bb, s: (bb, 0, 0)),
                  pl.BlockSpec((1, 1, 6 * D_MODEL), lambda bb, s: (bb, 0, 0)),
                  pl.BlockSpec((1, 1, 6 * D_MODEL), lambda bb, s: (b, 0, 0)),
                  const(1, D_MODEL), const(D_MODEL, IN_COLS_PAD), const(1, LANES), const(1, LANES),
                  tab, tab, tab, const(2 * LANES, LANES)],
        out_specs=[out(ATTN_WIDTH), out(2 * LANES), out(2 * LANES), out(SSM_WIDTH), out(XBC_WIDTH), out(LANES)],
        out_shape=[shp(ATTN_WIDTH, BF16), shp(2 * LANES, BF16), shp(2 * LANES, BF16),
                   shp(SSM_WIDTH, BF16), shp(XBC_WIDTH, BF16), shp(LANES, F32)],
        compiler_params=_params("arbitrary", "arbitrary"),
        name="inproj",
    )(x, x, x, ctx, mod3, mod3, g_mix.reshape(1, -1), w_in_p, gq, gk, cos, sina, sinb, bd)


def _place_kv_head(t, kh, par):
    src = 0 if par == kh else 1
    lane = lax.broadcasted_iota(jnp.int32, (t.shape[0], LANES), 1)
    half = (lane >= HEAD_DIM) if par else (lane < HEAD_DIM)
    tile = t[:, src * LANES:(src + 1) * LANES]
    return jnp.where(half, tile, jnp.zeros_like(tile))


def _attn_kernel(sink_ref, q_ref, k_ref, v_ref, o_ref, kx_scr, vx_scr):
    seq = o_ref.shape[1]
    nblk = seq // BLOCK
    qi = lax.broadcasted_iota(jnp.int32, (BLOCK, BLOCK), 0)
    kj = lax.broadcasted_iota(jnp.int32, (BLOCK, BLOCK), 1)
    zero = jnp.zeros((BLOCK, BLOCK), F32)
    tri_prev = jnp.where(kj >= qi, zero, NEG_INF)
    tri_next = jnp.where(kj <= qi, zero, NEG_INF)
    combos = [(kh, par) for kh in range(KV_HEADS) for par in range(2)]
    for idx, (kh, par) in enumerate(combos):
        kx_scr[idx] = _place_kv_head(k_ref[0, 0:CTX, :], kh, par)
        vx_scr[idx] = _place_kv_head(v_ref[0, 0:CTX, :], kh, par)

    def block(nb, carry):
        r0 = pl.multiple_of(CTX + nb * BLOCK, BLOCK)
        rn = pl.multiple_of(CTX + jnp.minimum(nb + 1, nblk - 1) * BLOCK, BLOCK)
        rp = pl.multiple_of(CTX - BLOCK + nb * BLOCK, BLOCK)
        kloc = jnp.concatenate([k_ref[0, pl.ds(rp, 2 * BLOCK), :], k_ref[0, pl.ds(rn, BLOCK), :]], axis=0)
        vloc = jnp.concatenate([v_ref[0, pl.ds(rp, 2 * BLOCK), :], v_ref[0, pl.ds(rn, BLOCK), :]], axis=0)
        bias_p = jnp.where(nb > 0, tri_prev, NEG_INF)
        bias_n = jnp.where(nb < nblk - 1, tri_next, NEG_INF)
        q = q_ref[0, pl.ds(r0, BLOCK), :]
        q2s = [jnp.concatenate([q[:, (2 * kh) * LANES:(2 * kh + 1) * LANES],
                                q[:, (2 * kh + 1) * LANES:(2 * kh + 2) * LANES]], axis=0)
               for kh in range(KV_HEADS)]
        sc, ee, oo = {}, {}, {}

        def scores(idx):
            kh, par = combos[idx]
            sc[idx] = (_dot_nt(q2s[kh], _place_kv_head(kloc, kh, par)),
                       _dot_nt(q2s[kh], kx_scr[idx]))

        def softmax(idx):
            kh, par = combos[idx]
            s_loc, s_ctx = sc[idx]
            es, inv = [], []
            for r in range(2):
                head = 4 * kh + par + 2 * r
                rs = slice(r * BLOCK, (r + 1) * BLOCK)
                sv = jnp.concatenate([s_loc[rs, 0:BLOCK] + bias_p, s_loc[rs, BLOCK:2 * BLOCK],
                                      s_loc[rs, 2 * BLOCK:3 * BLOCK] + bias_n, s_ctx[rs]], axis=1)
                sk = sink_ref[head] * LOG2E
                m = jnp.maximum(jnp.max(sv, axis=-1, keepdims=True), sk)
                e = jnp.exp2(sv - m)
                inv.append(1.0 / (jnp.sum(e, axis=-1, keepdims=True) + jnp.exp2(sk - m)))
                es.append(e.astype(BF16))
            ee[idx] = (jnp.concatenate(es, axis=0), inv)

        def pv(idx):
            kh, par = combos[idx]
            e16, inv = ee[idx]
            o2 = _dot(e16[:, 0:3 * BLOCK], _place_kv_head(vloc, kh, par)) + _dot(e16[:, 3 * BLOCK:], vx_scr[idx])
            oo[idx] = [o2[r * BLOCK:(r + 1) * BLOCK] * inv[r] for r in range(2)]

        scores(0); scores(1); softmax(0); scores(2); softmax(1); pv(0)
        scores(3); softmax(2); pv(1); softmax(3); pv(2); pv(3)
        for kh in range(KV_HEADS):
            for r in range(2):
                c = 2 * kh + r
                val = oo[combos.index((kh, 0))][r] + oo[combos.index((kh, 1))][r]
                o_ref[0, pl.ds(pl.multiple_of(nb * BLOCK, BLOCK), BLOCK), c * LANES:(c + 1) * LANES] = val.astype(BF16)
        return carry

    lax.fori_loop(0, nblk, block, 0)


def _attention(q, kk, vv, sink):
    b, tot = q.shape[0], q.shape[1]
    seq = tot - CTX
    whole = lambda w: pl.BlockSpec((1, tot, w), lambda bb: (bb, 0, 0))
    return pl.pallas_call(
        _attn_kernel,
        grid=(b,),
        in_specs=[pl.BlockSpec(memory_space=pltpu.SMEM), whole(ATTN_WIDTH), whole(2 * LANES), whole(2 * LANES)],
        out_specs=pl.BlockSpec((1, seq, ATTN_WIDTH), lambda bb: (bb, 0, 0)),
        out_shape=jax.ShapeDtypeStruct((b, seq, ATTN_WIDTH), BF16),
        scratch_shapes=[pltpu.VMEM((2 * KV_HEADS, CTX, LANES), BF16), pltpu.VMEM((2 * KV_HEADS, CTX, LANES), BF16)],
        compiler_params=_params("arbitrary"),
        name="attn",
    )(sink, q, kk, vv)


def _ssd_selectors():
    nh2 = 2 * SSM_HEADS
    npairs = SSM_HEADS // 2
    k = np.arange(LANES)
    part, q = k // (2 * nh2), k % (2 * nh2)
    live = part < 3
    col = np.arange(nh2 * LANES)
    tile, lane = col // LANES, col % LANES
    kind, blk = tile // (nh2 // 2), tile % (nh2 // 2)
    src = kind * nh2 + (blk // npairs) * SSM_HEADS + 2 * (blk % npairs) + (lane >= SSM_HEAD_DIM)
    pairsel = live[:, None] & (q[:, None] == src[None, :])
    segind = live[:, None] & (q[:, None] < nh2) & (q[:, None] == tile[None, :])
    return jnp.asarray(pairsel, BF16), jnp.asarray(segind, BF16)


def _ssd_kernel(x_ref, dt_ref, z_ref, cw_ref, cb_ref, dtb_ref, alog_ref, dsk_ref, g_ref, tri_ref, shift_ref,
                pairsel_ref, segind_ref,
                y_ref, hf_scr, hb_scr, yacc_scr, stb_scr, decb_scr, eb_scr, cbuf_scr, segrhs_scr):
    tot = x_ref.shape[1]
    nchunks = tot // CHUNK
    nctx = CTX // CHUNK
    npairs = SSM_HEADS // 2
    pairs_per_group = npairs // SSM_GROUPS
    group_cols = pairs_per_group * LANES

    hf_scr[...] = jnp.zeros_like(hf_scr)
    hb_scr[...] = jnp.zeros_like(hb_scr)
    segrhs_scr[...] = segind_ref[...]

    def chunk_pass(chunk, carry):
        lc = jnp.maximum(chunk - nctx, 0)
        r0 = pl.multiple_of(lc * CHUNK, CHUNK)
        c0 = pl.multiple_of(chunk * CHUNK, CHUNK)
        p0 = pl.multiple_of(jnp.maximum(chunk * CHUNK - HALO, 0), HALO)
        n0 = pl.multiple_of(jnp.minimum(chunk * CHUNK + CHUNK, tot - HALO), HALO)

        nh2 = 2 * SSM_HEADS
        dtv = _softplus(dt_ref[0, pl.ds(c0, CHUNK), :].T[0:nh2, :] + dtb_ref[...])
        a = dtv * (-jnp.exp(alog_ref[...]) * LOG2E)

        def split3(v):
            hi = v.astype(BF16)
            r1 = v - hi.astype(F32)
            mid = r1.astype(BF16)
            return hi, mid, (r1 - mid.astype(F32)).astype(BF16)

        parts = jnp.concatenate(split3(a), axis=0)
        fold = lambda p: p[0:nh2] + p[nh2:2 * nh2] + p[2 * nh2:3 * nh2]
        prefix = fold(_dot(parts, tri_ref[0]))
        suffix = fold(_dot(parts, tri_ref[1]))
        head_row = lax.broadcasted_iota(jnp.int32, (nh2, CHUNK), 0)
        ac_t = jnp.where(head_row < SSM_HEADS, prefix, suffix)

        v32 = [p.astype(F32) for p in split3(jnp.concatenate([ac_t, dtv], axis=0))]
        ones3 = (lax.broadcasted_iota(jnp.int32, (CHUNK - 6 * nh2, CHUNK), 0) < 3).astype(F32)
        cols3 = jnp.concatenate(v32 + [ones3], axis=0).T.astype(BF16)
        for q in range(nh2):
            neg = [-p[q:q + 1, :] for p in v32]
            tile = jnp.concatenate(neg + [jnp.zeros((BF16_SUBLANES - 3, CHUNK), F32)], axis=0)
            segrhs_scr[6 * nh2:6 * nh2 + BF16_SUBLANES, q * CHUNK:(q + 1) * CHUNK] = tile.astype(BF16)

        x16 = x_ref[0, pl.ds(c0, CHUNK), :]
        has_prev = (chunk != 0) & (chunk != nctx)
        has_next = (chunk != nctx - 1) & (chunk != nchunks - 1)
        halo_p = x_ref[0, pl.ds(p0, HALO), :]
        halo_n = x_ref[0, pl.ds(n0, HALO), :]
        xext = jnp.concatenate([jnp.where(has_prev, halo_p, jnp.zeros_like(halo_p)), x16,
                                jnp.where(has_next, halo_n, jnp.zeros_like(halo_n))], axis=0)
        shifted = _dot(shift_ref[...], xext)
        pairs = _dot(cols3, pairsel_ref[...])
        seg_all = _dot(cols3, segrhs_scr[...])

        u = (cw_ref[0:1, :] * shifted[0:CHUNK] + cw_ref[1:2, :] * x16.astype(F32)
             + cw_ref[2:3, :] * shifted[CHUNK:2 * CHUNK] + cb_ref[...])
        u = u * _sigmoid(u)
        xs = u[:, 0:SSM_WIDTH]
        bmat = u[:, SSM_WIDTH:SSM_WIDTH + SSM_GROUPS * D_STATE]
        cmat = u[:, SSM_WIDTH + SSM_GROUPS * D_STATE:XBC_WIDTH]
        cb16 = [cmat[:, g * D_STATE:(g + 1) * D_STATE].astype(BF16) for g in range(SSM_GROUPS)]
        bb16 = [bmat[:, g * D_STATE:(g + 1) * D_STATE].astype(BF16) for g in range(SSM_GROUPS)]
        bt16 = [bmat[:, g * D_STATE:(g + 1) * D_STATE].T.astype(BF16) for g in range(SSM_GROUPS)]
        cbuf_scr[lc] = jnp.concatenate(cb16, axis=1)

        hf_start = [hf_scr[:, g * group_cols:(g + 1) * group_cols] for g in range(SSM_GROUPS)]
        gmat = [_dot_nt(cb16[g], bb16[g]) for g in range(SSM_GROUPS)]
        yoff = [_dot(cb16[g], hf_start[g].astype(BF16)) for g in range(SSM_GROUPS)]

        half0 = lax.broadcasted_iota(jnp.int32, (CHUNK, LANES), 1) < SSM_HEAD_DIM
        ii = lax.broadcasted_iota(jnp.int32, (CHUNK, CHUNK), 0)
        jj = lax.broadcasted_iota(jnp.int32, (CHUNK, CHUNK), 1)
        causal = [jj <= ii, jj >= ii]
        last = [CHUNK - 1, 0]

        arg_pair, tot_pair, xdt_pair, xdtd_pair = [], [], [], []
        for d in range(2):
            args, tots, xdts, xdtds = [], [], [], []
            for m in range(npairs):
                blk = d * npairs + m
                arg = pairs[:, blk * LANES:(blk + 1) * LANES]
                dtp = pairs[:, (nh2 // 2 + blk) * LANES:(nh2 // 2 + blk + 1) * LANES]
                total = arg[last[d]:last[d] + 1, :]
                xdt = xs[:, m * LANES:(m + 1) * LANES] * dtp
                args.append(arg)
                tots.append(total)
                xdts.append(xdt)
                xdtds.append(xdt * jnp.exp2(total - arg))
            arg_pair.append(args)
            tot_pair.append(tots)
            xdt_pair.append(xdts)
            xdtd_pair.append(xdtds)

        for g in range(SSM_GROUPS):
            pr = range(g * pairs_per_group, (g + 1) * pairs_per_group)
            gs = slice(g * group_cols, (g + 1) * group_cols)
            st_f = _dot(bt16[g], jnp.concatenate([xdtd_pair[0][m] for m in pr], axis=1).astype(BF16))
            st_b = _dot(bt16[g], jnp.concatenate([xdtd_pair[1][m] for m in pr], axis=1).astype(BF16))
            dec_f = jnp.concatenate([jnp.exp2(tot_pair[0][m]) for m in pr], axis=1)
            dec_b = jnp.concatenate([jnp.exp2(tot_pair[1][m]) for m in pr], axis=1)
            hf_scr[:, gs] = hf_start[g] * dec_f + st_f
            stb_scr[chunk, :, gs] = st_b
            decb_scr[chunk, :, gs] = jnp.broadcast_to(dec_b, (8, group_cols))

        operands = {}

        def intra_operands(m):
            g = m // pairs_per_group
            smats, rhs = [], []
            for d in range(2):
                for hh in range(2):
                    r = d * SSM_HEADS + 2 * m + hh
                    seg = seg_all[:, r * CHUNK:(r + 1) * CHUNK]
                    smats.append(jnp.where(causal[d], gmat[g] * jnp.exp2(seg), 0.0).astype(BF16))
                xd = xdt_pair[d][m]
                rhs += [jnp.where(half0, xd, 0.0), jnp.where(half0, 0.0, xd)]
            operands[m] = (jnp.concatenate(smats, axis=1), jnp.concatenate(rhs, axis=0).astype(BF16))

        def intra_matmul(m):
            g, mm = m // pairs_per_group, m % pairs_per_group
            ms = slice(m * LANES, (m + 1) * LANES)
            ydiag = _dot(*operands[m])
            yacc_scr[pl.ds(r0, CHUNK), ms] = (ydiag + yoff[g][:, mm * LANES:(mm + 1) * LANES]
                                              * jnp.exp2(arg_pair[0][m]) + dsk_ref[:, ms] * xs[:, ms])
            eb_scr[lc, :, ms] = jnp.exp2(arg_pair[1][m])

        intra_operands(0)
        for m in range(1, npairs):
            intra_operands(m)
            intra_matmul(m - 1)
        intra_matmul(npairs - 1)
        return carry

    lax.fori_loop(0, nchunks, chunk_pass, 0)

    def carry_pass(i, carry):
        chunk = jnp.where(i < nctx, nctx - 1 - i, nchunks + nctx - 1 - i)
        hb = hb_scr[...]

        @pl.when(chunk >= nctx)
        def _():
            lc = chunk - nctx
            r0 = pl.multiple_of(lc * CHUNK, CHUNK)
            cb = cbuf_scr[lc]
            yoff = jnp.concatenate(
                [_dot(cb[:, g * D_STATE:(g + 1) * D_STATE], hb[:, g * group_cols:(g + 1) * group_cols].astype(BF16))
                 for g in range(SSM_GROUPS)], axis=1)
            zf = z_ref[0, pl.ds(pl.multiple_of(chunk * CHUNK, CHUNK), CHUNK), :].astype(F32)
            yt = (yacc_scr[pl.ds(r0, CHUNK), :] + yoff * eb_scr[lc]) * (zf * _sigmoid(zf))
            gw = SSM_WIDTH // SSM_GROUPS
            for g in range(SSM_GROUPS):
                yg = yt[:, g * gw:(g + 1) * gw]
                ms = jnp.mean(yg * yg, axis=-1, keepdims=True)
                y_ref[0, pl.ds(r0, CHUNK), g * gw:(g + 1) * gw] = (yg * lax.rsqrt(ms + EPS)
                                                                   * g_ref[:, g * gw:(g + 1) * gw]).astype(BF16)

        hb_scr[...] = hb * decb_scr[chunk, 0:1, :] + stb_scr[chunk]
        return carry

    lax.fori_loop(0, nchunks, carry_pass, 0)


def _ssd(xbc, dt, z, conv_w, conv_b, dtb16, alog16, dsk, g_ssm):
    b, tot = xbc.shape[0], xbc.shape[1]
    seq = tot - CTX
    nchunks = tot // CHUNK
    nctx = CTX // CHUNK
    nlat = nchunks - nctx
    const = lambda *shape: pl.BlockSpec(shape, lambda bb: (0,) * len(shape))
    whole = lambda rows, w: pl.BlockSpec((1, rows, w), lambda bb: (bb, 0, 0))
    jdx = np.arange(CHUNK)
    tri = jnp.asarray(np.stack([jdx[:, None] <= jdx[None, :], jdx[:, None] >= jdx[None, :]]), BF16)
    kdx = np.arange(CHUNK + 2 * HALO)
    shift = jnp.asarray(np.concatenate([kdx[None, :] == jdx[:, None] + HALO - 1,
                                        kdx[None, :] == jdx[:, None] + HALO + 1], axis=0), BF16)
    return pl.pallas_call(
        _ssd_kernel,
        grid=(b,),
        in_specs=[whole(tot, XBC_WIDTH), whole(tot, LANES), whole(tot, SSM_WIDTH),
                  const(3, XBC_WIDTH), const(1, XBC_WIDTH), const(2 * SSM_HEADS, CHUNK), const(2 * SSM_HEADS, CHUNK),
                  const(1, SSM_WIDTH), const(1, SSM_WIDTH), const(2, CHUNK, CHUNK),
                  const(2 * CHUNK, CHUNK + 2 * HALO),
                  const(LANES, 2 * SSM_HEADS * LANES), const(LANES, 2 * SSM_HEADS * LANES)],
        out_specs=whole(seq, SSM_WIDTH),
        out_shape=jax.ShapeDtypeStruct((b, seq, SSM_WIDTH), BF16),
        scratch_shapes=[pltpu.VMEM((D_STATE, SSM_WIDTH), F32),
                        pltpu.VMEM((D_STATE, SSM_WIDTH), F32),
                        pltpu.VMEM((seq, SSM_WIDTH), F32),
                        pltpu.VMEM((nchunks, D_STATE, SSM_WIDTH), F32),
                        pltpu.VMEM((nchunks, 8, SSM_WIDTH), F32),
                        pltpu.VMEM((nlat, CHUNK, SSM_WIDTH), F32),
                        pltpu.VMEM((nlat, CHUNK, SSM_GROUPS * D_STATE), BF16),
                        pltpu.VMEM((LANES, 2 * SSM_HEADS * LANES), BF16)],
        compiler_params=_params("arbitrary"),
        name="ssd",
    )(xbc, dt, z, conv_w, conv_b, dtb16, alog16, dsk, g_ssm, tri, shift, *_ssd_selectors())


def _mixffn_kernel(am_ref, ap_ref, an_ref, ym_ref, yp_ref, yn_ref, xm_ref, xp_ref, xn_ref,
                   wo_ref, wup_ref, cw_ref, cb_ref, wd_ref, mod_ref, g_ref, o_ref):
    i = pl.program_id(1)
    tm = am_ref.shape[1]
    nrows = tm + 2 * HALO
    ext = lambda p, m, n: jnp.concatenate([p[0], m[0], n[0]], axis=0)
    o = (_dot(ext(ap_ref, am_ref, an_ref), wo_ref[0:ATTN_WIDTH, :])
         + _dot(ext(yp_ref, ym_ref, yn_ref), wo_ref[ATTN_WIDTH:D_MODEL, :]))
    mod = mod_ref[0]
    xnew = ext(xp_ref, xm_ref, xn_ref) + mod[:, 2 * D_MODEL:3 * D_MODEL] * o
    ms = jnp.mean(xnew * xnew, axis=-1, keepdims=True)
    hn = xnew * lax.rsqrt(ms + EPS) * g_ref[...]
    h2 = hn * (1.0 + mod[:, 4 * D_MODEL:5 * D_MODEL]) + mod[:, 3 * D_MODEL:4 * D_MODEL]
    row = lax.broadcasted_iota(jnp.int32, (nrows, 1), 0)
    lo = jnp.where(i > 0, 0, HALO)
    hi = jnp.where(i < pl.num_programs(1) - 1, nrows, nrows - HALO)
    lhs = jnp.where(row >= lo, jnp.where(row < hi, h2, 0.0), 0.0).astype(BF16)

    chunks = [(c0, min(c0 + TF_FFN, D_FF)) for c0 in range(0, D_FF, TF_FFN)]
    ups, parts = {}, []

    def up_proj(k):
        c0, c1 = chunks[k]
        ups[k] = (_dot(lhs, wup_ref[:, c0:c1]), _dot(lhs, wup_ref[:, D_FF + c0:D_FF + c1]))

    def conv(uu, c0, c1):
        up = pltpu.roll(uu, 1, 0)[HALO:HALO + tm]
        dn = pltpu.roll(uu, nrows - 1, 0)[HALO:HALO + tm]
        return (cw_ref[0:1, c0:c1] * up + cw_ref[1:2, c0:c1] * uu[HALO:HALO + tm]
                + cw_ref[2:3, c0:c1] * dn + cb_ref[:, c0:c1])

    def gate_down(k):
        c0, c1 = chunks[k]
        ua = conv(ups[k][0], c0, c1)
        ug = conv(ups[k][1], D_FF + c0, D_FF + c1)
        act = (ua * (ug * _sigmoid(ug))).astype(BF16)
        parts.append(_dot(act, wd_ref[c0:c1, :]))

    up_proj(0)
    for k in range(1, len(chunks)):
        up_proj(k)
        gate_down(k - 1)
    gate_down(len(chunks) - 1)
    acc = parts[0]
    for part in parts[1:]:
        acc = acc + part
    o_ref[0] = xnew[HALO:HALO + tm] + mod[:, 5 * D_MODEL:6 * D_MODEL] * acc


def _mixffn(attn, y, x, w_out16, w_up16, conv_w, conv_b, w_down16, mod3, g_ffn):
    b, seq = x.shape[0], x.shape[1]
    ni = seq // TM_FFN
    hpt = TM_FFN // HALO
    nhalo = seq // HALO
    main = lambda w: pl.BlockSpec((1, TM_FFN, w), lambda bb, i: (bb, i, 0))
    prev = lambda w: pl.BlockSpec((1, HALO, w), lambda bb, i: (bb, jnp.maximum(i * hpt - 1, 0), 0))
    nxt = lambda w: pl.BlockSpec((1, HALO, w), lambda bb, i: (bb, jnp.minimum(i * hpt + hpt, nhalo - 1), 0))
    resident = lambda *shape: pl.BlockSpec(shape, lambda bb, i: (0,) * len(shape), pipeline_mode=pl.Buffered(1))
    trio = lambda w: [main(w), prev(w), nxt(w)]
    return pl.pallas_call(
        _mixffn_kernel,
        grid=(b, ni),
        in_specs=trio(ATTN_WIDTH) + trio(SSM_WIDTH) + trio(D_MODEL) + [
            resident(D_MODEL, D_MODEL), resident(D_MODEL, 2 * D_FF), resident(3, 2 * D_FF), resident(1, 2 * D_FF),
            resident(D_FF, D_MODEL),
            pl.BlockSpec((1, 1, 6 * D_MODEL), lambda bb, i: (bb, 0, 0)),
            resident(1, D_MODEL)],
        out_specs=main(D_MODEL),
        out_shape=jax.ShapeDtypeStruct((b, seq, D_MODEL), F32),
        compiler_params=_params("arbitrary", "arbitrary"),
        name="mixffn",
    )(attn, attn, attn, y, y, y, x, x, x, w_out16, w_up16, conv_w, conv_b.reshape(1, -1), w_down16, mod3,
      g_ffn.reshape(1, -1))


def _layer(x, c, ctx, c_ctx, w_mod, b_mod, g_mix, w_in, g_q, g_k, sink, ssm_conv_w, ssm_conv_b,
           a_log, dt_bias, d_skip, g_ssm, w_out, g_ffn, w_up, ffn_conv_w, ffn_conv_b, w_down):
    mod3 = _modulation(c, c_ctx, w_mod, b_mod)

    w_in_p = jnp.pad(w_in, ((0, 0), (0, IN_COLS_PAD - IN_COLS))).astype(BF16)
    q, kk, vv, z, xbc, dt = _inproj(x, ctx, mod3, g_mix, w_in_p, g_q, g_k)
    attn = _attention(q, kk, vv, sink)

    head_rows = lambda t: jnp.broadcast_to(t.reshape(2 * SSM_HEADS, 1), (2 * SSM_HEADS, CHUNK))
    y = _ssd(xbc, dt, z, ssm_conv_w, ssm_conv_b.reshape(1, -1), head_rows(dt_bias), head_rows(a_log),
             jnp.repeat(d_skip, SSM_HEAD_DIM).reshape(1, -1), g_ssm.reshape(1, -1))

    return _mixffn(attn, y, x, w_out.astype(BF16), w_up.astype(BF16), ffn_conv_w, ffn_conv_b, w_down.astype(BF16),
                   mod3, g_ffn)


def kernel(x, c, ctx, c_ctx, w_mod, b_mod, g_mix, w_in, g_q, g_k, sink, ssm_conv_w, ssm_conv_b,
           a_log, dt_bias, d_skip, g_ssm, w_out, g_ffn, w_up, ffn_conv_w, ffn_conv_b, w_down):
    assert w_mod.shape[0] == 1, "single-layer block"
    first = lambda t: t.reshape(t.shape[1:])
    return _layer(x, c, ctx, c_ctx, first(w_mod), first(b_mod), first(g_mix), first(w_in), first(g_q),
                  first(g_k), first(sink), first(ssm_conv_w), first(ssm_conv_b), first(a_log),
                  first(dt_bias), first(d_skip), first(g_ssm), first(w_out), first(g_ffn), first(w_up),
                  first(ffn_conv_w), first(ffn_conv_b), first(w_down))
```

```python
import jax
import jax.numpy as jnp
import numpy as np
from jax import lax
from jax.experimental import pallas as pl
from jax.experimental.pallas import tpu as pltpu

F32 = jnp.float32
BF16 = jnp.bfloat16

D_MODEL = 1024
CTX = 256
GRID_W = 64
HEAD_DIM = 64
ATTN_HEADS = 8
KV_HEADS = 2
ATTN_WIDTH = ATTN_HEADS * HEAD_DIM
KV_WIDTH = KV_HEADS * HEAD_DIM
WINDOW = 128
BLOCK = 128
ROPE_BASE = 10000.0
SSM_WIDTH = D_MODEL - ATTN_WIDTH
SSM_HEAD_DIM = 64
SSM_HEADS = SSM_WIDTH // SSM_HEAD_DIM
SSM_GROUPS = 2
D_STATE = 128
XBC_WIDTH = SSM_WIDTH + 2 * SSM_GROUPS * D_STATE
CHUNK = 128
D_FF = 2816
IN_COLS = ATTN_WIDTH + 2 * KV_WIDTH + SSM_WIDTH + XBC_WIDTH + 2 * SSM_HEADS
EPS = 1e-6
NEG_INF = -1e30
LOG2E = 1.4426950408889634

LANES = 128
BF16_SUBLANES = 16
IN_COLS_PAD = 19 * LANES
COL_Q, COL_K, COL_V = 0, ATTN_WIDTH, ATTN_WIDTH + KV_WIDTH
COL_Z = ATTN_WIDTH + 2 * KV_WIDTH
COL_XBC = COL_Z + SSM_WIDTH
COL_DT = COL_XBC + XBC_WIDTH
VMEM_LIMIT = 56 * 1024 * 1024

TM_IN = 3 * CTX
TM_FFN = 512
TF_FFN = 1024
HALO = BF16_SUBLANES


def _params(*sem):
    return pltpu.CompilerParams(dimension_semantics=sem, vmem_limit_bytes=VMEM_LIMIT)


def _sigmoid(v):
    return 1.0 / (1.0 + jnp.exp(-v))


def _softplus(v):
    return jnp.maximum(v, 0.0) + jnp.log1p(jnp.exp(-jnp.abs(v)))


def _dot(a, b):
    return jnp.dot(a, b, preferred_element_type=F32)


def _dot_nt(a, b):
    return lax.dot_general(a, b, (((1,), (1,)), ((), ())), preferred_element_type=F32)


def _mod_kernel(c_ref, cctx_ref, w_ref, b_ref, o_ref):
    rows = o_ref.shape[0]
    b = c_ref.shape[0]
    cv = jnp.concatenate([c_ref[...], cctx_ref[...], jnp.zeros((rows - b - 1, D_MODEL), F32)], axis=0)
    s = (cv * _sigmoid(cv)).astype(BF16)
    res = _dot(s, w_ref[...].astype(BF16)) + b_ref[...]
    for r in range(b + 1):
        o_ref[r] = res[r:r + 1, :]
    o_ref[b + 1:rows] = jnp.zeros((rows - b - 1, 1, res.shape[1]), F32)


def _modulation(c, c_ctx, w_mod, b_mod):
    b = c.shape[0]
    rows = 8 * ((b + 1 + 7) // 8)
    tn = 1024
    return pl.pallas_call(
        _mod_kernel,
        grid=(6 * D_MODEL // tn,),
        in_specs=[pl.BlockSpec((b, D_MODEL), lambda j: (0, 0)),
                  pl.BlockSpec((1, D_MODEL), lambda j: (0, 0)),
                  pl.BlockSpec((D_MODEL, tn), lambda j: (0, j)),
                  pl.BlockSpec((1, tn), lambda j: (0, j))],
        out_specs=pl.BlockSpec((rows, 1, tn), lambda j: (0, 0, j)),
        out_shape=jax.ShapeDtypeStruct((rows, 1, 6 * D_MODEL), F32),
        compiler_params=_params("arbitrary"),
        name="mod",
    )(c, c_ctx.reshape(1, -1), w_mod, b_mod.reshape(1, -1))


def _inproj_kernel(xa_ref, xb_ref, xc_ref, ctx_ref, modb_ref, modc_ref, gmix_ref, w_ref, gq_ref, gk_ref,
                   cos_ref, sina_ref, sinb_ref, bd_ref,
                   q_ref, k_ref, v_ref, z_ref, xbc_ref, dt_ref):
    first = pl.program_id(1) == 0

    def normed(xin, mod):
        ms = jnp.mean(xin * xin, axis=-1, keepdims=True)
        hn = xin * lax.rsqrt(ms + EPS) * gmix_ref[...]
        return (hn * (1.0 + mod[:, D_MODEL:2 * D_MODEL]) + mod[:, 0:D_MODEL]).astype(BF16)

    modb = modb_ref[0]
    h = jnp.concatenate([normed(jnp.where(first, ctx_ref[0], xa_ref[0]), jnp.where(first, modc_ref[0], modb)),
                         normed(xb_ref[0], modb), normed(xc_ref[0], modb)], axis=0)

    cos = cos_ref[...]
    sina = sina_ref[...]
    sinb = sinb_ref[...]
    bd = bd_ref[...]

    def norm_rope(pf, gain):
        ss = pf * pf
        hi = ss.astype(BF16)
        lo = (ss - hi.astype(F32)).astype(BF16)
        sums = _dot(jnp.concatenate([hi, lo], axis=1), bd)
        y = pf * lax.rsqrt(sums * (1.0 / HEAD_DIM) + EPS) * gain
        return y * cos + pltpu.roll(y, LANES - 16, 1) * sina + pltpu.roll(y, 16, 1) * sinb

    def q_chunks(lo, hi):
        for cidx in range(lo, hi):
            cs = slice(cidx * LANES, (cidx + 1) * LANES)
            q_ref[0, :, cs] = norm_rope(qf[:, cs], gq_ref[...]).astype(BF16)

    qf = _dot(h, w_ref[:, COL_Q:COL_Q + ATTN_WIDTH])
    kvf = _dot(h, w_ref[:, COL_K:COL_K + 2 * KV_WIDTH])
    zf = _dot(h, w_ref[:, COL_Z:COL_Z + SSM_WIDTH])
    q_chunks(0, 2)
    xf = _dot(h, w_ref[:, COL_XBC:COL_XBC + XBC_WIDTH])
    q_chunks(2, ATTN_WIDTH // LANES)
    dtf = _dot(h, w_ref[:, COL_DT:COL_DT + LANES])
    kf = norm_rope(kvf[:, 0:KV_WIDTH], gk_ref[...])
    k_ref[0, :, 0:LANES] = kf.astype(BF16)
    k_ref[0, :, LANES:2 * LANES] = pltpu.roll(kf, HEAD_DIM, 1).astype(BF16)
    vf = kvf[:, KV_WIDTH:2 * KV_WIDTH]
    v_ref[0, :, 0:LANES] = vf.astype(BF16)
    v_ref[0, :, LANES:2 * LANES] = pltpu.roll(vf, HEAD_DIM, 1).astype(BF16)
    z_ref[0] = zf.astype(BF16)
    xbc_ref[0] = xf.astype(BF16)
    dt_ref[0] = dtf


def _rope_tables(seq):
    rows = seq // GRID_W
    pos_r = np.repeat(np.arange(rows), GRID_W).astype(np.float32)
    pos_c = np.tile(np.arange(GRID_W), rows).astype(np.float32)
    quarter = HEAD_DIM // 4
    freqs = (ROPE_BASE ** (-np.arange(quarter, dtype=np.float32) / quarter)).astype(np.float32)
    ang_r = pos_r[:, None] * freqs[None, :]
    ang_c = pos_c[:, None] * freqs[None, :]
    cr, sr, cc, sc = np.cos(ang_r), np.sin(ang_r), np.cos(ang_c), np.sin(ang_c)
    zero = np.zeros_like(sr)
    cos = np.concatenate([cr, cr, cc, cc], axis=-1)
    sina = np.concatenate([-sr, zero, -sc, zero], axis=-1)
    sinb = np.concatenate([zero, sr, zero, sc], axis=-1)
    reps = LANES // HEAD_DIM
    pad = lambda t, v: np.concatenate([np.full((CTX, LANES), v, np.float32), np.tile(t, (1, reps))], axis=0)
    return tuple(jnp.asarray(pad(t, v), F32) for t, v in ((cos, 1.0), (sina, 0.0), (sinb, 0.0)))


def _inproj(x, ctx, mod3, g_mix, w_in_p, g_q, g_k):
    b, seq = x.shape[0], x.shape[1]
    tot = CTX + seq
    cos, sina, sinb = _rope_tables(seq)
    reps = LANES // HEAD_DIM
    gq = (jnp.tile(g_q, reps) * (HEAD_DIM ** -0.5 * LOG2E)).reshape(1, LANES)
    gk = jnp.tile(g_k, reps).reshape(1, LANES)
    lane = np.arange(LANES)
    bd = lane[:, None] // HEAD_DIM == lane[None, :] // HEAD_DIM
    bd = jnp.asarray(np.concatenate([bd, bd], axis=0), BF16)
    nsteps = tot // TM_IN
    sub = TM_IN // CTX
    const = lambda *shape: pl.BlockSpec(shape, lambda bb, s: (0,) * len(shape))
    tab = pl.BlockSpec((TM_IN, LANES), lambda bb, s: (s, 0))
    out = lambda w: pl.BlockSpec((1, TM_IN, w), lambda bb, s: (bb, s, 0))
    shp = lambda w, dt: jax.ShapeDtypeStruct((b, tot, w), dt)
    xsub = lambda j: pl.BlockSpec((1, CTX, D_MODEL), lambda bb, s: (bb, jnp.maximum(sub * s + j - 1, 0), 0))
    return pl.pallas_call(
        _inproj_kernel,
        grid=(b, nsteps),
        in_specs=[xsub(0), xsub(1), xsub(2),
                  pl.BlockSpec((1, CTX, D_MODEL), lambda bb, s: (bb, 0, 0)),
                  pl.BlockSpec((1, 1, 6 * D_MODEL), lambda bb, s: (bb, 0, 0)),
                  pl.BlockSpec((1, 1, 6 * D_MODEL), lambda bb, s: (b, 0, 0)),
                  const(1, D_MODEL), const(D_MODEL, IN_COLS_PAD), const(1, LANES), const(1, LANES),
                  tab, tab, tab, const(2 * LANES, LANES)],
        out_specs=[out(ATTN_WIDTH), out(2 * LANES), out(2 * LANES), out(SSM_WIDTH), out(XBC_WIDTH), out(LANES)],
        out_shape=[shp(ATTN_WIDTH, BF16), shp(2 * LANES, BF16), shp(2 * LANES, BF16),
                   shp(SSM_WIDTH, BF16), shp(XBC_WIDTH, BF16), shp(LANES, F32)],
        compiler_params=_params("arbitrary", "arbitrary"),
        name="inproj",
    )(x, x, x, ctx, mod3, mod3, g_mix.reshape(1, -1), w_in_p, gq, gk, cos, sina, sinb, bd)


def _place_kv_head(t, kh, par):
    src = 0 if par == kh else 1
    lane = lax.broadcasted_iota(jnp.int32, (t.shape[0], LANES), 1)
    half = (lane >= HEAD_DIM) if par else (lane < HEAD_DIM)
    tile = t[:, src * LANES:(src + 1) * LANES]
    return jnp.where(half, tile, jnp.zeros_like(tile))


def _attn_kernel(sink_ref, q_ref, k_ref, v_ref, o_ref, kx_scr, vx_scr):
    seq = o_ref.shape[1]
    nblk = seq // BLOCK
    qi = lax.broadcasted_iota(jnp.int32, (BLOCK, BLOCK), 0)
    kj = lax.broadcasted_iota(jnp.int32, (BLOCK, BLOCK), 1)
    zero = jnp.zeros((BLOCK, BLOCK), F32)
    tri_prev = jnp.where(kj >= qi, zero, NEG_INF)
    tri_next = jnp.where(kj <= qi, zero, NEG_INF)
    combos = [(kh, par) for kh in range(KV_HEADS) for par in range(2)]
    for idx, (kh, par) in enumerate(combos):
        kx_scr[idx] = _place_kv_head(k_ref[0, 0:CTX, :], kh, par)
        vx_scr[idx] = _place_kv_head(v_ref[0, 0:CTX, :], kh, par)

    def block(nb, carry):
        r0 = pl.multiple_of(CTX + nb * BLOCK, BLOCK)
        rn = pl.multiple_of(CTX + jnp.minimum(nb + 1, nblk - 1) * BLOCK, BLOCK)
        rp = pl.multiple_of(CTX - BLOCK + nb * BLOCK, BLOCK)
        kloc = jnp.concatenate([k_ref[0, pl.ds(rp, 2 * BLOCK), :], k_ref[0, pl.ds(rn, BLOCK), :]], axis=0)
        vloc = jnp.concatenate([v_ref[0, pl.ds(rp, 2 * BLOCK), :], v_ref[0, pl.ds(rn, BLOCK), :]], axis=0)
        bias_p = jnp.where(nb > 0, tri_prev, NEG_INF)
        bias_n = jnp.where(nb < nblk - 1, tri_next, NEG_INF)
        q = q_ref[0, pl.ds(r0, BLOCK), :]
        q2s = [jnp.concatenate([q[:, (2 * kh) * LANES:(2 * kh + 1) * LANES],
                                q[:, (2 * kh + 1) * LANES:(2 * kh + 2) * LANES]], axis=0)
               for kh in range(KV_HEADS)]
        sc, ee, oo = {}, {}, {}

        def scores(idx):
            kh, par = combos[idx]
            sc[idx] = (_dot_nt(q2s[kh], _place_kv_head(kloc, kh, par)),
                       _dot_nt(q2s[kh], kx_scr[idx]))

        def softmax(idx):
            kh, par = combos[idx]
            s_loc, s_ctx = sc[idx]
            es, inv = [], []
            for r in range(2):
                head = 4 * kh + par + 2 * r
                rs = slice(r * BLOCK, (r + 1) * BLOCK)
                sv = jnp.concatenate([s_loc[rs, 0:BLOCK] + bias_p, s_loc[rs, BLOCK:2 * BLOCK],
                                      s_loc[rs, 2 * BLOCK:3 * BLOCK] + bias_n, s_ctx[rs]], axis=1)
                sk = sink_ref[head] * LOG2E
                m = jnp.maximum(jnp.max(sv, axis=-1, keepdims=True), sk)
                e = jnp.exp2(sv - m)
                inv.append(1.0 / (jnp.sum(e, axis=-1, keepdims=True) + jnp.exp2(sk - m)))
                es.append(e.astype(BF16))
            ee[idx] = (jnp.concatenate(es, axis=0), inv)

        def pv(idx):
            kh, par = combos[idx]
            e16, inv = ee[idx]
            o2 = _dot(e16[:, 0:3 * BLOCK], _place_kv_head(vloc, kh, par)) + _dot(e16[:, 3 * BLOCK:], vx_scr[idx])
            oo[idx] = [o2[r * BLOCK:(r + 1) * BLOCK] * inv[r] for r in range(2)]

        scores(0); scores(1); softmax(0); scores(2); softmax(1); pv(0)
        scores(3); softmax(2); pv(1); softmax(3); pv(2); pv(3)
        for kh in range(KV_HEADS):
            for r in range(2):
                c = 2 * kh + r
                val = oo[combos.index((kh, 0))][r] + oo[combos.index((kh, 1))][r]
                o_ref[0, pl.ds(pl.multiple_of(nb * BLOCK, BLOCK), BLOCK), c * LANES:(c + 1) * LANES] = val.astype(BF16)
        return carry

    lax.fori_loop(0, nblk, block, 0)


def _attention(q, kk, vv, sink):
    b, tot = q.shape[0], q.shape[1]
    seq = tot - CTX
    whole = lambda w: pl.BlockSpec((1, tot, w), lambda bb: (bb, 0, 0))
    return pl.pallas_call(
        _attn_kernel,
        grid=(b,),
        in_specs=[pl.BlockSpec(memory_space=pltpu.SMEM), whole(ATTN_WIDTH), whole(2 * LANES), whole(2 * LANES)],
        out_specs=pl.BlockSpec((1, seq, ATTN_WIDTH), lambda bb: (bb, 0, 0)),
        out_shape=jax.ShapeDtypeStruct((b, seq, ATTN_WIDTH), BF16),
        scratch_shapes=[pltpu.VMEM((2 * KV_HEADS, CTX, LANES), BF16), pltpu.VMEM((2 * KV_HEADS, CTX, LANES), BF16)],
        compiler_params=_params("arbitrary"),
        name="attn",
    )(sink, q, kk, vv)


def _ssd_selectors():
    nh2 = 2 * SSM_HEADS
    npairs = SSM_HEADS // 2
    k = np.arange(LANES)
    part, q = k // (2 * nh2), k % (2 * nh2)
    live = part < 3
    col = np.arange(nh2 * LANES)
    tile, lane = col // LANES, col % LANES
    kind, blk = tile // (nh2 // 2), tile % (nh2 // 2)
    src = kind * nh2 + (blk // npairs) * SSM_HEADS + 2 * (blk % npairs) + (lane >= SSM_HEAD_DIM)
    pairsel = live[:, None] & (q[:, None] == src[None, :])
    segind = live[:, None] & (q[:, None] < nh2) & (q[:, None] == tile[None, :])
    return jnp.asarray(pairsel, BF16), jnp.asarray(segind, BF16)


def _ssd_kernel(x_ref, dt_ref, z_ref, cw_ref, cb_ref, dtb_ref, alog_ref, dsk_ref, g_ref, tri_ref, shift_ref,
                pairsel_ref, segind_ref,
                y_ref, hf_scr, hb_scr, yacc_scr, stb_scr, decb_scr, eb_scr, cbuf_scr, segrhs_scr, cols3_scr, yt_scr):
    tot = x_ref.shape[1]
    nchunks = tot // CHUNK
    nctx = CTX // CHUNK
    npairs = SSM_HEADS // 2
    pairs_per_group = npairs // SSM_GROUPS
    group_cols = pairs_per_group * LANES

    hf_scr[...] = jnp.zeros_like(hf_scr)
    hb_scr[...] = jnp.zeros_like(hb_scr)
    segrhs_scr[...] = segind_ref[...]

    nh2 = 2 * SSM_HEADS

    def prep(chunk):
        c0 = pl.multiple_of(chunk * CHUNK, CHUNK)
        dtv = _softplus(dt_ref[0, pl.ds(c0, CHUNK), :].T[0:nh2, :] + dtb_ref[...])
        a = dtv * (-jnp.exp(alog_ref[...]) * LOG2E)

        def split3(v):
            hi = v.astype(BF16)
            r1 = v - hi.astype(F32)
            mid = r1.astype(BF16)
            return hi, mid, (r1 - mid.astype(F32)).astype(BF16)

        parts = jnp.concatenate(split3(a), axis=0)
        fold = lambda p: p[0:nh2] + p[nh2:2 * nh2] + p[2 * nh2:3 * nh2]
        prefix = fold(_dot(parts, tri_ref[0]))
        suffix = fold(_dot(parts, tri_ref[1]))
        head_row = lax.broadcasted_iota(jnp.int32, (nh2, CHUNK), 0)
        ac_t = jnp.where(head_row < SSM_HEADS, prefix, suffix)

        v32 = [p.astype(F32) for p in split3(jnp.concatenate([ac_t, dtv], axis=0))]
        ones3 = (lax.broadcasted_iota(jnp.int32, (CHUNK - 6 * nh2, CHUNK), 0) < 3).astype(F32)
        cols3_scr[...] = jnp.concatenate(v32 + [ones3], axis=0).T.astype(BF16)
        for q in range(nh2):
            neg = [-p[q:q + 1, :] for p in v32]
            tile = jnp.concatenate(neg + [jnp.zeros((BF16_SUBLANES - 3, CHUNK), F32)], axis=0)
            segrhs_scr[6 * nh2:6 * nh2 + BF16_SUBLANES, q * CHUNK:(q + 1) * CHUNK] = tile.astype(BF16)

    prep(0)

    def chunk_pass(chunk, carry):
        lc = jnp.maximum(chunk - nctx, 0)
        r0 = pl.multiple_of(lc * CHUNK, CHUNK)
        c0 = pl.multiple_of(chunk * CHUNK, CHUNK)
        p0 = pl.multiple_of(jnp.maximum(chunk * CHUNK - HALO, 0), HALO)
        n0 = pl.multiple_of(jnp.minimum(chunk * CHUNK + CHUNK, tot - HALO), HALO)

        cols3 = cols3_scr[...]
        x16 = x_ref[0, pl.ds(c0, CHUNK), :]
        has_prev = (chunk != 0) & (chunk != nctx)
        has_next = (chunk != nctx - 1) & (chunk != nchunks - 1)
        halo_p = x_ref[0, pl.ds(p0, HALO), :]
        halo_n = x_ref[0, pl.ds(n0, HALO), :]
        xext = jnp.concatenate([jnp.where(has_prev, halo_p, jnp.zeros_like(halo_p)), x16,
                                jnp.where(has_next, halo_n, jnp.zeros_like(halo_n))], axis=0)
        shifted = _dot(shift_ref[...], xext)
        pairs = _dot(cols3, pairsel_ref[...])
        seg_all = _dot(cols3, segrhs_scr[...])

        u = (cw_ref[0:1, :] * shifted[0:CHUNK] + cw_ref[1:2, :] * x16.astype(F32)
             + cw_ref[2:3, :] * shifted[CHUNK:2 * CHUNK] + cb_ref[...])
        u = u * _sigmoid(u)
        xs = u[:, 0:SSM_WIDTH]
        bmat = u[:, SSM_WIDTH:SSM_WIDTH + SSM_GROUPS * D_STATE]
        cmat = u[:, SSM_WIDTH + SSM_GROUPS * D_STATE:XBC_WIDTH]
        cb16 = [cmat[:, g * D_STATE:(g + 1) * D_STATE].astype(BF16) for g in range(SSM_GROUPS)]
        bb16 = [bmat[:, g * D_STATE:(g + 1) * D_STATE].astype(BF16) for g in range(SSM_GROUPS)]
        bt16 = [bmat[:, g * D_STATE:(g + 1) * D_STATE].T.astype(BF16) for g in range(SSM_GROUPS)]
        cbuf_scr[lc] = jnp.concatenate(cb16, axis=1)

        hf_start = [hf_scr[:, g * group_cols:(g + 1) * group_cols] for g in range(SSM_GROUPS)]
        gmat = [_dot_nt(cb16[g], bb16[g]) for g in range(SSM_GROUPS)]
        yoff = [_dot(cb16[g], hf_start[g].astype(BF16)) for g in range(SSM_GROUPS)]

        half0 = lax.broadcasted_iota(jnp.int32, (CHUNK, LANES), 1) < SSM_HEAD_DIM
        ii = lax.broadcasted_iota(jnp.int32, (CHUNK, CHUNK), 0)
        jj = lax.broadcasted_iota(jnp.int32, (CHUNK, CHUNK), 1)
        causal = [jj <= ii, jj >= ii]
        last = [CHUNK - 1, 0]

        arg_pair, tot_pair, xdt_pair, xdtd_pair = [], [], [], []
        for d in range(2):
            args, tots, xdts, xdtds = [], [], [], []
            for m in range(npairs):
                blk = d * npairs + m
                arg = pairs[:, blk * LANES:(blk + 1) * LANES]
                dtp = pairs[:, (nh2 // 2 + blk) * LANES:(nh2 // 2 + blk + 1) * LANES]
                total = arg[last[d]:last[d] + 1, :]
                xdt = xs[:, m * LANES:(m + 1) * LANES] * dtp
                args.append(arg)
                tots.append(total)
                xdts.append(xdt)
                xdtds.append(xdt * jnp.exp2(total - arg))
            arg_pair.append(args)
            tot_pair.append(tots)
            xdt_pair.append(xdts)
            xdtd_pair.append(xdtds)

        for g in range(SSM_GROUPS):
            pr = range(g * pairs_per_group, (g + 1) * pairs_per_group)
            gs = slice(g * group_cols, (g + 1) * group_cols)
            st_f = _dot(bt16[g], jnp.concatenate([xdtd_pair[0][m] for m in pr], axis=1).astype(BF16))
            st_b = _dot(bt16[g], jnp.concatenate([xdtd_pair[1][m] for m in pr], axis=1).astype(BF16))
            dec_f = jnp.concatenate([jnp.exp2(tot_pair[0][m]) for m in pr], axis=1)
            dec_b = jnp.concatenate([jnp.exp2(tot_pair[1][m]) for m in pr], axis=1)
            hf_scr[:, gs] = hf_start[g] * dec_f + st_f
            stb_scr[chunk, :, gs] = st_b
            decb_scr[chunk, :, gs] = jnp.broadcast_to(dec_b, (8, group_cols))

        prep(jnp.minimum(chunk + 1, nchunks - 1))
        operands = {}

        def intra_operands(m):
            g = m // pairs_per_group
            smats, rhs = [], []
            for d in range(2):
                for hh in range(2):
                    r = d * SSM_HEADS + 2 * m + hh
                    seg = seg_all[:, r * CHUNK:(r + 1) * CHUNK]
                    smats.append(jnp.where(causal[d], gmat[g] * jnp.exp2(seg), 0.0).astype(BF16))
                xd = xdt_pair[d][m]
                rhs += [jnp.where(half0, xd, 0.0), jnp.where(half0, 0.0, xd)]
            operands[m] = (jnp.concatenate(smats, axis=1), jnp.concatenate(rhs, axis=0).astype(BF16))

        def intra_matmul(m):
            g, mm = m // pairs_per_group, m % pairs_per_group
            ms = slice(m * LANES, (m + 1) * LANES)
            ydiag = _dot(*operands[m])
            yacc_scr[pl.ds(r0, CHUNK), ms] = (ydiag + yoff[g][:, mm * LANES:(mm + 1) * LANES]
                                              * jnp.exp2(arg_pair[0][m]) + dsk_ref[:, ms] * xs[:, ms])
            eb_scr[lc, :, ms] = jnp.exp2(arg_pair[1][m])

        intra_operands(0)
        for m in range(1, npairs):
            intra_operands(m)
            intra_matmul(m - 1)
        intra_matmul(npairs - 1)
        return carry

    lax.fori_loop(0, nchunks, chunk_pass, 0)

    yt_scr[...] = jnp.zeros_like(yt_scr)
    order = lambda i: jnp.where(i < nctx, nctx - 1 - i, nchunks + nctx - 1 - i)

    def carry_pass(i, carry):
        chunk = order(jnp.minimum(i, nchunks - 1))
        lc = jnp.maximum(chunk - nctx, 0)
        r0 = pl.multiple_of(lc * CHUNK, CHUNK)
        hb = hb_scr[...]
        cb = cbuf_scr[lc]
        yoff = jnp.concatenate(
            [_dot(cb[:, g * D_STATE:(g + 1) * D_STATE], hb[:, g * group_cols:(g + 1) * group_cols].astype(BF16))
             for g in range(SSM_GROUPS)], axis=1)

        prev = order(jnp.maximum(i - 1, 0))
        pr0 = pl.multiple_of(jnp.maximum(prev - nctx, 0) * CHUNK, CHUNK)
        zf = z_ref[0, pl.ds(pl.multiple_of(prev * CHUNK, CHUNK), CHUNK), :].astype(F32)
        yt = yt_scr[...] * (zf * _sigmoid(zf))
        gw = SSM_WIDTH // SSM_GROUPS
        for g in range(SSM_GROUPS):
            yg = yt[:, g * gw:(g + 1) * gw]
            ms = jnp.mean(yg * yg, axis=-1, keepdims=True)
            y_ref[0, pl.ds(pr0, CHUNK), g * gw:(g + 1) * gw] = (yg * lax.rsqrt(ms + EPS)
                                                                * g_ref[:, g * gw:(g + 1) * gw]).astype(BF16)

        yt_scr[...] = yacc_scr[pl.ds(r0, CHUNK), :] + yoff * eb_scr[lc]
        hb_scr[...] = hb * decb_scr[chunk, 0:1, :] + stb_scr[chunk]
        return carry

    lax.fori_loop(0, nchunks + 1, carry_pass, 0)


def _ssd(xbc, dt, z, conv_w, conv_b, dtb16, alog16, dsk, g_ssm):
    b, tot = xbc.shape[0], xbc.shape[1]
    seq = tot - CTX
    nchunks = tot // CHUNK
    nctx = CTX // CHUNK
    nlat = nchunks - nctx
    const = lambda *shape: pl.BlockSpec(shape, lambda bb: (0,) * len(shape))
    whole = lambda rows, w: pl.BlockSpec((1, rows, w), lambda bb: (bb, 0, 0))
    jdx = np.arange(CHUNK)
    tri = jnp.asarray(np.stack([jdx[:, None] <= jdx[None, :], jdx[:, None] >= jdx[None, :]]), BF16)
    kdx = np.arange(CHUNK + 2 * HALO)
    shift = jnp.asarray(np.concatenate([kdx[None, :] == jdx[:, None] + HALO - 1,
                                        kdx[None, :] == jdx[:, None] + HALO + 1], axis=0), BF16)
    return pl.pallas_call(
        _ssd_kernel,
        grid=(b,),
        in_specs=[whole(tot, XBC_WIDTH), whole(tot, LANES), whole(tot, SSM_WIDTH),
                  const(3, XBC_WIDTH), const(1, XBC_WIDTH), const(2 * SSM_HEADS, CHUNK), const(2 * SSM_HEADS, CHUNK),
                  const(1, SSM_WIDTH), const(1, SSM_WIDTH), const(2, CHUNK, CHUNK),
                  const(2 * CHUNK, CHUNK + 2 * HALO),
                  const(LANES, 2 * SSM_HEADS * LANES), const(LANES, 2 * SSM_HEADS * LANES)],
        out_specs=whole(seq, SSM_WIDTH),
        out_shape=jax.ShapeDtypeStruct((b, seq, SSM_WIDTH), BF16),
        scratch_shapes=[pltpu.VMEM((D_STATE, SSM_WIDTH), F32),
                        pltpu.VMEM((D_STATE, SSM_WIDTH), F32),
                        pltpu.VMEM((seq, SSM_WIDTH), F32),
                        pltpu.VMEM((nchunks, D_STATE, SSM_WIDTH), F32),
                        pltpu.VMEM((nchunks, 8, SSM_WIDTH), F32),
                        pltpu.VMEM((nlat, CHUNK, SSM_WIDTH), F32),
                        pltpu.VMEM((nlat, CHUNK, SSM_GROUPS * D_STATE), BF16),
                        pltpu.VMEM((LANES, 2 * SSM_HEADS * LANES), BF16),
                        pltpu.VMEM((CHUNK, LANES), BF16),
                        pltpu.VMEM((CHUNK, SSM_WIDTH), F32)],
        compiler_params=_params("arbitrary"),
        name="ssd",
    )(xbc, dt, z, conv_w, conv_b, dtb16, alog16, dsk, g_ssm, tri, shift, *_ssd_selectors())


def _mixffn_kernel(am_ref, ap_ref, an_ref, ym_ref, yp_ref, yn_ref, xm_ref, xp_ref, xn_ref,
                   wo_ref, wup_ref, cw_ref, cb_ref, wd_ref, mod_ref, g_ref, o_ref):
    i = pl.program_id(1)
    tm = am_ref.shape[1]
    nrows = tm + 2 * HALO
    ext = lambda p, m, n: jnp.concatenate([p[0], m[0], n[0]], axis=0)
    o = (_dot(ext(ap_ref, am_ref, an_ref), wo_ref[0:ATTN_WIDTH, :])
         + _dot(ext(yp_ref, ym_ref, yn_ref), wo_ref[ATTN_WIDTH:D_MODEL, :]))
    mod = mod_ref[0]
    xnew = ext(xp_ref, xm_ref, xn_ref) + mod[:, 2 * D_MODEL:3 * D_MODEL] * o
    ms = jnp.mean(xnew * xnew, axis=-1, keepdims=True)
    hn = xnew * lax.rsqrt(ms + EPS) * g_ref[...]
    h2 = hn * (1.0 + mod[:, 4 * D_MODEL:5 * D_MODEL]) + mod[:, 3 * D_MODEL:4 * D_MODEL]
    row = lax.broadcasted_iota(jnp.int32, (nrows, 1), 0)
    lo = jnp.where(i > 0, 0, HALO)
    hi = jnp.where(i < pl.num_programs(1) - 1, nrows, nrows - HALO)
    lhs = jnp.where(row >= lo, jnp.where(row < hi, h2, 0.0), 0.0).astype(BF16)

    chunks = [(c0, min(c0 + TF_FFN, D_FF)) for c0 in range(0, D_FF, TF_FFN)]
    ups, parts = {}, []

    def up_proj(k):
        c0, c1 = chunks[k]
        ups[k] = (_dot(lhs, wup_ref[:, c0:c1]), _dot(lhs, wup_ref[:, D_FF + c0:D_FF + c1]))

    def conv(uu, c0, c1):
        up = pltpu.roll(uu, 1, 0)[HALO:HALO + tm]
        dn = pltpu.roll(uu, nrows - 1, 0)[HALO:HALO + tm]
        return (cw_ref[0:1, c0:c1] * up + cw_ref[1:2, c0:c1] * uu[HALO:HALO + tm]
                + cw_ref[2:3, c0:c1] * dn + cb_ref[:, c0:c1])

    def gate_down(k):
        c0, c1 = chunks[k]
        ua = conv(ups[k][0], c0, c1)
        ug = conv(ups[k][1], D_FF + c0, D_FF + c1)
        act = (ua * (ug * _sigmoid(ug))).astype(BF16)
        parts.append(_dot(act, wd_ref[c0:c1, :]))

    up_proj(0)
    for k in range(1, len(chunks)):
        up_proj(k)
        gate_down(k - 1)
    gate_down(len(chunks) - 1)
    acc = parts[0]
    for part in parts[1:]:
        acc = acc + part
    o_ref[0] = xnew[HALO:HALO + tm] + mod[:, 5 * D_MODEL:6 * D_MODEL] * acc


def _mixffn(attn, y, x, w_out16, w_up16, conv_w, conv_b, w_down16, mod3, g_ffn):
    b, seq = x.shape[0], x.shape[1]
    ni = seq // TM_FFN
    hpt = TM_FFN // HALO
    nhalo = seq // HALO
    main = lambda w: pl.BlockSpec((1, TM_FFN, w), lambda bb, i: (bb, i, 0))
    prev = lambda w: pl.BlockSpec((1, HALO, w), lambda bb, i: (bb, jnp.maximum(i * hpt - 1, 0), 0))
    nxt = lambda w: pl.BlockSpec((1, HALO, w), lambda bb, i: (bb, jnp.minimum(i * hpt + hpt, nhalo - 1), 0))
    resident = lambda *shape: pl.BlockSpec(shape, lambda bb, i: (0,) * len(shape), pipeline_mode=pl.Buffered(1))
    trio = lambda w: [main(w), prev(w), nxt(w)]
    return pl.pallas_call(
        _mixffn_kernel,
        grid=(b, ni),
        in_specs=trio(ATTN_WIDTH) + trio(SSM_WIDTH) + trio(D_MODEL) + [
            resident(D_MODEL, D_MODEL), resident(D_MODEL, 2 * D_FF), resident(3, 2 * D_FF), resident(1, 2 * D_FF),
            resident(D_FF, D_MODEL),
            pl.BlockSpec((1, 1, 6 * D_MODEL), lambda bb, i: (bb, 0, 0)),
            resident(1, D_MODEL)],
        out_specs=main(D_MODEL),
        out_shape=jax.ShapeDtypeStruct((b, seq, D_MODEL), F32),
        compiler_params=_params("arbitrary", "arbitrary"),
        name="mixffn",
    )(attn, attn, attn, y, y, y, x, x, x, w_out16, w_up16, conv_w, conv_b.reshape(1, -1), w_down16, mod3,
      g_ffn.reshape(1, -1))


def _layer(x, c, ctx, c_ctx, w_mod, b_mod, g_mix, w_in, g_q, g_k, sink, ssm_conv_w, ssm_conv_b,
           a_log, dt_bias, d_skip, g_ssm, w_out, g_ffn, w_up, ffn_conv_w, ffn_conv_b, w_down):
    mod3 = _modulation(c, c_ctx, w_mod, b_mod)

    w_in_p = jnp.pad(w_in, ((0, 0), (0, IN_COLS_PAD - IN_COLS))).astype(BF16)
    q, kk, vv, z, xbc, dt = _inproj(x, ctx, mod3, g_mix, w_in_p, g_q, g_k)
    attn = _attention(q, kk, vv, sink)

    head_rows = lambda t: jnp.broadcast_to(t.reshape(2 * SSM_HEADS, 1), (2 * SSM_HEADS, CHUNK))
    y = _ssd(xbc, dt, z, ssm_conv_w, ssm_conv_b.reshape(1, -1), head_rows(dt_bias), head_rows(a_log),
             jnp.repeat(d_skip, SSM_HEAD_DIM).reshape(1, -1), g_ssm.reshape(1, -1))

    return _mixffn(attn, y, x, w_out.astype(BF16), w_up.astype(BF16), ffn_conv_w, ffn_conv_b, w_down.astype(BF16),
                   mod3, g_ffn)


def kernel(x, c, ctx, c_ctx, w_mod, b_mod, g_mix, w_in, g_q, g_k, sink, ssm_conv_w, ssm_conv_b,
           a_log, dt_bias, d_skip, g_ssm, w_out, g_ffn, w_up, ffn_conv_w, ffn_conv_b, w_down):
    assert w_mod.shape[0] == 1, "single-layer block"
    first = lambda t: t.reshape(t.shape[1:])
    return _layer(x, c, ctx, c_ctx, first(w_mod), first(b_mod), first(g_mix), first(w_in), first(g_q),
                  first(g_k), first(sink), first(ssm_conv_w), first(ssm_conv_b), first(a_log),
                  first(dt_bias), first(d_skip), first(g_ssm), first(w_out), first(g_ffn), first(w_up),
                  first(ffn_conv_w), first(ffn_conv_b), first(w_down))
```

```python
import jax
import jax.numpy as jnp
import numpy as np
from jax import lax
from jax.experimental import pallas as pl
from jax.experimental.pallas import tpu as pltpu

F32 = jnp.float32
BF16 = jnp.bfloat16

D_MODEL = 1024
CTX = 256
GRID_W = 64
HEAD_DIM = 64
ATTN_HEADS = 8
KV_HEADS = 2
ATTN_WIDTH = ATTN_HEADS * HEAD_DIM
KV_WIDTH = KV_HEADS * HEAD_DIM
WINDOW = 128
BLOCK = 128
ROPE_BASE = 10000.0
SSM_WIDTH = D_MODEL - ATTN_WIDTH
SSM_HEAD_DIM = 64
SSM_HEADS = SSM_WIDTH // SSM_HEAD_DIM
SSM_GROUPS = 2
D_STATE = 128
XBC_WIDTH = SSM_WIDTH + 2 * SSM_GROUPS * D_STATE
CHUNK = 128
D_FF = 2816
IN_COLS = ATTN_WIDTH + 2 * KV_WIDTH + SSM_WIDTH + XBC_WIDTH + 2 * SSM_HEADS
EPS = 1e-6
NEG_INF = -1e30
LOG2E = 1.4426950408889634

LANES = 128
BF16_SUBLANES = 16
IN_COLS_PAD = 19 * LANES
COL_Q, COL_K, COL_V = 0, ATTN_WIDTH, ATTN_WIDTH + KV_WIDTH
COL_Z = ATTN_WIDTH + 2 * KV_WIDTH
COL_XBC = COL_Z + SSM_WIDTH
COL_DT = COL_XBC + XBC_WIDTH
VMEM_LIMIT = 56 * 1024 * 1024

TM_IN = 3 * CTX
TM_FFN = 512
TF_FFN = 1024
HALO = BF16_SUBLANES


def _params(*sem):
    return pltpu.CompilerParams(dimension_semantics=sem, vmem_limit_bytes=VMEM_LIMIT)


def _sigmoid(v):
    return 1.0 / (1.0 + jnp.exp2(v * (-LOG2E)))


def _softplus(v):
    return jnp.maximum(v, 0.0) + jnp.log1p(jnp.exp(-jnp.abs(v)))


def _dot(a, b):
    return jnp.dot(a, b, preferred_element_type=F32)


def _dot_nt(a, b):
    return lax.dot_general(a, b, (((1,), (1,)), ((), ())), preferred_element_type=F32)


def _mod_kernel(c_ref, cctx_ref, w_ref, b_ref, o_ref):
    rows = o_ref.shape[0]
    b = c_ref.shape[0]
    cv = jnp.concatenate([c_ref[...], cctx_ref[...], jnp.zeros((rows - b - 1, D_MODEL), F32)], axis=0)
    s = (cv * _sigmoid(cv)).astype(BF16)
    res = _dot(s, w_ref[...].astype(BF16)) + b_ref[...]
    for r in range(b + 1):
        o_ref[r] = res[r:r + 1, :]
    o_ref[b + 1:rows] = jnp.zeros((rows - b - 1, 1, res.shape[1]), F32)


def _modulation(c, c_ctx, w_mod, b_mod):
    b = c.shape[0]
    rows = 8 * ((b + 1 + 7) // 8)
    tn = 1024
    return pl.pallas_call(
        _mod_kernel,
        grid=(6 * D_MODEL // tn,),
        in_specs=[pl.BlockSpec((b, D_MODEL), lambda j: (0, 0)),
                  pl.BlockSpec((1, D_MODEL), lambda j: (0, 0)),
                  pl.BlockSpec((D_MODEL, tn), lambda j: (0, j)),
                  pl.BlockSpec((1, tn), lambda j: (0, j))],
        out_specs=pl.BlockSpec((rows, 1, tn), lambda j: (0, 0, j)),
        out_shape=jax.ShapeDtypeStruct((rows, 1, 6 * D_MODEL), F32),
        compiler_params=_params("arbitrary"),
        name="mod",
    )(c, c_ctx.reshape(1, -1), w_mod, b_mod.reshape(1, -1))


def _inproj_kernel(xa_ref, xb_ref, xc_ref, ctx_ref, modb_ref, modc_ref, gmix_ref, w_ref, gq_ref, gk_ref,
                   cos_ref, sina_ref, sinb_ref, bd_ref,
                   q_ref, k_ref, v_ref, z_ref, xbc_ref, dt_ref, w16_scr):
    first = pl.program_id(1) == 0

    @pl.when(first & (pl.program_id(0) == 0))
    def _():
        w16_scr[:, 0:IN_COLS] = w_ref[...].astype(BF16)
        w16_scr[:, IN_COLS:IN_COLS_PAD] = jnp.zeros((D_MODEL, IN_COLS_PAD - IN_COLS), BF16)

    def normed(xin, mod):
        ms = jnp.mean(xin * xin, axis=-1, keepdims=True)
        hn = xin * lax.rsqrt(ms + EPS) * gmix_ref[...]
        return (hn * (1.0 + mod[:, D_MODEL:2 * D_MODEL]) + mod[:, 0:D_MODEL]).astype(BF16)

    modb = modb_ref[0]
    h = jnp.concatenate([normed(jnp.where(first, ctx_ref[0], xa_ref[0]), jnp.where(first, modc_ref[0], modb)),
                         normed(xb_ref[0], modb), normed(xc_ref[0], modb)], axis=0)

    cos = cos_ref[...]
    sina = sina_ref[...]
    sinb = sinb_ref[...]
    bd = bd_ref[...]

    def norm_rope(pf, gain):
        ss = pf * pf
        hi = ss.astype(BF16)
        lo = (ss - hi.astype(F32)).astype(BF16)
        sums = _dot(jnp.concatenate([hi, lo], axis=1), bd)
        y = pf * lax.rsqrt(sums * (1.0 / HEAD_DIM) + EPS) * gain
        return y * cos + pltpu.roll(y, LANES - 16, 1) * sina + pltpu.roll(y, 16, 1) * sinb

    def q_chunks(lo, hi):
        for cidx in range(lo, hi):
            cs = slice(cidx * LANES, (cidx + 1) * LANES)
            q_ref[0, :, cs] = norm_rope(qf[:, cs], gq_ref[...]).astype(BF16)

    qf = _dot(h, w16_scr[:,COL_Q:COL_Q + ATTN_WIDTH])
    kvf = _dot(h, w16_scr[:,COL_K:COL_K + 2 * KV_WIDTH])
    zf = _dot(h, w16_scr[:,COL_Z:COL_Z + SSM_WIDTH])
    q_chunks(0, 2)
    xf = _dot(h, w16_scr[:,COL_XBC:COL_XBC + XBC_WIDTH])
    q_chunks(2, ATTN_WIDTH // LANES)
    dtf = _dot(h, w16_scr[:,COL_DT:COL_DT + LANES])
    kf = norm_rope(kvf[:, 0:KV_WIDTH], gk_ref[...])
    k_ref[0, :, 0:LANES] = kf.astype(BF16)
    k_ref[0, :, LANES:2 * LANES] = pltpu.roll(kf, HEAD_DIM, 1).astype(BF16)
    vf = kvf[:, KV_WIDTH:2 * KV_WIDTH]
    v_ref[0, :, 0:LANES] = vf.astype(BF16)
    v_ref[0, :, LANES:2 * LANES] = pltpu.roll(vf, HEAD_DIM, 1).astype(BF16)
    z_ref[0] = zf.astype(BF16)
    xbc_ref[0] = xf.astype(BF16)
    dt_ref[0] = dtf


def _rope_tables(seq):
    rows = seq // GRID_W
    pos_r = np.repeat(np.arange(rows), GRID_W).astype(np.float32)
    pos_c = np.tile(np.arange(GRID_W), rows).astype(np.float32)
    quarter = HEAD_DIM // 4
    freqs = (ROPE_BASE ** (-np.arange(quarter, dtype=np.float32) / quarter)).astype(np.float32)
    ang_r = pos_r[:, None] * freqs[None, :]
    ang_c = pos_c[:, None] * freqs[None, :]
    cr, sr, cc, sc = np.cos(ang_r), np.sin(ang_r), np.cos(ang_c), np.sin(ang_c)
    zero = np.zeros_like(sr)
    cos = np.concatenate([cr, cr, cc, cc], axis=-1)
    sina = np.concatenate([-sr, zero, -sc, zero], axis=-1)
    sinb = np.concatenate([zero, sr, zero, sc], axis=-1)
    reps = LANES // HEAD_DIM
    pad = lambda t, v: np.concatenate([np.full((CTX, LANES), v, np.float32), np.tile(t, (1, reps))], axis=0)
    return tuple(jnp.asarray(pad(t, v), F32) for t, v in ((cos, 1.0), (sina, 0.0), (sinb, 0.0)))


def _inproj(x, ctx, mod3, g_mix, w_in, g_q, g_k):
    b, seq = x.shape[0], x.shape[1]
    tot = CTX + seq
    cos, sina, sinb = _rope_tables(seq)
    reps = LANES // HEAD_DIM
    gq = (jnp.tile(g_q, reps) * (HEAD_DIM ** -0.5 * LOG2E)).reshape(1, LANES)
    gk = jnp.tile(g_k, reps).reshape(1, LANES)
    lane = np.arange(LANES)
    bd = lane[:, None] // HEAD_DIM == lane[None, :] // HEAD_DIM
    bd = jnp.asarray(np.concatenate([bd, bd], axis=0), BF16)
    nsteps = tot // TM_IN
    sub = TM_IN // CTX
    const = lambda *shape: pl.BlockSpec(shape, lambda bb, s: (0,) * len(shape))
    tab = pl.BlockSpec((TM_IN, LANES), lambda bb, s: (s, 0))
    out = lambda w: pl.BlockSpec((1, TM_IN, w), lambda bb, s: (bb, s, 0))
    shp = lambda w, dt: jax.ShapeDtypeStruct((b, tot, w), dt)
    xsub = lambda j: pl.BlockSpec((1, CTX, D_MODEL), lambda bb, s: (bb, jnp.maximum(sub * s + j - 1, 0), 0))
    return pl.pallas_call(
        _inproj_kernel,
        grid=(b, nsteps),
        in_specs=[xsub(0), xsub(1), xsub(2),
                  pl.BlockSpec((1, CTX, D_MODEL), lambda bb, s: (bb, 0, 0)),
                  pl.BlockSpec((1, 1, 6 * D_MODEL), lambda bb, s: (bb, 0, 0)),
                  pl.BlockSpec((1, 1, 6 * D_MODEL), lambda bb, s: (b, 0, 0)),
                  const(1, D_MODEL),
                  pl.BlockSpec((D_MODEL, IN_COLS), lambda bb, s: (0, 0), pipeline_mode=pl.Buffered(1)),
                  const(1, LANES), const(1, LANES),
                  tab, tab, tab, const(2 * LANES, LANES)],
        out_specs=[out(ATTN_WIDTH), out(2 * LANES), out(2 * LANES), out(SSM_WIDTH), out(XBC_WIDTH), out(LANES)],
        out_shape=[shp(ATTN_WIDTH, BF16), shp(2 * LANES, BF16), shp(2 * LANES, BF16),
                   shp(SSM_WIDTH, BF16), shp(XBC_WIDTH, BF16), shp(LANES, F32)],
        scratch_shapes=[pltpu.VMEM((D_MODEL, IN_COLS_PAD), BF16)],
        compiler_params=_params("arbitrary", "arbitrary"),
        name="inproj",
    )(x, x, x, ctx, mod3, mod3, g_mix.reshape(1, -1), w_in, gq, gk, cos, sina, sinb, bd)


def _place_kv_head(t, kh, par):
    src = 0 if par == kh else 1
    lane = lax.broadcasted_iota(jnp.int32, (t.shape[0], LANES), 1)
    half = (lane >= HEAD_DIM) if par else (lane < HEAD_DIM)
    tile = t[:, src * LANES:(src + 1) * LANES]
    return jnp.where(half, tile, jnp.zeros_like(tile))


def _attn_kernel(sink_ref, q_ref, k_ref, v_ref, o_ref, kx_scr, vx_scr):
    seq = o_ref.shape[1]
    nblk = seq // BLOCK
    qi = lax.broadcasted_iota(jnp.int32, (BLOCK, BLOCK), 0)
    kj = lax.broadcasted_iota(jnp.int32, (BLOCK, BLOCK), 1)
    zero = jnp.zeros((BLOCK, BLOCK), F32)
    tri_prev = jnp.where(kj >= qi, zero, NEG_INF)
    tri_next = jnp.where(kj <= qi, zero, NEG_INF)
    combos = [(kh, par) for kh in range(KV_HEADS) for par in range(2)]
    for idx, (kh, par) in enumerate(combos):
        kx_scr[idx] = _place_kv_head(k_ref[0, 0:CTX, :], kh, par)
        vx_scr[idx] = _place_kv_head(v_ref[0, 0:CTX, :], kh, par)

    def block(nb, carry):
        r0 = pl.multiple_of(CTX + nb * BLOCK, BLOCK)
        rn = pl.multiple_of(CTX + jnp.minimum(nb + 1, nblk - 1) * BLOCK, BLOCK)
        rp = pl.multiple_of(CTX - BLOCK + nb * BLOCK, BLOCK)
        kloc = jnp.concatenate([k_ref[0, pl.ds(rp, 2 * BLOCK), :], k_ref[0, pl.ds(rn, BLOCK), :]], axis=0)
        vloc = jnp.concatenate([v_ref[0, pl.ds(rp, 2 * BLOCK), :], v_ref[0, pl.ds(rn, BLOCK), :]], axis=0)
        bias_p = jnp.where(nb > 0, tri_prev, NEG_INF)
        bias_n = jnp.where(nb < nblk - 1, tri_next, NEG_INF)
        q = q_ref[0, pl.ds(r0, BLOCK), :]
        q2s = [jnp.concatenate([q[:, (2 * kh) * LANES:(2 * kh + 1) * LANES],
                                q[:, (2 * kh + 1) * LANES:(2 * kh + 2) * LANES]], axis=0)
               for kh in range(KV_HEADS)]
        sc, ee, oo = {}, {}, {}

        def scores(idx):
            kh, par = combos[idx]
            sc[idx] = (_dot_nt(q2s[kh], _place_kv_head(kloc, kh, par)),
                       _dot_nt(q2s[kh], kx_scr[idx]))

        def softmax(idx):
            kh, par = combos[idx]
            s_loc, s_ctx = sc[idx]
            es, inv = [], []
            for r in range(2):
                head = 4 * kh + par + 2 * r
                rs = slice(r * BLOCK, (r + 1) * BLOCK)
                sv = jnp.concatenate([s_loc[rs, 0:BLOCK] + bias_p, s_loc[rs, BLOCK:2 * BLOCK],
                                      s_loc[rs, 2 * BLOCK:3 * BLOCK] + bias_n, s_ctx[rs]], axis=1)
                sk = sink_ref[head] * LOG2E
                m = jnp.maximum(jnp.max(sv, axis=-1, keepdims=True), sk)
                e = jnp.exp2(sv - m)
                inv.append(1.0 / (jnp.sum(e, axis=-1, keepdims=True) + jnp.exp2(sk - m)))
                es.append(e.astype(BF16))
            ee[idx] = (jnp.concatenate(es, axis=0), inv)

        def pv(idx):
            kh, par = combos[idx]
            e16, inv = ee[idx]
            o2 = _dot(e16[:, 0:3 * BLOCK], _place_kv_head(vloc, kh, par)) + _dot(e16[:, 3 * BLOCK:], vx_scr[idx])
            oo[idx] = [o2[r * BLOCK:(r + 1) * BLOCK] * inv[r] for r in range(2)]

        scores(0); scores(1); softmax(0); scores(2); softmax(1); pv(0)
        scores(3); softmax(2); pv(1); softmax(3); pv(2); pv(3)
        for kh in range(KV_HEADS):
            for r in range(2):
                c = 2 * kh + r
                val = oo[combos.index((kh, 0))][r] + oo[combos.index((kh, 1))][r]
                o_ref[0, pl.ds(pl.multiple_of(nb * BLOCK, BLOCK), BLOCK), c * LANES:(c + 1) * LANES] = val.astype(BF16)
        return carry

    lax.fori_loop(0, nblk, block, 0)


def _attention(q, kk, vv, sink):
    b, tot = q.shape[0], q.shape[1]
    seq = tot - CTX
    whole = lambda w: pl.BlockSpec((1, tot, w), lambda bb: (bb, 0, 0))
    return pl.pallas_call(
        _attn_kernel,
        grid=(b,),
        in_specs=[pl.BlockSpec(memory_space=pltpu.SMEM), whole(ATTN_WIDTH), whole(2 * LANES), whole(2 * LANES)],
        out_specs=pl.BlockSpec((1, seq, ATTN_WIDTH), lambda bb: (bb, 0, 0)),
        out_shape=jax.ShapeDtypeStruct((b, seq, ATTN_WIDTH), BF16),
        scratch_shapes=[pltpu.VMEM((2 * KV_HEADS, CTX, LANES), BF16), pltpu.VMEM((2 * KV_HEADS, CTX, LANES), BF16)],
        compiler_params=_params("arbitrary"),
        name="attn",
    )(sink, q, kk, vv)


def _ssd_selectors():
    nh2 = 2 * SSM_HEADS
    npairs = SSM_HEADS // 2
    k = np.arange(LANES)
    part, q = k // (2 * nh2), k % (2 * nh2)
    live = part < 3
    col = np.arange(nh2 * LANES)
    tile, lane = col // LANES, col % LANES
    kind, blk = tile // (nh2 // 2), tile % (nh2 // 2)
    src = kind * nh2 + (blk // npairs) * SSM_HEADS + 2 * (blk % npairs) + (lane >= SSM_HEAD_DIM)
    pairsel = live[:, None] & (q[:, None] == src[None, :])
    segind = live[:, None] & (q[:, None] < nh2) & (q[:, None] == tile[None, :])
    return jnp.asarray(pairsel, BF16), jnp.asarray(segind, BF16)


def _ssd_kernel(x_ref, dt_ref, z_ref, cw_ref, cb_ref, dtb_ref, alog_ref, dsk_ref, g_ref, tri_ref, shift_ref,
                pairsel_ref, segind_ref,
                y_ref, hf_scr, hb_scr, yacc_scr, stb_scr, decb_scr, eb_scr, cbuf_scr, segrhs_scr, cols3_scr, yt_scr):
    tot = x_ref.shape[1]
    nchunks = tot // CHUNK
    nctx = CTX // CHUNK
    npairs = SSM_HEADS // 2
    pairs_per_group = npairs // SSM_GROUPS
    group_cols = pairs_per_group * LANES

    hf_scr[...] = jnp.zeros_like(hf_scr)
    hb_scr[...] = jnp.zeros_like(hb_scr)
    segrhs_scr[...] = segind_ref[...]

    nh2 = 2 * SSM_HEADS

    def prep(chunk):
        c0 = pl.multiple_of(chunk * CHUNK, CHUNK)
        dtv = _softplus(dt_ref[0, pl.ds(c0, CHUNK), :].T[0:nh2, :] + dtb_ref[...])
        a = dtv * (-jnp.exp(alog_ref[...]) * LOG2E)

        def split3(v):
            hi = v.astype(BF16)
            r1 = v - hi.astype(F32)
            mid = r1.astype(BF16)
            return hi, mid, (r1 - mid.astype(F32)).astype(BF16)

        parts = jnp.concatenate(split3(a), axis=0)
        fold = lambda p: p[0:nh2] + p[nh2:2 * nh2] + p[2 * nh2:3 * nh2]
        prefix = fold(_dot(parts, tri_ref[0]))
        suffix = fold(_dot(parts, tri_ref[1]))
        head_row = lax.broadcasted_iota(jnp.int32, (nh2, CHUNK), 0)
        ac_t = jnp.where(head_row < SSM_HEADS, prefix, suffix)

        v32 = [p.astype(F32) for p in split3(jnp.concatenate([ac_t, dtv], axis=0))]
        ones3 = (lax.broadcasted_iota(jnp.int32, (CHUNK - 6 * nh2, CHUNK), 0) < 3).astype(F32)
        cols3_scr[...] = jnp.concatenate(v32 + [ones3], axis=0).T.astype(BF16)
        for q in range(nh2):
            neg = [-p[q:q + 1, :] for p in v32]
            tile = jnp.concatenate(neg + [jnp.zeros((BF16_SUBLANES - 3, CHUNK), F32)], axis=0)
            segrhs_scr[6 * nh2:6 * nh2 + BF16_SUBLANES, q * CHUNK:(q + 1) * CHUNK] = tile.astype(BF16)

    prep(0)

    def chunk_pass(chunk, carry):
        lc = jnp.maximum(chunk - nctx, 0)
        r0 = pl.multiple_of(lc * CHUNK, CHUNK)
        c0 = pl.multiple_of(chunk * CHUNK, CHUNK)
        p0 = pl.multiple_of(jnp.maximum(chunk * CHUNK - HALO, 0), HALO)
        n0 = pl.multiple_of(jnp.minimum(chunk * CHUNK + CHUNK, tot - HALO), HALO)

        cols3 = cols3_scr[...]
        x16 = x_ref[0, pl.ds(c0, CHUNK), :]
        has_prev = (chunk != 0) & (chunk != nctx)
        has_next = (chunk != nctx - 1) & (chunk != nchunks - 1)
        halo_p = x_ref[0, pl.ds(p0, HALO), :]
        halo_n = x_ref[0, pl.ds(n0, HALO), :]
        xext = jnp.concatenate([jnp.where(has_prev, halo_p, jnp.zeros_like(halo_p)), x16,
                                jnp.where(has_next, halo_n, jnp.zeros_like(halo_n))], axis=0)
        shifted = _dot(shift_ref[...], xext)
        pairs = _dot(cols3, pairsel_ref[...])
        seg_all = _dot(cols3, segrhs_scr[...])

        u = (cw_ref[0:1, :] * shifted[0:CHUNK] + cw_ref[1:2, :] * x16.astype(F32)
             + cw_ref[2:3, :] * shifted[CHUNK:2 * CHUNK] + cb_ref[...])
        u = u * _sigmoid(u)
        xs = u[:, 0:SSM_WIDTH]
        bmat = u[:, SSM_WIDTH:SSM_WIDTH + SSM_GROUPS * D_STATE]
        cmat = u[:, SSM_WIDTH + SSM_GROUPS * D_STATE:XBC_WIDTH]
        cb16 = [cmat[:, g * D_STATE:(g + 1) * D_STATE].astype(BF16) for g in range(SSM_GROUPS)]
        bb16 = [bmat[:, g * D_STATE:(g + 1) * D_STATE].astype(BF16) for g in range(SSM_GROUPS)]
        bt16 = [bmat[:, g * D_STATE:(g + 1) * D_STATE].T.astype(BF16) for g in range(SSM_GROUPS)]
        cbuf_scr[lc] = jnp.concatenate(cb16, axis=1)

        hf_start = [hf_scr[:, g * group_cols:(g + 1) * group_cols] for g in range(SSM_GROUPS)]
        gmat = [_dot_nt(cb16[g], bb16[g]) for g in range(SSM_GROUPS)]
        yoff = [_dot(cb16[g], hf_start[g].astype(BF16)) for g in range(SSM_GROUPS)]

        half0 = lax.broadcasted_iota(jnp.int32, (CHUNK, LANES), 1) < SSM_HEAD_DIM
        ii = lax.broadcasted_iota(jnp.int32, (CHUNK, CHUNK), 0)
        jj = lax.broadcasted_iota(jnp.int32, (CHUNK, CHUNK), 1)
        causal = [jj <= ii, jj >= ii]
        last = [CHUNK - 1, 0]

        arg_pair, tot_pair, xdt_pair, xdtd_pair = [], [], [], []
        for d in range(2):
            args, tots, xdts, xdtds = [], [], [], []
            for m in range(npairs):
                blk = d * npairs + m
                arg = pairs[:, blk * LANES:(blk + 1) * LANES]
                dtp = pairs[:, (nh2 // 2 + blk) * LANES:(nh2 // 2 + blk + 1) * LANES]
                total = arg[last[d]:last[d] + 1, :]
                xdt = xs[:, m * LANES:(m + 1) * LANES] * dtp
                args.append(arg)
                tots.append(total)
                xdts.append(xdt)
                xdtds.append(xdt * jnp.exp2(total - arg))
            arg_pair.append(args)
            tot_pair.append(tots)
            xdt_pair.append(xdts)
            xdtd_pair.append(xdtds)

        for g in range(SSM_GROUPS):
            pr = range(g * pairs_per_group, (g + 1) * pairs_per_group)
            gs = slice(g * group_cols, (g + 1) * group_cols)
            st_f = _dot(bt16[g], jnp.concatenate([xdtd_pair[0][m] for m in pr], axis=1).astype(BF16))
            st_b = _dot(bt16[g], jnp.concatenate([xdtd_pair[1][m] for m in pr], axis=1).astype(BF16))
            dec_f = jnp.concatenate([jnp.exp2(tot_pair[0][m]) for m in pr], axis=1)
            dec_b = jnp.concatenate([jnp.exp2(tot_pair[1][m]) for m in pr], axis=1)
            hf_scr[:, gs] = hf_start[g] * dec_f + st_f
            stb_scr[chunk, :, gs] = st_b
            decb_scr[chunk, :, gs] = jnp.broadcast_to(dec_b, (8, group_cols))

        prep(jnp.minimum(chunk + 1, nchunks - 1))
        operands = {}

        def intra_operands(m):
            g = m // pairs_per_group
            smats, rhs = [], []
            for d in range(2):
                for hh in range(2):
                    r = d * SSM_HEADS + 2 * m + hh
                    seg = seg_all[:, r * CHUNK:(r + 1) * CHUNK]
                    smats.append(jnp.where(causal[d], gmat[g] * jnp.exp2(seg), 0.0).astype(BF16))
                xd = xdt_pair[d][m]
                rhs += [jnp.where(half0, xd, 0.0), jnp.where(half0, 0.0, xd)]
            operands[m] = (jnp.concatenate(smats, axis=1), jnp.concatenate(rhs, axis=0).astype(BF16))

        def intra_matmul(m):
            g, mm = m // pairs_per_group, m % pairs_per_group
            ms = slice(m * LANES, (m + 1) * LANES)
            ydiag = _dot(*operands[m])
            yacc_scr[pl.ds(r0, CHUNK), ms] = (ydiag + yoff[g][:, mm * LANES:(mm + 1) * LANES]
                                              * jnp.exp2(arg_pair[0][m]) + dsk_ref[:, ms] * xs[:, ms])
            eb_scr[lc, :, ms] = jnp.exp2(arg_pair[1][m])

        intra_operands(0)
        for m in range(1, npairs):
            intra_operands(m)
            intra_matmul(m - 1)
        intra_matmul(npairs - 1)
        return carry

    lax.fori_loop(0, nchunks, chunk_pass, 0)

    yt_scr[...] = jnp.zeros_like(yt_scr)
    order = lambda i: jnp.where(i < nctx, nctx - 1 - i, nchunks + nctx - 1 - i)

    def carry_pass(i, carry):
        chunk = order(jnp.minimum(i, nchunks - 1))
        lc = jnp.maximum(chunk - nctx, 0)
        r0 = pl.multiple_of(lc * CHUNK, CHUNK)
        hb = hb_scr[...]
        cb = cbuf_scr[lc]
        yoff = jnp.concatenate(
            [_dot(cb[:, g * D_STATE:(g + 1) * D_STATE], hb[:, g * group_cols:(g + 1) * group_cols].astype(BF16))
             for g in range(SSM_GROUPS)], axis=1)

        prev = order(jnp.maximum(i - 1, 0))
        pr0 = pl.multiple_of(jnp.maximum(prev - nctx, 0) * CHUNK, CHUNK)
        zf = z_ref[0, pl.ds(pl.multiple_of(prev * CHUNK, CHUNK), CHUNK), :].astype(F32)
        yt = yt_scr[...] * (zf * _sigmoid(zf))
        gw = SSM_WIDTH // SSM_GROUPS
        for g in range(SSM_GROUPS):
            yg = yt[:, g * gw:(g + 1) * gw]
            ms = jnp.mean(yg * yg, axis=-1, keepdims=True)
            y_ref[0, pl.ds(pr0, CHUNK), g * gw:(g + 1) * gw] = (yg * lax.rsqrt(ms + EPS)
                                                                * g_ref[:, g * gw:(g + 1) * gw]).astype(BF16)

        yt_scr[...] = yacc_scr[pl.ds(r0, CHUNK), :] + yoff * eb_scr[lc]
        hb_scr[...] = hb * decb_scr[chunk, 0:1, :] + stb_scr[chunk]
        return carry

    lax.fori_loop(0, nchunks + 1, carry_pass, 0)


def _ssd(xbc, dt, z, conv_w, conv_b, dtb16, alog16, dsk, g_ssm):
    b, tot = xbc.shape[0], xbc.shape[1]
    seq = tot - CTX
    nchunks = tot // CHUNK
    nctx = CTX // CHUNK
    nlat = nchunks - nctx
    const = lambda *shape: pl.BlockSpec(shape, lambda bb: (0,) * len(shape))
    whole = lambda rows, w: pl.BlockSpec((1, rows, w), lambda bb: (bb, 0, 0))
    jdx = np.arange(CHUNK)
    tri = jnp.asarray(np.stack([jdx[:, None] <= jdx[None, :], jdx[:, None] >= jdx[None, :]]), BF16)
    kdx = np.arange(CHUNK + 2 * HALO)
    shift = jnp.asarray(np.concatenate([kdx[None, :] == jdx[:, None] + HALO - 1,
                                        kdx[None, :] == jdx[:, None] + HALO + 1], axis=0), BF16)
    return pl.pallas_call(
        _ssd_kernel,
        grid=(b,),
        in_specs=[whole(tot, XBC_WIDTH), whole(tot, LANES), whole(tot, SSM_WIDTH),
                  const(3, XBC_WIDTH), const(1, XBC_WIDTH), const(2 * SSM_HEADS, CHUNK), const(2 * SSM_HEADS, CHUNK),
                  const(1, SSM_WIDTH), const(1, SSM_WIDTH), const(2, CHUNK, CHUNK),
                  const(2 * CHUNK, CHUNK + 2 * HALO),
                  const(LANES, 2 * SSM_HEADS * LANES), const(LANES, 2 * SSM_HEADS * LANES)],
        out_specs=whole(seq, SSM_WIDTH),
        out_shape=jax.ShapeDtypeStruct((b, seq, SSM_WIDTH), BF16),
        scratch_shapes=[pltpu.VMEM((D_STATE, SSM_WIDTH), F32),
                        pltpu.VMEM((D_STATE, SSM_WIDTH), F32),
                        pltpu.VMEM((seq, SSM_WIDTH), F32),
                        pltpu.VMEM((nchunks, D_STATE, SSM_WIDTH), F32),
                        pltpu.VMEM((nchunks, 8, SSM_WIDTH), F32),
                        pltpu.VMEM((nlat, CHUNK, SSM_WIDTH), F32),
                        pltpu.VMEM((nlat, CHUNK, SSM_GROUPS * D_STATE), BF16),
                        pltpu.VMEM((LANES, 2 * SSM_HEADS * LANES), BF16),
                        pltpu.VMEM((CHUNK, LANES), BF16),
                        pltpu.VMEM((CHUNK, SSM_WIDTH), F32)],
        compiler_params=_params("arbitrary"),
        name="ssd",
    )(xbc, dt, z, conv_w, conv_b, dtb16, alog16, dsk, g_ssm, tri, shift, *_ssd_selectors())


def _mixffn_kernel(am_ref, ap_ref, an_ref, ym_ref, yp_ref, yn_ref, xm_ref, xp_ref, xn_ref,
                   wo_ref, wup_ref, cw_ref, cb_ref, wd_ref, mod_ref, g_ref, o_ref):
    i = pl.program_id(1)
    tm = am_ref.shape[1]
    nrows = tm + 2 * HALO
    ext = lambda p, m, n: jnp.concatenate([p[0], m[0], n[0]], axis=0)
    a_ext = ext(ap_ref, am_ref, an_ref)
    y_ext = ext(yp_ref, ym_ref, yn_ref)
    x_ext = ext(xp_ref, xm_ref, xn_ref)
    mod = mod_ref[0]
    lo = jnp.where(i > 0, 0, HALO)
    hi = jnp.where(i < pl.num_programs(1) - 1, nrows, nrows - HALO)
    half = nrows // 2
    halves = [slice(k * half, (k + 1) * half) for k in range(2)]
    proj = [_dot(a_ext[rs], wo_ref[0:ATTN_WIDTH, :]) + _dot(y_ext[rs], wo_ref[ATTN_WIDTH:D_MODEL, :])
            for rs in halves]
    xparts, lparts = [], []
    for k, rs in enumerate(halves):
        xk = x_ext[rs] + mod[:, 2 * D_MODEL:3 * D_MODEL] * proj[k]
        ms = jnp.mean(xk * xk, axis=-1, keepdims=True)
        hn = xk * lax.rsqrt(ms + EPS) * g_ref[...]
        h2 = hn * (1.0 + mod[:, 4 * D_MODEL:5 * D_MODEL]) + mod[:, 3 * D_MODEL:4 * D_MODEL]
        row = lax.broadcasted_iota(jnp.int32, (half, 1), 0) + k * half
        lparts.append(jnp.where(row >= lo, jnp.where(row < hi, h2, 0.0), 0.0).astype(BF16))
        xparts.append(xk)
    xnew = jnp.concatenate(xparts, axis=0)
    lhs = jnp.concatenate(lparts, axis=0)

    chunks = [(c0, min(c0 + TF_FFN, D_FF)) for c0 in range(0, D_FF, TF_FFN)]
    ups, parts = {}, []

    def up_proj(k):
        c0, c1 = chunks[k]
        ups[k] = (_dot(lhs, wup_ref[:, c0:c1]), _dot(lhs, wup_ref[:, D_FF + c0:D_FF + c1]))

    def conv(uu, c0, c1):
        up = pltpu.roll(uu, 1, 0)[HALO:HALO + tm]
        dn = pltpu.roll(uu, nrows - 1, 0)[HALO:HALO + tm]
        return (cw_ref[0:1, c0:c1] * up + cw_ref[1:2, c0:c1] * uu[HALO:HALO + tm]
                + cw_ref[2:3, c0:c1] * dn + cb_ref[:, c0:c1])

    def gate_down(k):
        c0, c1 = chunks[k]
        ua = conv(ups[k][0], c0, c1)
        ug = conv(ups[k][1], D_FF + c0, D_FF + c1)
        act = (ua * (ug * _sigmoid(ug))).astype(BF16)
        parts.append(_dot(act, wd_ref[c0:c1, :]))

    up_proj(0)
    for k in range(1, len(chunks)):
        up_proj(k)
        gate_down(k - 1)
    gate_down(len(chunks) - 1)
    acc = parts[0]
    for part in parts[1:]:
        acc = acc + part
    o_ref[0] = xnew[HALO:HALO + tm] + mod[:, 5 * D_MODEL:6 * D_MODEL] * acc


def _mixffn(attn, y, x, w_out16, w_up16, conv_w, conv_b, w_down16, mod3, g_ffn):
    b, seq = x.shape[0], x.shape[1]
    ni = seq // TM_FFN
    hpt = TM_FFN // HALO
    nhalo = seq // HALO
    main = lambda w: pl.BlockSpec((1, TM_FFN, w), lambda bb, i: (bb, i, 0))
    prev = lambda w: pl.BlockSpec((1, HALO, w), lambda bb, i: (bb, jnp.maximum(i * hpt - 1, 0), 0))
    nxt = lambda w: pl.BlockSpec((1, HALO, w), lambda bb, i: (bb, jnp.minimum(i * hpt + hpt, nhalo - 1), 0))
    resident = lambda *shape: pl.BlockSpec(shape, lambda bb, i: (0,) * len(shape), pipeline_mode=pl.Buffered(1))
    trio = lambda w: [main(w), prev(w), nxt(w)]
    return pl.pallas_call(
        _mixffn_kernel,
        grid=(b, ni),
        in_specs=trio(ATTN_WIDTH) + trio(SSM_WIDTH) + trio(D_MODEL) + [
            resident(D_MODEL, D_MODEL), resident(D_MODEL, 2 * D_FF), resident(3, 2 * D_FF), resident(1, 2 * D_FF),
            resident(D_FF, D_MODEL),
            pl.BlockSpec((1, 1, 6 * D_MODEL), lambda bb, i: (bb, 0, 0)),
            resident(1, D_MODEL)],
        out_specs=main(D_MODEL),
        out_shape=jax.ShapeDtypeStruct((b, seq, D_MODEL), F32),
        compiler_params=_params("arbitrary", "arbitrary"),
        name="mixffn",
    )(attn, attn, attn, y, y, y, x, x, x, w_out16, w_up16, conv_w, conv_b.reshape(1, -1), w_down16, mod3,
      g_ffn.reshape(1, -1))


def _layer(x, c, ctx, c_ctx, w_mod, b_mod, g_mix, w_in, g_q, g_k, sink, ssm_conv_w, ssm_conv_b,
           a_log, dt_bias, d_skip, g_ssm, w_out, g_ffn, w_up, ffn_conv_w, ffn_conv_b, w_down):
    mod3 = _modulation(c, c_ctx, w_mod, b_mod)

    q, kk, vv, z, xbc, dt = _inproj(x, ctx, mod3, g_mix, w_in, g_q, g_k)
    attn = _attention(q, kk, vv, sink)

    head_rows = lambda t: jnp.broadcast_to(t.reshape(2 * SSM_HEADS, 1), (2 * SSM_HEADS, CHUNK))
    y = _ssd(xbc, dt, z, ssm_conv_w, ssm_conv_b.reshape(1, -1), head_rows(dt_bias), head_rows(a_log),
             jnp.repeat(d_skip, SSM_HEAD_DIM).reshape(1, -1), g_ssm.reshape(1, -1))

    return _mixffn(attn, y, x, w_out.astype(BF16), w_up.astype(BF16), ffn_conv_w, ffn_conv_b, w_down.astype(BF16),
                   mod3, g_ffn)


def kernel(x, c, ctx, c_ctx, w_mod, b_mod, g_mix, w_in, g_q, g_k, sink, ssm_conv_w, ssm_conv_b,
           a_log, dt_bias, d_skip, g_ssm, w_out, g_ffn, w_up, ffn_conv_w, ffn_conv_b, w_down):
    assert w_mod.shape[0] == 1, "single-layer block"
    first = lambda t: t.reshape(t.shape[1:])
    return _layer(x, c, ctx, c_ctx, first(w_mod), first(b_mod), first(g_mix), first(w_in), first(g_q),
                  first(g_k), first(sink), first(ssm_conv_w), first(ssm_conv_b), first(a_log),
                  first(dt_bias), first(d_skip), first(g_ssm), first(w_out), first(g_ffn), first(w_up),
                  first(ffn_conv_w), first(ffn_conv_b), first(w_down))
```

```python
import jax
import jax.numpy as jnp
import numpy as np
from jax import lax
from jax.experimental import pallas as pl
from jax.experimental.pallas import tpu as pltpu

F32 = jnp.float32
BF16 = jnp.bfloat16

D_MODEL = 1024
CTX = 256
GRID_W = 64
HEAD_DIM = 64
ATTN_HEADS = 8
KV_HEADS = 2
ATTN_WIDTH = ATTN_HEADS * HEAD_DIM
KV_WIDTH = KV_HEADS * HEAD_DIM
WINDOW = 128
BLOCK = 128
ROPE_BASE = 10000.0
SSM_WIDTH = D_MODEL - ATTN_WIDTH
SSM_HEAD_DIM = 64
SSM_HEADS = SSM_WIDTH // SSM_HEAD_DIM
SSM_GROUPS = 2
D_STATE = 128
XBC_WIDTH = SSM_WIDTH + 2 * SSM_GROUPS * D_STATE
CHUNK = 128
D_FF = 2816
IN_COLS = ATTN_WIDTH + 2 * KV_WIDTH + SSM_WIDTH + XBC_WIDTH + 2 * SSM_HEADS
EPS = 1e-6
NEG_INF = -1e30
LOG2E = 1.4426950408889634

LANES = 128
BF16_SUBLANES = 16
IN_COLS_PAD = 19 * LANES
COL_Q, COL_K, COL_V = 0, ATTN_WIDTH, ATTN_WIDTH + KV_WIDTH
COL_Z = ATTN_WIDTH + 2 * KV_WIDTH
COL_XBC = COL_Z + SSM_WIDTH
COL_DT = COL_XBC + XBC_WIDTH
VMEM_LIMIT = 56 * 1024 * 1024

TM_IN = 3 * CTX
TM_FFN = 512
TF_FFN = 1024
HALO = BF16_SUBLANES


def _params(*sem):
    return pltpu.CompilerParams(dimension_semantics=sem, vmem_limit_bytes=VMEM_LIMIT)


def _sigmoid(v):
    return 1.0 / (1.0 + jnp.exp2(v * (-LOG2E)))


def _softplus(v):
    return jnp.maximum(v, 0.0) + jnp.log1p(jnp.exp(-jnp.abs(v)))


def _dot(a, b):
    return jnp.dot(a, b, preferred_element_type=F32)


def _dot_nt(a, b):
    return lax.dot_general(a, b, (((1,), (1,)), ((), ())), preferred_element_type=F32)


def _mod_kernel(c_ref, cctx_ref, w_ref, b_ref, o_ref):
    rows = o_ref.shape[0]
    b = c_ref.shape[0]
    cv = jnp.concatenate([c_ref[...], cctx_ref[...], jnp.zeros((rows - b - 1, D_MODEL), F32)], axis=0)
    s = (cv * _sigmoid(cv)).astype(BF16)
    res = _dot(s, w_ref[...].astype(BF16)) + b_ref[...]
    for r in range(b + 1):
        o_ref[r] = res[r:r + 1, :]
    o_ref[b + 1:rows] = jnp.zeros((rows - b - 1, 1, res.shape[1]), F32)


def _modulation(c, c_ctx, w_mod, b_mod):
    b = c.shape[0]
    rows = 8 * ((b + 1 + 7) // 8)
    tn = 1024
    return pl.pallas_call(
        _mod_kernel,
        grid=(6 * D_MODEL // tn,),
        in_specs=[pl.BlockSpec((b, D_MODEL), lambda j: (0, 0)),
                  pl.BlockSpec((1, D_MODEL), lambda j: (0, 0)),
                  pl.BlockSpec((D_MODEL, tn), lambda j: (0, j)),
                  pl.BlockSpec((1, tn), lambda j: (0, j))],
        out_specs=pl.BlockSpec((rows, 1, tn), lambda j: (0, 0, j)),
        out_shape=jax.ShapeDtypeStruct((rows, 1, 6 * D_MODEL), F32),
        compiler_params=_params("arbitrary"),
        name="mod",
    )(c, c_ctx.reshape(1, -1), w_mod, b_mod.reshape(1, -1))


def _inproj_kernel(xa_ref, xb_ref, xc_ref, ctx_ref, modb_ref, modc_ref, gmix_ref, w_ref, gq_ref, gk_ref,
                   cos_ref, sina_ref, sinb_ref, bd_ref,
                   q_ref, k_ref, v_ref, z_ref, xbc_ref, dt_ref, w16_scr):
    first = pl.program_id(1) == 0

    @pl.when(first & (pl.program_id(0) == 0))
    def _():
        w16_scr[:, 0:IN_COLS] = w_ref[...].astype(BF16)
        w16_scr[:, IN_COLS:IN_COLS_PAD] = jnp.zeros((D_MODEL, IN_COLS_PAD - IN_COLS), BF16)

    def normed(xin, mod):
        ms = jnp.mean(xin * xin, axis=-1, keepdims=True)
        hn = xin * lax.rsqrt(ms + EPS) * gmix_ref[...]
        return (hn * (1.0 + mod[:, D_MODEL:2 * D_MODEL]) + mod[:, 0:D_MODEL]).astype(BF16)

    modb = modb_ref[0]
    h = jnp.concatenate([normed(jnp.where(first, ctx_ref[0], xa_ref[0]), jnp.where(first, modc_ref[0], modb)),
                         normed(xb_ref[0], modb), normed(xc_ref[0], modb)], axis=0)

    cos = cos_ref[...]
    sina = sina_ref[...]
    sinb = sinb_ref[...]
    bd = bd_ref[...]

    def norm_rope(pf, gain):
        ss = pf * pf
        hi = ss.astype(BF16)
        lo = (ss - hi.astype(F32)).astype(BF16)
        sums = _dot(jnp.concatenate([hi, lo], axis=1), bd)
        y = pf * lax.rsqrt(sums * (1.0 / HEAD_DIM) + EPS) * gain
        return y * cos + pltpu.roll(y, LANES - 16, 1) * sina + pltpu.roll(y, 16, 1) * sinb

    def q_chunks(lo, hi):
        for cidx in range(lo, hi):
            cs = slice(cidx * LANES, (cidx + 1) * LANES)
            q_ref[0, :, cs] = norm_rope(qf[:, cs], gq_ref[...]).astype(BF16)

    qf = _dot(h, w16_scr[:,COL_Q:COL_Q + ATTN_WIDTH])
    kvf = _dot(h, w16_scr[:,COL_K:COL_K + 2 * KV_WIDTH])
    zf = _dot(h, w16_scr[:,COL_Z:COL_Z + SSM_WIDTH])
    q_chunks(0, 2)
    xf = _dot(h, w16_scr[:,COL_XBC:COL_XBC + XBC_WIDTH])
    q_chunks(2, ATTN_WIDTH // LANES)
    dtf = _dot(h, w16_scr[:,COL_DT:COL_DT + LANES])
    kf = norm_rope(kvf[:, 0:KV_WIDTH], gk_ref[...])
    k_ref[0, :, 0:LANES] = kf.astype(BF16)
    k_ref[0, :, LANES:2 * LANES] = pltpu.roll(kf, HEAD_DIM, 1).astype(BF16)
    vf = kvf[:, KV_WIDTH:2 * KV_WIDTH]
    v_ref[0, :, 0:LANES] = vf.astype(BF16)
    v_ref[0, :, LANES:2 * LANES] = pltpu.roll(vf, HEAD_DIM, 1).astype(BF16)
    z_ref[0] = zf.astype(BF16)
    xbc_ref[0] = xf.astype(BF16)
    dt_ref[0] = dtf


def _rope_tables(seq):
    rows = seq // GRID_W
    pos_r = np.repeat(np.arange(rows), GRID_W).astype(np.float32)
    pos_c = np.tile(np.arange(GRID_W), rows).astype(np.float32)
    quarter = HEAD_DIM // 4
    freqs = (ROPE_BASE ** (-np.arange(quarter, dtype=np.float32) / quarter)).astype(np.float32)
    ang_r = pos_r[:, None] * freqs[None, :]
    ang_c = pos_c[:, None] * freqs[None, :]
    cr, sr, cc, sc = np.cos(ang_r), np.sin(ang_r), np.cos(ang_c), np.sin(ang_c)
    zero = np.zeros_like(sr)
    cos = np.concatenate([cr, cr, cc, cc], axis=-1)
    sina = np.concatenate([-sr, zero, -sc, zero], axis=-1)
    sinb = np.concatenate([zero, sr, zero, sc], axis=-1)
    reps = LANES // HEAD_DIM
    pad = lambda t, v: np.concatenate([np.full((CTX, LANES), v, np.float32), np.tile(t, (1, reps))], axis=0)
    return tuple(jnp.asarray(pad(t, v), F32) for t, v in ((cos, 1.0), (sina, 0.0), (sinb, 0.0)))


def _inproj(x, ctx, mod3, g_mix, w_in, g_q, g_k):
    b, seq = x.shape[0], x.shape[1]
    tot = CTX + seq
    cos, sina, sinb = _rope_tables(seq)
    reps = LANES // HEAD_DIM
    gq = (jnp.tile(g_q, reps) * (HEAD_DIM ** -0.5 * LOG2E)).reshape(1, LANES)
    gk = jnp.tile(g_k, reps).reshape(1, LANES)
    lane = np.arange(LANES)
    bd = lane[:, None] // HEAD_DIM == lane[None, :] // HEAD_DIM
    bd = jnp.asarray(np.concatenate([bd, bd], axis=0), BF16)
    nsteps = tot // TM_IN
    sub = TM_IN // CTX
    const = lambda *shape: pl.BlockSpec(shape, lambda bb, s: (0,) * len(shape))
    tab = pl.BlockSpec((TM_IN, LANES), lambda bb, s: (s, 0))
    out = lambda w: pl.BlockSpec((1, TM_IN, w), lambda bb, s: (bb, s, 0))
    shp = lambda w, dt: jax.ShapeDtypeStruct((b, tot, w), dt)
    xsub = lambda j: pl.BlockSpec((1, CTX, D_MODEL), lambda bb, s: (bb, jnp.maximum(sub * s + j - 1, 0), 0))
    return pl.pallas_call(
        _inproj_kernel,
        grid=(b, nsteps),
        in_specs=[xsub(0), xsub(1), xsub(2),
                  pl.BlockSpec((1, CTX, D_MODEL), lambda bb, s: (bb, 0, 0)),
                  pl.BlockSpec((1, 1, 6 * D_MODEL), lambda bb, s: (bb, 0, 0)),
                  pl.BlockSpec((1, 1, 6 * D_MODEL), lambda bb, s: (b, 0, 0)),
                  const(1, D_MODEL),
                  pl.BlockSpec((D_MODEL, IN_COLS), lambda bb, s: (0, 0), pipeline_mode=pl.Buffered(1)),
                  const(1, LANES), const(1, LANES),
                  tab, tab, tab, const(2 * LANES, LANES)],
        out_specs=[out(ATTN_WIDTH), out(2 * LANES), out(2 * LANES), out(SSM_WIDTH), out(XBC_WIDTH), out(LANES)],
        out_shape=[shp(ATTN_WIDTH, BF16), shp(2 * LANES, BF16), shp(2 * LANES, BF16),
                   shp(SSM_WIDTH, BF16), shp(XBC_WIDTH, BF16), shp(LANES, F32)],
        scratch_shapes=[pltpu.VMEM((D_MODEL, IN_COLS_PAD), BF16)],
        compiler_params=_params("arbitrary", "arbitrary"),
        name="inproj",
    )(x, x, x, ctx, mod3, mod3, g_mix.reshape(1, -1), w_in, gq, gk, cos, sina, sinb, bd)


def _place_kv_head(t, kh, par):
    src = 0 if par == kh else 1
    lane = lax.broadcasted_iota(jnp.int32, (t.shape[0], LANES), 1)
    half = (lane >= HEAD_DIM) if par else (lane < HEAD_DIM)
    tile = t[:, src * LANES:(src + 1) * LANES]
    return jnp.where(half, tile, jnp.zeros_like(tile))


def _attn_kernel(sink_ref, q_ref, k_ref, v_ref, o_ref, kx_scr, vx_scr):
    seq = o_ref.shape[1]
    nblk = seq // BLOCK
    qi = lax.broadcasted_iota(jnp.int32, (BLOCK, BLOCK), 0)
    kj = lax.broadcasted_iota(jnp.int32, (BLOCK, BLOCK), 1)
    zero = jnp.zeros((BLOCK, BLOCK), F32)
    tri_prev = jnp.where(kj >= qi, zero, NEG_INF)
    tri_next = jnp.where(kj <= qi, zero, NEG_INF)
    combos = [(kh, par) for kh in range(KV_HEADS) for par in range(2)]
    for idx, (kh, par) in enumerate(combos):
        kx_scr[idx] = _place_kv_head(k_ref[0, 0:CTX, :], kh, par)
        vx_scr[idx] = _place_kv_head(v_ref[0, 0:CTX, :], kh, par)

    def block(nb, carry):
        r0 = pl.multiple_of(CTX + nb * BLOCK, BLOCK)
        rn = pl.multiple_of(CTX + jnp.minimum(nb + 1, nblk - 1) * BLOCK, BLOCK)
        rp = pl.multiple_of(CTX - BLOCK + nb * BLOCK, BLOCK)
        kloc = jnp.concatenate([k_ref[0, pl.ds(rp, 2 * BLOCK), :], k_ref[0, pl.ds(rn, BLOCK), :]], axis=0)
        vloc = jnp.concatenate([v_ref[0, pl.ds(rp, 2 * BLOCK), :], v_ref[0, pl.ds(rn, BLOCK), :]], axis=0)
        bias_p = jnp.where(nb > 0, tri_prev, NEG_INF)
        bias_n = jnp.where(nb < nblk - 1, tri_next, NEG_INF)
        q = q_ref[0, pl.ds(r0, BLOCK), :]
        q2s = [jnp.concatenate([q[:, (2 * kh) * LANES:(2 * kh + 1) * LANES],
                                q[:, (2 * kh + 1) * LANES:(2 * kh + 2) * LANES]], axis=0)
               for kh in range(KV_HEADS)]
        sc, ee, oo = {}, {}, {}

        def scores(idx):
            kh, par = combos[idx]
            sc[idx] = (_dot_nt(q2s[kh], _place_kv_head(kloc, kh, par)),
                       _dot_nt(q2s[kh], kx_scr[idx]))

        def softmax(idx):
            kh, par = combos[idx]
            s_loc, s_ctx = sc[idx]
            es, inv = [], []
            for r in range(2):
                head = 4 * kh + par + 2 * r
                rs = slice(r * BLOCK, (r + 1) * BLOCK)
                sv = jnp.concatenate([s_loc[rs, 0:BLOCK] + bias_p, s_loc[rs, BLOCK:2 * BLOCK],
                                      s_loc[rs, 2 * BLOCK:3 * BLOCK] + bias_n, s_ctx[rs]], axis=1)
                sk = sink_ref[head] * LOG2E
                m = jnp.maximum(jnp.max(sv, axis=-1, keepdims=True), sk)
                e = jnp.exp2(sv - m)
                inv.append(1.0 / (jnp.sum(e, axis=-1, keepdims=True) + jnp.exp2(sk - m)))
                es.append(e.astype(BF16))
            ee[idx] = (jnp.concatenate(es, axis=0), inv)

        def pv(idx):
            kh, par = combos[idx]
            e16, inv = ee[idx]
            o2 = _dot(e16[:, 0:3 * BLOCK], _place_kv_head(vloc, kh, par)) + _dot(e16[:, 3 * BLOCK:], vx_scr[idx])
            oo[idx] = [o2[r * BLOCK:(r + 1) * BLOCK] * inv[r] for r in range(2)]

        scores(0); scores(1); softmax(0); scores(2); softmax(1); pv(0)
        scores(3); softmax(2); pv(1); softmax(3); pv(2); pv(3)
        for kh in range(KV_HEADS):
            for r in range(2):
                c = 2 * kh + r
                val = oo[combos.index((kh, 0))][r] + oo[combos.index((kh, 1))][r]
                o_ref[0, pl.ds(pl.multiple_of(nb * BLOCK, BLOCK), BLOCK), c * LANES:(c + 1) * LANES] = val.astype(BF16)
        return carry

    lax.fori_loop(0, nblk, block, 0)


def _attention(q, kk, vv, sink):
    b, tot = q.shape[0], q.shape[1]
    seq = tot - CTX
    whole = lambda w: pl.BlockSpec((1, tot, w), lambda bb: (bb, 0, 0))
    return pl.pallas_call(
        _attn_kernel,
        grid=(b,),
        in_specs=[pl.BlockSpec(memory_space=pltpu.SMEM), whole(ATTN_WIDTH), whole(2 * LANES), whole(2 * LANES)],
        out_specs=pl.BlockSpec((1, seq, ATTN_WIDTH), lambda bb: (bb, 0, 0)),
        out_shape=jax.ShapeDtypeStruct((b, seq, ATTN_WIDTH), BF16),
        scratch_shapes=[pltpu.VMEM((2 * KV_HEADS, CTX, LANES), BF16), pltpu.VMEM((2 * KV_HEADS, CTX, LANES), BF16)],
        compiler_params=_params("arbitrary"),
        name="attn",
    )(sink, q, kk, vv)


def _ssd_selectors():
    nh2 = 2 * SSM_HEADS
    npairs = SSM_HEADS // 2
    k = np.arange(LANES)
    part, q = k // (2 * nh2), k % (2 * nh2)
    live = part < 3
    col = np.arange(nh2 * LANES)
    tile, lane = col // LANES, col % LANES
    kind, blk = tile // (nh2 // 2), tile % (nh2 // 2)
    src = kind * nh2 + (blk // npairs) * SSM_HEADS + 2 * (blk % npairs) + (lane >= SSM_HEAD_DIM)
    pairsel = live[:, None] & (q[:, None] == src[None, :])
    segind = live[:, None] & (q[:, None] < nh2) & (q[:, None] == tile[None, :])
    return jnp.asarray(pairsel, BF16), jnp.asarray(segind, BF16)


def _ssd_kernel(x_ref, dt_ref, z_ref, cw_ref, cb_ref, dtb_ref, alog_ref, dsk_ref, g_ref, tri_ref, shift_ref,
                pairsel_ref, segind_ref,
                y_ref, hf_scr, hb_scr, yacc_scr, stb_scr, decb_scr, eb_scr, cbuf_scr, segrhs_scr, cols3_scr, yt_scr):
    tot = x_ref.shape[1]
    nchunks = tot // CHUNK
    nctx = CTX // CHUNK
    npairs = SSM_HEADS // 2
    pairs_per_group = npairs // SSM_GROUPS
    group_cols = pairs_per_group * LANES

    hf_scr[...] = jnp.zeros_like(hf_scr)
    hb_scr[...] = jnp.zeros_like(hb_scr)
    segrhs_scr[...] = segind_ref[...]

    nh2 = 2 * SSM_HEADS

    def prep(chunk):
        c0 = pl.multiple_of(chunk * CHUNK, CHUNK)
        dtv = _softplus(dt_ref[0, pl.ds(c0, CHUNK), :].T[0:nh2, :] + dtb_ref[...])
        a = dtv * (-jnp.exp(alog_ref[...]) * LOG2E)

        def split3(v):
            hi = v.astype(BF16)
            r1 = v - hi.astype(F32)
            mid = r1.astype(BF16)
            return hi, mid, (r1 - mid.astype(F32)).astype(BF16)

        parts = jnp.concatenate(split3(a), axis=0)
        fold = lambda p: p[0:nh2] + p[nh2:2 * nh2] + p[2 * nh2:3 * nh2]
        prefix = fold(_dot(parts, tri_ref[0]))
        suffix = fold(_dot(parts, tri_ref[1]))
        head_row = lax.broadcasted_iota(jnp.int32, (nh2, CHUNK), 0)
        ac_t = jnp.where(head_row < SSM_HEADS, prefix, suffix)

        v32 = [p.astype(F32) for p in split3(jnp.concatenate([ac_t, dtv], axis=0))]
        ones3 = (lax.broadcasted_iota(jnp.int32, (CHUNK - 6 * nh2, CHUNK), 0) < 3).astype(F32)
        cols3_scr[...] = jnp.concatenate(v32 + [ones3], axis=0).T.astype(BF16)
        for q in range(nh2):
            neg = [-p[q:q + 1, :] for p in v32]
            tile = jnp.concatenate(neg + [jnp.zeros((BF16_SUBLANES - 3, CHUNK), F32)], axis=0)
            segrhs_scr[6 * nh2:6 * nh2 + BF16_SUBLANES, q * CHUNK:(q + 1) * CHUNK] = tile.astype(BF16)

    prep(0)

    def chunk_pass(chunk, carry):
        lc = jnp.maximum(chunk - nctx, 0)
        r0 = pl.multiple_of(lc * CHUNK, CHUNK)
        c0 = pl.multiple_of(chunk * CHUNK, CHUNK)
        p0 = pl.multiple_of(jnp.maximum(chunk * CHUNK - HALO, 0), HALO)
        n0 = pl.multiple_of(jnp.minimum(chunk * CHUNK + CHUNK, tot - HALO), HALO)

        cols3 = cols3_scr[...]
        x16 = x_ref[0, pl.ds(c0, CHUNK), :]
        has_prev = (chunk != 0) & (chunk != nctx)
        has_next = (chunk != nctx - 1) & (chunk != nchunks - 1)
        halo_p = x_ref[0, pl.ds(p0, HALO), :]
        halo_n = x_ref[0, pl.ds(n0, HALO), :]
        xext = jnp.concatenate([jnp.where(has_prev, halo_p, jnp.zeros_like(halo_p)), x16,
                                jnp.where(has_next, halo_n, jnp.zeros_like(halo_n))], axis=0)
        shifted = _dot(shift_ref[...], xext)
        pairs = _dot(cols3, pairsel_ref[...])
        seg_all = _dot(cols3, segrhs_scr[...])

        u = (cw_ref[0:1, :] * shifted[0:CHUNK] + cw_ref[1:2, :] * x16.astype(F32)
             + cw_ref[2:3, :] * shifted[CHUNK:2 * CHUNK] + cb_ref[...])
        u = u * _sigmoid(u)
        xs = u[:, 0:SSM_WIDTH]
        bmat = u[:, SSM_WIDTH:SSM_WIDTH + SSM_GROUPS * D_STATE]
        cmat = u[:, SSM_WIDTH + SSM_GROUPS * D_STATE:XBC_WIDTH]
        cb16 = [cmat[:, g * D_STATE:(g + 1) * D_STATE].astype(BF16) for g in range(SSM_GROUPS)]
        bb16 = [bmat[:, g * D_STATE:(g + 1) * D_STATE].astype(BF16) for g in range(SSM_GROUPS)]
        bt16 = [bmat[:, g * D_STATE:(g + 1) * D_STATE].T.astype(BF16) for g in range(SSM_GROUPS)]
        cbuf_scr[lc] = jnp.concatenate(cb16, axis=1)

        hf_start = [hf_scr[:, g * group_cols:(g + 1) * group_cols] for g in range(SSM_GROUPS)]
        gm2 = _dot_nt(jnp.concatenate(cb16, axis=0), jnp.concatenate(bb16, axis=0))
        gmat = [gm2[g * CHUNK:(g + 1) * CHUNK, g * CHUNK:(g + 1) * CHUNK] for g in range(SSM_GROUPS)]
        yoff = [_dot(cb16[g], hf_start[g].astype(BF16)) for g in range(SSM_GROUPS)]

        half0 = lax.broadcasted_iota(jnp.int32, (CHUNK, LANES), 1) < SSM_HEAD_DIM
        ii = lax.broadcasted_iota(jnp.int32, (CHUNK, CHUNK), 0)
        jj = lax.broadcasted_iota(jnp.int32, (CHUNK, CHUNK), 1)
        causal = [jj <= ii, jj >= ii]
        last = [CHUNK - 1, 0]

        arg_pair, tot_pair, xdt_pair, xdtd_pair = [], [], [], []
        for d in range(2):
            args, tots, xdts, xdtds = [], [], [], []
            for m in range(npairs):
                blk = d * npairs + m
                arg = pairs[:, blk * LANES:(blk + 1) * LANES]
                dtp = pairs[:, (nh2 // 2 + blk) * LANES:(nh2 // 2 + blk + 1) * LANES]
                total = arg[last[d]:last[d] + 1, :]
                xdt = xs[:, m * LANES:(m + 1) * LANES] * dtp
                args.append(arg)
                tots.append(total)
                xdts.append(xdt)
                xdtds.append(xdt * jnp.exp2(total - arg))
            arg_pair.append(args)
            tot_pair.append(tots)
            xdt_pair.append(xdts)
            xdtd_pair.append(xdtds)

        for g in range(SSM_GROUPS):
            pr = range(g * pairs_per_group, (g + 1) * pairs_per_group)
            gs = slice(g * group_cols, (g + 1) * group_cols)
            st_f = _dot(bt16[g], jnp.concatenate([xdtd_pair[0][m] for m in pr], axis=1).astype(BF16))
            st_b = _dot(bt16[g], jnp.concatenate([xdtd_pair[1][m] for m in pr], axis=1).astype(BF16))
            dec_f = jnp.concatenate([jnp.exp2(tot_pair[0][m]) for m in pr], axis=1)
            dec_b = jnp.concatenate([jnp.exp2(tot_pair[1][m]) for m in pr], axis=1)
            hf_scr[:, gs] = hf_start[g] * dec_f + st_f
            stb_scr[chunk, :, gs] = st_b
            decb_scr[chunk, :, gs] = jnp.broadcast_to(dec_b, (8, group_cols))

        prep(jnp.minimum(chunk + 1, nchunks - 1))
        operands = {}

        def intra_operands(m):
            g = m // pairs_per_group
            smats, rhs = [], []
            for d in range(2):
                for hh in range(2):
                    r = d * SSM_HEADS + 2 * m + hh
                    seg = seg_all[:, r * CHUNK:(r + 1) * CHUNK]
                    smats.append(jnp.where(causal[d], gmat[g] * jnp.exp2(seg), 0.0).astype(BF16))
                xd = xdt_pair[d][m]
                rhs += [jnp.where(half0, xd, 0.0), jnp.where(half0, 0.0, xd)]
            operands[m] = (jnp.concatenate(smats, axis=1), jnp.concatenate(rhs, axis=0).astype(BF16))

        def intra_matmul(m0):
            lhs = jnp.concatenate([operands[m0][0], operands[m0 + 1][0]], axis=1)
            zero = jnp.zeros_like(operands[m0][1])
            rhs = jnp.concatenate([jnp.concatenate([operands[m0][1], zero], axis=1),
                                   jnp.concatenate([zero, operands[m0 + 1][1]], axis=1)], axis=0)
            ydiag = _dot(lhs, rhs)
            for k in range(2):
                m = m0 + k
                g, mm = m // pairs_per_group, m % pairs_per_group
                ms = slice(m * LANES, (m + 1) * LANES)
                yacc_scr[pl.ds(r0, CHUNK), ms] = (ydiag[:, k * LANES:(k + 1) * LANES]
                                                  + yoff[g][:, mm * LANES:(mm + 1) * LANES] * jnp.exp2(arg_pair[0][m])
                                                  + dsk_ref[:, ms] * xs[:, ms])
                eb_scr[lc, :, ms] = jnp.exp2(arg_pair[1][m])

        assert npairs == 4
        intra_operands(0)
        intra_operands(1)
        intra_operands(2)
        intra_matmul(0)
        intra_operands(3)
        intra_matmul(2)
        return carry

    lax.fori_loop(0, nchunks, chunk_pass, 0)

    yt_scr[...] = jnp.zeros_like(yt_scr)
    order = lambda i: jnp.where(i < nctx, nctx - 1 - i, nchunks + nctx - 1 - i)

    def carry_pass(i, carry):
        chunk = order(jnp.minimum(i, nchunks - 1))
        lc = jnp.maximum(chunk - nctx, 0)
        r0 = pl.multiple_of(lc * CHUNK, CHUNK)
        hb = hb_scr[...]
        cb = cbuf_scr[lc]
        yoff = jnp.concatenate(
            [_dot(cb[:, g * D_STATE:(g + 1) * D_STATE], hb[:, g * group_cols:(g + 1) * group_cols].astype(BF16))
             for g in range(SSM_GROUPS)], axis=1)

        prev = order(jnp.maximum(i - 1, 0))
        pr0 = pl.multiple_of(jnp.maximum(prev - nctx, 0) * CHUNK, CHUNK)
        zf = z_ref[0, pl.ds(pl.multiple_of(prev * CHUNK, CHUNK), CHUNK), :].astype(F32)
        yt = yt_scr[...] * (zf * _sigmoid(zf))
        gw = SSM_WIDTH // SSM_GROUPS
        for g in range(SSM_GROUPS):
            yg = yt[:, g * gw:(g + 1) * gw]
            ms = jnp.mean(yg * yg, axis=-1, keepdims=True)
            y_ref[0, pl.ds(pr0, CHUNK), g * gw:(g + 1) * gw] = (yg * lax.rsqrt(ms + EPS)
                                                                * g_ref[:, g * gw:(g + 1) * gw]).astype(BF16)

        yt_scr[...] = yacc_scr[pl.ds(r0, CHUNK), :] + yoff * eb_scr[lc]
        hb_scr[...] = hb * decb_scr[chunk, 0:1, :] + stb_scr[chunk]
        return carry

    lax.fori_loop(0, nchunks + 1, carry_pass, 0)


def _ssd(xbc, dt, z, conv_w, conv_b, dtb16, alog16, dsk, g_ssm):
    b, tot = xbc.shape[0], xbc.shape[1]
    seq = tot - CTX
    nchunks = tot // CHUNK
    nctx = CTX // CHUNK
    nlat = nchunks - nctx
    const = lambda *shape: pl.BlockSpec(shape, lambda bb: (0,) * len(shape))
    whole = lambda rows, w: pl.BlockSpec((1, rows, w), lambda bb: (bb, 0, 0))
    jdx = np.arange(CHUNK)
    tri = jnp.asarray(np.stack([jdx[:, None] <= jdx[None, :], jdx[:, None] >= jdx[None, :]]), BF16)
    kdx = np.arange(CHUNK + 2 * HALO)
    shift = jnp.asarray(np.concatenate([kdx[None, :] == jdx[:, None] + HALO - 1,
                                        kdx[None, :] == jdx[:, None] + HALO + 1], axis=0), BF16)
    return pl.pallas_call(
        _ssd_kernel,
        grid=(b,),
        in_specs=[whole(tot, XBC_WIDTH), whole(tot, LANES), whole(tot, SSM_WIDTH),
                  const(3, XBC_WIDTH), const(1, XBC_WIDTH), const(2 * SSM_HEADS, CHUNK), const(2 * SSM_HEADS, CHUNK),
                  const(1, SSM_WIDTH), const(1, SSM_WIDTH), const(2, CHUNK, CHUNK),
                  const(2 * CHUNK, CHUNK + 2 * HALO),
                  const(LANES, 2 * SSM_HEADS * LANES), const(LANES, 2 * SSM_HEADS * LANES)],
        out_specs=whole(seq, SSM_WIDTH),
        out_shape=jax.ShapeDtypeStruct((b, seq, SSM_WIDTH), BF16),
        scratch_shapes=[pltpu.VMEM((D_STATE, SSM_WIDTH), F32),
                        pltpu.VMEM((D_STATE, SSM_WIDTH), F32),
                        pltpu.VMEM((seq, SSM_WIDTH), F32),
                        pltpu.VMEM((nchunks, D_STATE, SSM_WIDTH), F32),
                        pltpu.VMEM((nchunks, 8, SSM_WIDTH), F32),
                        pltpu.VMEM((nlat, CHUNK, SSM_WIDTH), F32),
                        pltpu.VMEM((nlat, CHUNK, SSM_GROUPS * D_STATE), BF16),
                        pltpu.VMEM((LANES, 2 * SSM_HEADS * LANES), BF16),
                        pltpu.VMEM((CHUNK, LANES), BF16),
                        pltpu.VMEM((CHUNK, SSM_WIDTH), F32)],
        compiler_params=_params("arbitrary"),
        name="ssd",
    )(xbc, dt, z, conv_w, conv_b, dtb16, alog16, dsk, g_ssm, tri, shift, *_ssd_selectors())


def _mixffn_kernel(am_ref, ap_ref, an_ref, ym_ref, yp_ref, yn_ref, xm_ref, xp_ref, xn_ref,
                   wo_ref, wup_ref, cw_ref, cb_ref, wd_ref, mod_ref, g_ref, o_ref):
    i = pl.program_id(1)
    tm = am_ref.shape[1]
    nrows = tm + 2 * HALO
    ext = lambda p, m, n: jnp.concatenate([p[0], m[0], n[0]], axis=0)
    a_ext = ext(ap_ref, am_ref, an_ref)
    y_ext = ext(yp_ref, ym_ref, yn_ref)
    x_ext = ext(xp_ref, xm_ref, xn_ref)
    mod = mod_ref[0]
    lo = jnp.where(i > 0, 0, HALO)
    hi = jnp.where(i < pl.num_programs(1) - 1, nrows, nrows - HALO)
    half = nrows // 2
    halves = [slice(k * half, (k + 1) * half) for k in range(2)]
    proj = [_dot(a_ext[rs], wo_ref[0:ATTN_WIDTH, :]) + _dot(y_ext[rs], wo_ref[ATTN_WIDTH:D_MODEL, :])
            for rs in halves]
    xparts, lparts = [], []
    for k, rs in enumerate(halves):
        xk = x_ext[rs] + mod[:, 2 * D_MODEL:3 * D_MODEL] * proj[k]
        ms = jnp.mean(xk * xk, axis=-1, keepdims=True)
        hn = xk * lax.rsqrt(ms + EPS) * g_ref[...]
        h2 = hn * (1.0 + mod[:, 4 * D_MODEL:5 * D_MODEL]) + mod[:, 3 * D_MODEL:4 * D_MODEL]
        row = lax.broadcasted_iota(jnp.int32, (half, 1), 0) + k * half
        lparts.append(jnp.where(row >= lo, jnp.where(row < hi, h2, 0.0), 0.0).astype(BF16))
        xparts.append(xk)
    xnew = jnp.concatenate(xparts, axis=0)
    lhs = jnp.concatenate(lparts, axis=0)

    chunks = [(c0, min(c0 + TF_FFN, D_FF)) for c0 in range(0, D_FF, TF_FFN)]
    ups, parts = {}, []

    def up_proj(k):
        c0, c1 = chunks[k]
        ups[k] = (_dot(lhs, wup_ref[:, c0:c1]), _dot(lhs, wup_ref[:, D_FF + c0:D_FF + c1]))

    def conv(uu, c0, c1):
        up = pltpu.roll(uu, 1, 0)[HALO:HALO + tm]
        dn = pltpu.roll(uu, nrows - 1, 0)[HALO:HALO + tm]
        return (cw_ref[0:1, c0:c1] * up + cw_ref[1:2, c0:c1] * uu[HALO:HALO + tm]
                + cw_ref[2:3, c0:c1] * dn + cb_ref[:, c0:c1])

    def gate_down(k):
        c0, c1 = chunks[k]
        ua = conv(ups[k][0], c0, c1)
        ug = conv(ups[k][1], D_FF + c0, D_FF + c1)
        act = (ua * (ug * _sigmoid(ug))).astype(BF16)
        parts.append(_dot(act, wd_ref[c0:c1, :]))

    up_proj(0)
    for k in range(1, len(chunks)):
        up_proj(k)
        gate_down(k - 1)
    gate_down(len(chunks) - 1)
    acc = parts[0]
    for part in parts[1:]:
        acc = acc + part
    o_ref[0] = xnew[HALO:HALO + tm] + mod[:, 5 * D_MODEL:6 * D_MODEL] * acc


def _mixffn(attn, y, x, w_out16, w_up16, conv_w, conv_b, w_down16, mod3, g_ffn):
    b, seq = x.shape[0], x.shape[1]
    ni = seq // TM_FFN
    hpt = TM_FFN // HALO
    nhalo = seq // HALO
    main = lambda w: pl.BlockSpec((1, TM_FFN, w), lambda bb, i: (bb, i, 0))
    prev = lambda w: pl.BlockSpec((1, HALO, w), lambda bb, i: (bb, jnp.maximum(i * hpt - 1, 0), 0))
    nxt = lambda w: pl.BlockSpec((1, HALO, w), lambda bb, i: (bb, jnp.minimum(i * hpt + hpt, nhalo - 1), 0))
    resident = lambda *shape: pl.BlockSpec(shape, lambda bb, i: (0,) * len(shape), pipeline_mode=pl.Buffered(1))
    trio = lambda w: [main(w), prev(w), nxt(w)]
    return pl.pallas_call(
        _mixffn_kernel,
        grid=(b, ni),
        in_specs=trio(ATTN_WIDTH) + trio(SSM_WIDTH) + trio(D_MODEL) + [
            resident(D_MODEL, D_MODEL), resident(D_MODEL, 2 * D_FF), resident(3, 2 * D_FF), resident(1, 2 * D_FF),
            resident(D_FF, D_MODEL),
            pl.BlockSpec((1, 1, 6 * D_MODEL), lambda bb, i: (bb, 0, 0)),
            resident(1, D_MODEL)],
        out_specs=main(D_MODEL),
        out_shape=jax.ShapeDtypeStruct((b, seq, D_MODEL), F32),
        compiler_params=_params("arbitrary", "arbitrary"),
        name="mixffn",
    )(attn, attn, attn, y, y, y, x, x, x, w_out16, w_up16, conv_w, conv_b.reshape(1, -1), w_down16, mod3,
      g_ffn.reshape(1, -1))


def _layer(x, c, ctx, c_ctx, w_mod, b_mod, g_mix, w_in, g_q, g_k, sink, ssm_conv_w, ssm_conv_b,
           a_log, dt_bias, d_skip, g_ssm, w_out, g_ffn, w_up, ffn_conv_w, ffn_conv_b, w_down):
    mod3 = _modulation(c, c_ctx, w_mod, b_mod)

    q, kk, vv, z, xbc, dt = _inproj(x, ctx, mod3, g_mix, w_in, g_q, g_k)
    attn = _attention(q, kk, vv, sink)

    head_rows = lambda t: jnp.broadcast_to(t.reshape(2 * SSM_HEADS, 1), (2 * SSM_HEADS, CHUNK))
    y = _ssd(xbc, dt, z, ssm_conv_w, ssm_conv_b.reshape(1, -1), head_rows(dt_bias), head_rows(a_log),
             jnp.repeat(d_skip, SSM_HEAD_DIM).reshape(1, -1), g_ssm.reshape(1, -1))

    return _mixffn(attn, y, x, w_out.astype(BF16), w_up.astype(BF16), ffn_conv_w, ffn_conv_b, w_down.astype(BF16),
                   mod3, g_ffn)


def kernel(x, c, ctx, c_ctx, w_mod, b_mod, g_mix, w_in, g_q, g_k, sink, ssm_conv_w, ssm_conv_b,
           a_log, dt_bias, d_skip, g_ssm, w_out, g_ffn, w_up, ffn_conv_w, ffn_conv_b, w_down):
    assert w_mod.shape[0] == 1, "single-layer block"
    first = lambda t: t.reshape(t.shape[1:])
    return _layer(x, c, ctx, c_ctx, first(w_mod), first(b_mod), first(g_mix), first(w_in), first(g_q),
                  first(g_k), first(sink), first(ssm_conv_w), first(ssm_conv_b), first(a_log),
                  first(dt_bias), first(d_skip), first(g_ssm), first(w_out), first(g_ffn), first(w_up),
                  first(ffn_conv_w), first(ffn_conv_b), first(w_down))
```

```python
import jax
import jax.numpy as jnp
import numpy as np
from jax import lax
from jax.experimental import pallas as pl
from jax.experimental.pallas import tpu as pltpu

F32 = jnp.float32
BF16 = jnp.bfloat16

D_MODEL = 1024
CTX = 256
GRID_W = 64
HEAD_DIM = 64
ATTN_HEADS = 8
KV_HEADS = 2
ATTN_WIDTH = ATTN_HEADS * HEAD_DIM
KV_WIDTH = KV_HEADS * HEAD_DIM
WINDOW = 128
BLOCK = 128
ROPE_BASE = 10000.0
SSM_WIDTH = D_MODEL - ATTN_WIDTH
SSM_HEAD_DIM = 64
SSM_HEADS = SSM_WIDTH // SSM_HEAD_DIM
SSM_GROUPS = 2
D_STATE = 128
XBC_WIDTH = SSM_WIDTH + 2 * SSM_GROUPS * D_STATE
CHUNK = 128
D_FF = 2816
IN_COLS = ATTN_WIDTH + 2 * KV_WIDTH + SSM_WIDTH + XBC_WIDTH + 2 * SSM_HEADS
EPS = 1e-6
NEG_INF = -1e30
LOG2E = 1.4426950408889634

LANES = 128
BF16_SUBLANES = 16
IN_COLS_PAD = 19 * LANES
COL_Q, COL_K, COL_V = 0, ATTN_WIDTH, ATTN_WIDTH + KV_WIDTH
COL_Z = ATTN_WIDTH + 2 * KV_WIDTH
COL_XBC = COL_Z + SSM_WIDTH
COL_DT = COL_XBC + XBC_WIDTH
VMEM_LIMIT = 56 * 1024 * 1024

TM_IN = 3 * CTX
TM_FFN = 512
TF_FFN = 1024
HALO = BF16_SUBLANES


def _params(*sem):
    return pltpu.CompilerParams(dimension_semantics=sem, vmem_limit_bytes=VMEM_LIMIT)


def _sigmoid(v):
    return 1.0 / (1.0 + jnp.exp2(v * (-LOG2E)))


def _softplus(v):
    return jnp.maximum(v, 0.0) + jnp.log1p(jnp.exp(-jnp.abs(v)))


def _dot(a, b):
    return jnp.dot(a, b, preferred_element_type=F32)


def _dot_nt(a, b):
    return lax.dot_general(a, b, (((1,), (1,)), ((), ())), preferred_element_type=F32)


def _mod_kernel(c_ref, cctx_ref, w_ref, b_ref, o_ref):
    rows = o_ref.shape[0]
    b = c_ref.shape[0]
    cv = jnp.concatenate([c_ref[...], cctx_ref[...], jnp.zeros((rows - b - 1, D_MODEL), F32)], axis=0)
    s = (cv * _sigmoid(cv)).astype(BF16)
    res = _dot(s, w_ref[...].astype(BF16)) + b_ref[...]
    for r in range(b + 1):
        o_ref[r] = res[r:r + 1, :]
    o_ref[b + 1:rows] = jnp.zeros((rows - b - 1, 1, res.shape[1]), F32)


def _modulation(c, c_ctx, w_mod, b_mod):
    b = c.shape[0]
    rows = 8 * ((b + 1 + 7) // 8)
    tn = 1024
    return pl.pallas_call(
        _mod_kernel,
        grid=(6 * D_MODEL // tn,),
        in_specs=[pl.BlockSpec((b, D_MODEL), lambda j: (0, 0)),
                  pl.BlockSpec((1, D_MODEL), lambda j: (0, 0)),
                  pl.BlockSpec((D_MODEL, tn), lambda j: (0, j)),
                  pl.BlockSpec((1, tn), lambda j: (0, j))],
        out_specs=pl.BlockSpec((rows, 1, tn), lambda j: (0, 0, j)),
        out_shape=jax.ShapeDtypeStruct((rows, 1, 6 * D_MODEL), F32),
        compiler_params=_params("arbitrary"),
        name="mod",
    )(c, c_ctx.reshape(1, -1), w_mod, b_mod.reshape(1, -1))


def _inproj_kernel(xa_ref, xb_ref, xc_ref, ctx_ref, modb_ref, modc_ref, gmix_ref, w_ref, gq_ref, gk_ref,
                   cos_ref, sina_ref, sinb_ref, bd_ref,
                   q_ref, k_ref, v_ref, z_ref, xbc_ref, dt_ref, w16_scr):
    first = pl.program_id(1) == 0

    @pl.when(first & (pl.program_id(0) == 0))
    def _():
        w16_scr[:, 0:IN_COLS] = w_ref[...].astype(BF16)
        w16_scr[:, IN_COLS:IN_COLS_PAD] = jnp.zeros((D_MODEL, IN_COLS_PAD - IN_COLS), BF16)

    def normed(xin, mod):
        ms = jnp.mean(xin * xin, axis=-1, keepdims=True)
        hn = xin * lax.rsqrt(ms + EPS) * gmix_ref[...]
        return (hn * (1.0 + mod[:, D_MODEL:2 * D_MODEL]) + mod[:, 0:D_MODEL]).astype(BF16)

    modb = modb_ref[0]
    h = jnp.concatenate([normed(jnp.where(first, ctx_ref[0], xa_ref[0]), jnp.where(first, modc_ref[0], modb)),
                         normed(xb_ref[0], modb), normed(xc_ref[0], modb)], axis=0)

    cos = cos_ref[...]
    sina = sina_ref[...]
    sinb = sinb_ref[...]
    bd = bd_ref[...]

    def norm_rope(pf, gain):
        ss = pf * pf
        hi = ss.astype(BF16)
        lo = (ss - hi.astype(F32)).astype(BF16)
        sums = _dot(jnp.concatenate([hi, lo], axis=1), bd)
        y = pf * lax.rsqrt(sums * (1.0 / HEAD_DIM) + EPS) * gain
        return y * cos + pltpu.roll(y, LANES - 16, 1) * sina + pltpu.roll(y, 16, 1) * sinb

    def q_chunks(lo, hi):
        for cidx in range(lo, hi):
            cs = slice(cidx * LANES, (cidx + 1) * LANES)
            q_ref[0, :, cs] = norm_rope(qf[:, cs], gq_ref[...]).astype(BF16)

    qf = _dot(h, w16_scr[:, COL_Q:COL_Q + ATTN_WIDTH])
    xf = _dot(h, w16_scr[:, COL_XBC:COL_XBC + XBC_WIDTH])
    q_chunks(0, 2)
    kvf = _dot(h, w16_scr[:, COL_K:COL_K + 2 * KV_WIDTH])
    q_chunks(2, ATTN_WIDTH // LANES)
    zf = _dot(h, w16_scr[:, COL_Z:COL_Z + SSM_WIDTH])
    dtf = _dot(h, w16_scr[:, COL_DT:COL_DT + LANES])
    kf = norm_rope(kvf[:, 0:KV_WIDTH], gk_ref[...])
    k_ref[0, :, 0:LANES] = kf.astype(BF16)
    k_ref[0, :, LANES:2 * LANES] = pltpu.roll(kf, HEAD_DIM, 1).astype(BF16)
    vf = kvf[:, KV_WIDTH:2 * KV_WIDTH]
    v_ref[0, :, 0:LANES] = vf.astype(BF16)
    v_ref[0, :, LANES:2 * LANES] = pltpu.roll(vf, HEAD_DIM, 1).astype(BF16)
    z_ref[0] = zf.astype(BF16)
    xbc_ref[0] = xf.astype(BF16)
    dt_ref[0] = dtf


def _rope_tables(seq):
    rows = seq // GRID_W
    pos_r = np.repeat(np.arange(rows), GRID_W).astype(np.float32)
    pos_c = np.tile(np.arange(GRID_W), rows).astype(np.float32)
    quarter = HEAD_DIM // 4
    freqs = (ROPE_BASE ** (-np.arange(quarter, dtype=np.float32) / quarter)).astype(np.float32)
    ang_r = pos_r[:, None] * freqs[None, :]
    ang_c = pos_c[:, None] * freqs[None, :]
    cr, sr, cc, sc = np.cos(ang_r), np.sin(ang_r), np.cos(ang_c), np.sin(ang_c)
    zero = np.zeros_like(sr)
    cos = np.concatenate([cr, cr, cc, cc], axis=-1)
    sina = np.concatenate([-sr, zero, -sc, zero], axis=-1)
    sinb = np.concatenate([zero, sr, zero, sc], axis=-1)
    reps = LANES // HEAD_DIM
    pad = lambda t, v: np.concatenate([np.full((CTX, LANES), v, np.float32), np.tile(t, (1, reps))], axis=0)
    return tuple(jnp.asarray(pad(t, v), F32) for t, v in ((cos, 1.0), (sina, 0.0), (sinb, 0.0)))


def _inproj(x, ctx, mod3, g_mix, w_in, g_q, g_k):
    b, seq = x.shape[0], x.shape[1]
    tot = CTX + seq
    cos, sina, sinb = _rope_tables(seq)
    reps = LANES // HEAD_DIM
    gq = (jnp.tile(g_q, reps) * (HEAD_DIM ** -0.5 * LOG2E)).reshape(1, LANES)
    gk = jnp.tile(g_k, reps).reshape(1, LANES)
    lane = np.arange(LANES)
    bd = lane[:, None] // HEAD_DIM == lane[None, :] // HEAD_DIM
    bd = jnp.asarray(np.concatenate([bd, bd], axis=0), BF16)
    nsteps = tot // TM_IN
    sub = TM_IN // CTX
    const = lambda *shape: pl.BlockSpec(shape, lambda bb, s: (0,) * len(shape))
    tab = pl.BlockSpec((TM_IN, LANES), lambda bb, s: (s, 0))
    out = lambda w: pl.BlockSpec((1, TM_IN, w), lambda bb, s: (bb, s, 0))
    shp = lambda w, dt: jax.ShapeDtypeStruct((b, tot, w), dt)
    xsub = lambda j: pl.BlockSpec((1, CTX, D_MODEL), lambda bb, s: (bb, jnp.maximum(sub * s + j - 1, 0), 0))
    return pl.pallas_call(
        _inproj_kernel,
        grid=(b, nsteps),
        in_specs=[xsub(0), xsub(1), xsub(2),
                  pl.BlockSpec((1, CTX, D_MODEL), lambda bb, s: (bb, 0, 0)),
                  pl.BlockSpec((1, 1, 6 * D_MODEL), lambda bb, s: (bb, 0, 0)),
                  pl.BlockSpec((1, 1, 6 * D_MODEL), lambda bb, s: (b, 0, 0)),
                  const(1, D_MODEL),
                  pl.BlockSpec((D_MODEL, IN_COLS), lambda bb, s: (0, 0), pipeline_mode=pl.Buffered(1)),
                  const(1, LANES), const(1, LANES),
                  tab, tab, tab, const(2 * LANES, LANES)],
        out_specs=[out(ATTN_WIDTH), out(2 * LANES), out(2 * LANES), out(SSM_WIDTH), out(XBC_WIDTH), out(LANES)],
        out_shape=[shp(ATTN_WIDTH, BF16), shp(2 * LANES, BF16), shp(2 * LANES, BF16),
                   shp(SSM_WIDTH, BF16), shp(XBC_WIDTH, BF16), shp(LANES, F32)],
        scratch_shapes=[pltpu.VMEM((D_MODEL, IN_COLS_PAD), BF16)],
        compiler_params=_params("arbitrary", "arbitrary"),
        name="inproj",
    )(x, x, x, ctx, mod3, mod3, g_mix.reshape(1, -1), w_in, gq, gk, cos, sina, sinb, bd)


def _place_kv_head(t, kh, par):
    src = 0 if par == kh else 1
    lane = lax.broadcasted_iota(jnp.int32, (t.shape[0], LANES), 1)
    half = (lane >= HEAD_DIM) if par else (lane < HEAD_DIM)
    tile = t[:, src * LANES:(src + 1) * LANES]
    return jnp.where(half, tile, jnp.zeros_like(tile))


def _attn_kernel(sink_ref, q_ref, k_ref, v_ref, o_ref, kx_scr, vx_scr):
    seq = o_ref.shape[1]
    nblk = seq // BLOCK
    qi = lax.broadcasted_iota(jnp.int32, (BLOCK, BLOCK), 0)
    kj = lax.broadcasted_iota(jnp.int32, (BLOCK, BLOCK), 1)
    zero = jnp.zeros((BLOCK, BLOCK), F32)
    tri_prev = jnp.where(kj >= qi, zero, NEG_INF)
    tri_next = jnp.where(kj <= qi, zero, NEG_INF)
    combos = [(kh, par) for kh in range(KV_HEADS) for par in range(2)]
    for idx, (kh, par) in enumerate(combos):
        kx_scr[idx] = _place_kv_head(k_ref[0, 0:CTX, :], kh, par)
        vx_scr[idx] = _place_kv_head(v_ref[0, 0:CTX, :], kh, par)

    def block(nb):
        r0 = pl.multiple_of(CTX + nb * BLOCK, BLOCK)
        rn = pl.multiple_of(CTX + jnp.minimum(nb + 1, nblk - 1) * BLOCK, BLOCK)
        rp = pl.multiple_of(CTX - BLOCK + nb * BLOCK, BLOCK)
        kloc = jnp.concatenate([k_ref[0, pl.ds(rp, 2 * BLOCK), :], k_ref[0, pl.ds(rn, BLOCK), :]], axis=0)
        vloc = jnp.concatenate([v_ref[0, pl.ds(rp, 2 * BLOCK), :], v_ref[0, pl.ds(rn, BLOCK), :]], axis=0)
        bias_p = jnp.where(nb > 0, tri_prev, NEG_INF)
        bias_n = jnp.where(nb < nblk - 1, tri_next, NEG_INF)
        q = q_ref[0, pl.ds(r0, BLOCK), :]
        q2s = [jnp.concatenate([q[:, (2 * kh) * LANES:(2 * kh + 1) * LANES],
                                q[:, (2 * kh + 1) * LANES:(2 * kh + 2) * LANES]], axis=0)
               for kh in range(KV_HEADS)]
        sc, ee, oo = {}, {}, {}

        def scores(idx):
            kh, par = combos[idx]
            sc[idx] = (_dot_nt(q2s[kh], _place_kv_head(kloc, kh, par)),
                       _dot_nt(q2s[kh], kx_scr[idx]))

        def softmax(idx):
            kh, par = combos[idx]
            s_loc, s_ctx = sc[idx]
            es, inv = [], []
            for r in range(2):
                head = 4 * kh + par + 2 * r
                rs = slice(r * BLOCK, (r + 1) * BLOCK)
                sv = jnp.concatenate([s_loc[rs, 0:BLOCK] + bias_p, s_loc[rs, BLOCK:2 * BLOCK],
                                      s_loc[rs, 2 * BLOCK:3 * BLOCK] + bias_n, s_ctx[rs]], axis=1)
                sk = sink_ref[head] * LOG2E
                m = jnp.maximum(jnp.max(sv, axis=-1, keepdims=True), sk)
                e = jnp.exp2(sv - m)
                inv.append(1.0 / (jnp.sum(e, axis=-1, keepdims=True) + jnp.exp2(sk - m)))
                es.append(e.astype(BF16))
            ee[idx] = (jnp.concatenate(es, axis=0), inv)

        def pv(idx):
            kh, par = combos[idx]
            e16, inv = ee[idx]
            o2 = _dot(e16[:, 0:3 * BLOCK], _place_kv_head(vloc, kh, par)) + _dot(e16[:, 3 * BLOCK:], vx_scr[idx])
            oo[idx] = [o2[r * BLOCK:(r + 1) * BLOCK] * inv[r] for r in range(2)]

        def finish():
            for kh in range(KV_HEADS):
                for r in range(2):
                    c = 2 * kh + r
                    val = oo[combos.index((kh, 0))][r] + oo[combos.index((kh, 1))][r]
                    o_ref[0, pl.ds(pl.multiple_of(nb * BLOCK, BLOCK), BLOCK), c * LANES:(c + 1) * LANES] = val.astype(BF16)

        stage = lambda f, idx: (lambda: f(idx))
        return [stage(scores, 0), stage(scores, 1), stage(softmax, 0), stage(scores, 2), stage(softmax, 1),
                stage(pv, 0), stage(scores, 3), stage(softmax, 2), stage(pv, 1), stage(softmax, 3), stage(pv, 2),
                stage(pv, 3), finish]

    def two_blocks(i, carry):
        first, second = block(2 * i), block(2 * i + 1)
        lag = len(first) // 2
        for k in range(len(first) + lag):
            if k < len(first):
                first[k]()
            if 0 <= k - lag < len(second):
                second[k - lag]()
        return carry

    assert nblk % 2 == 0
    lax.fori_loop(0, nblk // 2, two_blocks, 0)


def _attention(q, kk, vv, sink):
    b, tot = q.shape[0], q.shape[1]
    seq = tot - CTX
    whole = lambda w: pl.BlockSpec((1, tot, w), lambda bb: (bb, 0, 0))
    return pl.pallas_call(
        _attn_kernel,
        grid=(b,),
        in_specs=[pl.BlockSpec(memory_space=pltpu.SMEM), whole(ATTN_WIDTH), whole(2 * LANES), whole(2 * LANES)],
        out_specs=pl.BlockSpec((1, seq, ATTN_WIDTH), lambda bb: (bb, 0, 0)),
        out_shape=jax.ShapeDtypeStruct((b, seq, ATTN_WIDTH), BF16),
        scratch_shapes=[pltpu.VMEM((2 * KV_HEADS, CTX, LANES), BF16), pltpu.VMEM((2 * KV_HEADS, CTX, LANES), BF16)],
        compiler_params=_params("arbitrary"),
        name="attn",
    )(sink, q, kk, vv)


def _ssd_selectors():
    nh2 = 2 * SSM_HEADS
    npairs = SSM_HEADS // 2
    k = np.arange(LANES)
    part, q = k // (2 * nh2), k % (2 * nh2)
    live = part < 3
    col = np.arange(nh2 * LANES)
    tile, lane = col // LANES, col % LANES
    kind, blk = tile // (nh2 // 2), tile % (nh2 // 2)
    src = kind * nh2 + (blk // npairs) * SSM_HEADS + 2 * (blk % npairs) + (lane >= SSM_HEAD_DIM)
    pairsel = live[:, None] & (q[:, None] == src[None, :])
    segind = live[:, None] & (q[:, None] < nh2) & (q[:, None] == tile[None, :])
    return jnp.asarray(pairsel, BF16), jnp.asarray(segind, BF16)


def _ssd_kernel(x_ref, dt_ref, z_ref, cw_ref, cb_ref, dtb_ref, alog_ref, dsk_ref, g_ref, tri_ref, shift_ref,
                pairsel_ref, segind_ref,
                y_ref, hf_scr, hb_scr, yacc_scr, stb_scr, decb_scr, eb_scr, cbuf_scr, segrhs_scr, cols3_scr, yt_scr):
    tot = x_ref.shape[1]
    nchunks = tot // CHUNK
    nctx = CTX // CHUNK
    npairs = SSM_HEADS // 2
    pairs_per_group = npairs // SSM_GROUPS
    group_cols = pairs_per_group * LANES

    hf_scr[...] = jnp.zeros_like(hf_scr)
    hb_scr[...] = jnp.zeros_like(hb_scr)
    segrhs_scr[...] = segind_ref[...]

    nh2 = 2 * SSM_HEADS

    def prep(chunk):
        c0 = pl.multiple_of(chunk * CHUNK, CHUNK)
        dtv = _softplus(dt_ref[0, pl.ds(c0, CHUNK), :].T[0:nh2, :] + dtb_ref[...])
        a = dtv * (-jnp.exp(alog_ref[...]) * LOG2E)

        def split3(v):
            hi = v.astype(BF16)
            r1 = v - hi.astype(F32)
            mid = r1.astype(BF16)
            return hi, mid, (r1 - mid.astype(F32)).astype(BF16)

        parts = jnp.concatenate(split3(a), axis=0)
        fold = lambda p: p[0:nh2] + p[nh2:2 * nh2] + p[2 * nh2:3 * nh2]
        prefix = fold(_dot(parts, tri_ref[0]))
        suffix = fold(_dot(parts, tri_ref[1]))
        head_row = lax.broadcasted_iota(jnp.int32, (nh2, CHUNK), 0)
        ac_t = jnp.where(head_row < SSM_HEADS, prefix, suffix)

        v32 = [p.astype(F32) for p in split3(jnp.concatenate([ac_t, dtv], axis=0))]
        ones3 = (lax.broadcasted_iota(jnp.int32, (CHUNK - 6 * nh2, CHUNK), 0) < 3).astype(F32)
        cols3_scr[...] = jnp.concatenate(v32 + [ones3], axis=0).T.astype(BF16)
        for q in range(nh2):
            neg = [-p[q:q + 1, :] for p in v32]
            tile = jnp.concatenate(neg + [jnp.zeros((BF16_SUBLANES - 3, CHUNK), F32)], axis=0)
            segrhs_scr[6 * nh2:6 * nh2 + BF16_SUBLANES, q * CHUNK:(q + 1) * CHUNK] = tile.astype(BF16)

    prep(0)

    def chunk_pass(chunk, carry):
        lc = jnp.maximum(chunk - nctx, 0)
        r0 = pl.multiple_of(lc * CHUNK, CHUNK)
        c0 = pl.multiple_of(chunk * CHUNK, CHUNK)
        p0 = pl.multiple_of(jnp.maximum(chunk * CHUNK - HALO, 0), HALO)
        n0 = pl.multiple_of(jnp.minimum(chunk * CHUNK + CHUNK, tot - HALO), HALO)

        cols3 = cols3_scr[...]
        x16 = x_ref[0, pl.ds(c0, CHUNK), :]
        has_prev = (chunk != 0) & (chunk != nctx)
        has_next = (chunk != nctx - 1) & (chunk != nchunks - 1)
        halo_p = x_ref[0, pl.ds(p0, HALO), :]
        halo_n = x_ref[0, pl.ds(n0, HALO), :]
        xext = jnp.concatenate([jnp.where(has_prev, halo_p, jnp.zeros_like(halo_p)), x16,
                                jnp.where(has_next, halo_n, jnp.zeros_like(halo_n))], axis=0)
        shifted = _dot(shift_ref[...], xext)
        pairs = _dot(cols3, pairsel_ref[...])
        seg_all = _dot(cols3, segrhs_scr[...])

        u = (cw_ref[0:1, :] * shifted[0:CHUNK] + cw_ref[1:2, :] * x16.astype(F32)
             + cw_ref[2:3, :] * shifted[CHUNK:2 * CHUNK] + cb_ref[...])
        u = u * _sigmoid(u)
        xs = u[:, 0:SSM_WIDTH]
        bmat = u[:, SSM_WIDTH:SSM_WIDTH + SSM_GROUPS * D_STATE]
        cmat = u[:, SSM_WIDTH + SSM_GROUPS * D_STATE:XBC_WIDTH]
        cb16 = [cmat[:, g * D_STATE:(g + 1) * D_STATE].astype(BF16) for g in range(SSM_GROUPS)]
        bb16 = [bmat[:, g * D_STATE:(g + 1) * D_STATE].astype(BF16) for g in range(SSM_GROUPS)]
        bt16 = [bmat[:, g * D_STATE:(g + 1) * D_STATE].T.astype(BF16) for g in range(SSM_GROUPS)]
        cbuf_scr[lc] = jnp.concatenate(cb16, axis=1)

        hf_start = [hf_scr[:, g * group_cols:(g + 1) * group_cols] for g in range(SSM_GROUPS)]
        gm2 = _dot_nt(jnp.concatenate(cb16, axis=0), jnp.concatenate(bb16, axis=0))
        gmat = [gm2[g * CHUNK:(g + 1) * CHUNK, g * CHUNK:(g + 1) * CHUNK] for g in range(SSM_GROUPS)]
        yoff = [_dot(cb16[g], hf_start[g].astype(BF16)) for g in range(SSM_GROUPS)]

        half0 = lax.broadcasted_iota(jnp.int32, (CHUNK, LANES), 1) < SSM_HEAD_DIM
        ii = lax.broadcasted_iota(jnp.int32, (CHUNK, CHUNK), 0)
        jj = lax.broadcasted_iota(jnp.int32, (CHUNK, CHUNK), 1)
        causal = [jj <= ii, jj >= ii]
        last = [CHUNK - 1, 0]

        arg_pair, tot_pair, xdt_pair, xdtd_pair = [], [], [], []
        for d in range(2):
            args, tots, xdts, xdtds = [], [], [], []
            for m in range(npairs):
                blk = d * npairs + m
                arg = pairs[:, blk * LANES:(blk + 1) * LANES]
                dtp = pairs[:, (nh2 // 2 + blk) * LANES:(nh2 // 2 + blk + 1) * LANES]
                total = arg[last[d]:last[d] + 1, :]
                xdt = xs[:, m * LANES:(m + 1) * LANES] * dtp
                args.append(arg)
                tots.append(total)
                xdts.append(xdt)
                xdtds.append(xdt * jnp.exp2(total - arg))
            arg_pair.append(args)
            tot_pair.append(tots)
            xdt_pair.append(xdts)
            xdtd_pair.append(xdtds)

        for g in range(SSM_GROUPS):
            pr = range(g * pairs_per_group, (g + 1) * pairs_per_group)
            gs = slice(g * group_cols, (g + 1) * group_cols)
            st_f = _dot(bt16[g], jnp.concatenate([xdtd_pair[0][m] for m in pr], axis=1).astype(BF16))
            st_b = _dot(bt16[g], jnp.concatenate([xdtd_pair[1][m] for m in pr], axis=1).astype(BF16))
            dec_f = jnp.concatenate([jnp.exp2(tot_pair[0][m]) for m in pr], axis=1)
            dec_b = jnp.concatenate([jnp.exp2(tot_pair[1][m]) for m in pr], axis=1)
            hf_scr[:, gs] = hf_start[g] * dec_f + st_f
            stb_scr[chunk, :, gs] = st_b
            decb_scr[chunk, :, gs] = jnp.broadcast_to(dec_b, (8, group_cols))

        prep(jnp.minimum(chunk + 1, nchunks - 1))
        operands = {}

        def intra_operands(m):
            g = m // pairs_per_group
            smats, rhs = [], []
            for d in range(2):
                for hh in range(2):
                    r = d * SSM_HEADS + 2 * m + hh
                    seg = seg_all[:, r * CHUNK:(r + 1) * CHUNK]
                    smats.append(jnp.where(causal[d], gmat[g] * jnp.exp2(seg), 0.0).astype(BF16))
                xd = xdt_pair[d][m]
                rhs += [jnp.where(half0, xd, 0.0), jnp.where(half0, 0.0, xd)]
            operands[m] = (jnp.concatenate(smats, axis=1), jnp.concatenate(rhs, axis=0).astype(BF16))

        def intra_matmul(m0):
            lhs = jnp.concatenate([operands[m0][0], operands[m0 + 1][0]], axis=1)
            zero = jnp.zeros_like(operands[m0][1])
            rhs = jnp.concatenate([jnp.concatenate([operands[m0][1], zero], axis=1),
                                   jnp.concatenate([zero, operands[m0 + 1][1]], axis=1)], axis=0)
            ydiag = _dot(lhs, rhs)
            for k in range(2):
                m = m0 + k
                g, mm = m // pairs_per_group, m % pairs_per_group
                ms = slice(m * LANES, (m + 1) * LANES)
                yacc_scr[pl.ds(r0, CHUNK), ms] = (ydiag[:, k * LANES:(k + 1) * LANES]
                                                  + yoff[g][:, mm * LANES:(mm + 1) * LANES] * jnp.exp2(arg_pair[0][m])
                                                  + dsk_ref[:, ms] * xs[:, ms])
                eb_scr[lc, :, ms] = jnp.exp2(arg_pair[1][m])

        assert npairs == 4
        intra_operands(0)
        intra_operands(1)
        intra_operands(2)
        intra_matmul(0)
        intra_operands(3)
        intra_matmul(2)
        return carry

    lax.fori_loop(0, nchunks, chunk_pass, 0)

    yt_scr[...] = jnp.zeros_like(yt_scr)
    order = lambda i: jnp.where(i < nctx, nctx - 1 - i, nchunks + nctx - 1 - i)

    def carry_pass(i, carry):
        chunk = order(jnp.minimum(i, nchunks - 1))
        lc = jnp.maximum(chunk - nctx, 0)
        r0 = pl.multiple_of(lc * CHUNK, CHUNK)
        hb = hb_scr[...]
        cb = cbuf_scr[lc]
        yoff = jnp.concatenate(
            [_dot(cb[:, g * D_STATE:(g + 1) * D_STATE], hb[:, g * group_cols:(g + 1) * group_cols].astype(BF16))
             for g in range(SSM_GROUPS)], axis=1)

        prev = order(jnp.maximum(i - 1, 0))
        pr0 = pl.multiple_of(jnp.maximum(prev - nctx, 0) * CHUNK, CHUNK)
        zf = z_ref[0, pl.ds(pl.multiple_of(prev * CHUNK, CHUNK), CHUNK), :].astype(F32)
        yt = yt_scr[...] * (zf * _sigmoid(zf))
        gw = SSM_WIDTH // SSM_GROUPS
        for g in range(SSM_GROUPS):
            yg = yt[:, g * gw:(g + 1) * gw]
            ms = jnp.mean(yg * yg, axis=-1, keepdims=True)
            y_ref[0, pl.ds(pr0, CHUNK), g * gw:(g + 1) * gw] = (yg * lax.rsqrt(ms + EPS)
                                                                * g_ref[:, g * gw:(g + 1) * gw]).astype(BF16)

        yt_scr[...] = yacc_scr[pl.ds(r0, CHUNK), :] + yoff * eb_scr[lc]
        hb_scr[...] = hb * decb_scr[chunk, 0:1, :] + stb_scr[chunk]
        return carry

    lax.fori_loop(0, nchunks + 1, carry_pass, 0)


def _ssd(xbc, dt, z, conv_w, conv_b, dtb16, alog16, dsk, g_ssm):
    b, tot = xbc.shape[0], xbc.shape[1]
    seq = tot - CTX
    nchunks = tot // CHUNK
    nctx = CTX // CHUNK
    nlat = nchunks - nctx
    const = lambda *shape: pl.BlockSpec(shape, lambda bb: (0,) * len(shape))
    whole = lambda rows, w: pl.BlockSpec((1, rows, w), lambda bb: (bb, 0, 0))
    jdx = np.arange(CHUNK)
    tri = jnp.asarray(np.stack([jdx[:, None] <= jdx[None, :], jdx[:, None] >= jdx[None, :]]), BF16)
    kdx = np.arange(CHUNK + 2 * HALO)
    shift = jnp.asarray(np.concatenate([kdx[None, :] == jdx[:, None] + HALO - 1,
                                        kdx[None, :] == jdx[:, None] + HALO + 1], axis=0), BF16)
    return pl.pallas_call(
        _ssd_kernel,
        grid=(b,),
        in_specs=[whole(tot, XBC_WIDTH), whole(tot, LANES), whole(tot, SSM_WIDTH),
                  const(3, XBC_WIDTH), const(1, XBC_WIDTH), const(2 * SSM_HEADS, CHUNK), const(2 * SSM_HEADS, CHUNK),
                  const(1, SSM_WIDTH), const(1, SSM_WIDTH), const(2, CHUNK, CHUNK),
                  const(2 * CHUNK, CHUNK + 2 * HALO),
                  const(LANES, 2 * SSM_HEADS * LANES), const(LANES, 2 * SSM_HEADS * LANES)],
        out_specs=whole(seq, SSM_WIDTH),
        out_shape=jax.ShapeDtypeStruct((b, seq, SSM_WIDTH), BF16),
        scratch_shapes=[pltpu.VMEM((D_STATE, SSM_WIDTH), F32),
                        pltpu.VMEM((D_STATE, SSM_WIDTH), F32),
                        pltpu.VMEM((seq, SSM_WIDTH), F32),
                        pltpu.VMEM((nchunks, D_STATE, SSM_WIDTH), F32),
                        pltpu.VMEM((nchunks, 8, SSM_WIDTH), F32),
                        pltpu.VMEM((nlat, CHUNK, SSM_WIDTH), F32),
                        pltpu.VMEM((nlat, CHUNK, SSM_GROUPS * D_STATE), BF16),
                        pltpu.VMEM((LANES, 2 * SSM_HEADS * LANES), BF16),
                        pltpu.VMEM((CHUNK, LANES), BF16),
                        pltpu.VMEM((CHUNK, SSM_WIDTH), F32)],
        compiler_params=_params("arbitrary"),
        name="ssd",
    )(xbc, dt, z, conv_w, conv_b, dtb16, alog16, dsk, g_ssm, tri, shift, *_ssd_selectors())


def _mixffn_kernel(am_ref, ap_ref, an_ref, ym_ref, yp_ref, yn_ref, xm_ref, xp_ref, xn_ref,
                   wo_ref, wup_ref, cw_ref, cb_ref, wd_ref, mod_ref, g_ref, o_ref):
    i = pl.program_id(1)
    tm = am_ref.shape[1]
    nrows = tm + 2 * HALO
    ext = lambda p, m, n: jnp.concatenate([p[0], m[0], n[0]], axis=0)
    a_ext = ext(ap_ref, am_ref, an_ref)
    y_ext = ext(yp_ref, ym_ref, yn_ref)
    x_ext = ext(xp_ref, xm_ref, xn_ref)
    mod = mod_ref[0]
    lo = jnp.where(i > 0, 0, HALO)
    hi = jnp.where(i < pl.num_programs(1) - 1, nrows, nrows - HALO)
    half = nrows // 2
    halves = [slice(k * half, (k + 1) * half) for k in range(2)]
    proj = [_dot(a_ext[rs], wo_ref[0:ATTN_WIDTH, :]) + _dot(y_ext[rs], wo_ref[ATTN_WIDTH:D_MODEL, :])
            for rs in halves]
    xparts, lparts = [], []
    for k, rs in enumerate(halves):
        xk = x_ext[rs] + mod[:, 2 * D_MODEL:3 * D_MODEL] * proj[k]
        ms = jnp.mean(xk * xk, axis=-1, keepdims=True)
        hn = xk * lax.rsqrt(ms + EPS) * g_ref[...]
        h2 = hn * (1.0 + mod[:, 4 * D_MODEL:5 * D_MODEL]) + mod[:, 3 * D_MODEL:4 * D_MODEL]
        row = lax.broadcasted_iota(jnp.int32, (half, 1), 0) + k * half
        lparts.append(jnp.where(row >= lo, jnp.where(row < hi, h2, 0.0), 0.0).astype(BF16))
        xparts.append(xk)
    xnew = jnp.concatenate(xparts, axis=0)
    lhs = jnp.concatenate(lparts, axis=0)

    chunks = [(c0, min(c0 + TF_FFN, D_FF)) for c0 in range(0, D_FF, TF_FFN)]
    ups, parts = {}, []

    def up_proj(k):
        c0, c1 = chunks[k]
        ups[k] = (_dot(lhs, wup_ref[:, c0:c1]), _dot(lhs, wup_ref[:, D_FF + c0:D_FF + c1]))

    def conv(uu, c0, c1):
        up = pltpu.roll(uu, 1, 0)[HALO:HALO + tm]
        dn = pltpu.roll(uu, nrows - 1, 0)[HALO:HALO + tm]
        return (cw_ref[0:1, c0:c1] * up + cw_ref[1:2, c0:c1] * uu[HALO:HALO + tm]
                + cw_ref[2:3, c0:c1] * dn + cb_ref[:, c0:c1])

    def gate_down(k):
        c0, c1 = chunks[k]
        ua = conv(ups[k][0], c0, c1)
        ug = conv(ups[k][1], D_FF + c0, D_FF + c1)
        act = (ua * (ug * _sigmoid(ug))).astype(BF16)
        parts.append(_dot(act, wd_ref[c0:c1, :]))

    up_proj(0)
    for k in range(1, len(chunks)):
        up_proj(k)
        gate_down(k - 1)
    gate_down(len(chunks) - 1)
    acc = parts[0]
    for part in parts[1:]:
        acc = acc + part
    o_ref[0] = xnew[HALO:HALO + tm] + mod[:, 5 * D_MODEL:6 * D_MODEL] * acc


def _mixffn(attn, y, x, w_out16, w_up16, conv_w, conv_b, w_down16, mod3, g_ffn):
    b, seq = x.shape[0], x.shape[1]
    ni = seq // TM_FFN
    hpt = TM_FFN // HALO
    nhalo = seq // HALO
    main = lambda w: pl.BlockSpec((1, TM_FFN, w), lambda bb, i: (bb, i, 0))
    prev = lambda w: pl.BlockSpec((1, HALO, w), lambda bb, i: (bb, jnp.maximum(i * hpt - 1, 0), 0))
    nxt = lambda w: pl.BlockSpec((1, HALO, w), lambda bb, i: (bb, jnp.minimum(i * hpt + hpt, nhalo - 1), 0))
    resident = lambda *shape: pl.BlockSpec(shape, lambda bb, i: (0,) * len(shape), pipeline_mode=pl.Buffered(1))
    trio = lambda w: [main(w), prev(w), nxt(w)]
    return pl.pallas_call(
        _mixffn_kernel,
        grid=(b, ni),
        in_specs=trio(ATTN_WIDTH) + trio(SSM_WIDTH) + trio(D_MODEL) + [
            resident(D_MODEL, D_MODEL), resident(D_MODEL, 2 * D_FF), resident(3, 2 * D_FF), resident(1, 2 * D_FF),
            resident(D_FF, D_MODEL),
            pl.BlockSpec((1, 1, 6 * D_MODEL), lambda bb, i: (bb, 0, 0)),
            resident(1, D_MODEL)],
        out_specs=main(D_MODEL),
        out_shape=jax.ShapeDtypeStruct((b, seq, D_MODEL), F32),
        compiler_params=_params("arbitrary", "arbitrary"),
        name="mixffn",
    )(attn, attn, attn, y, y, y, x, x, x, w_out16, w_up16, conv_w, conv_b.reshape(1, -1), w_down16, mod3,
      g_ffn.reshape(1, -1))


def _layer(x, c, ctx, c_ctx, w_mod, b_mod, g_mix, w_in, g_q, g_k, sink, ssm_conv_w, ssm_conv_b,
           a_log, dt_bias, d_skip, g_ssm, w_out, g_ffn, w_up, ffn_conv_w, ffn_conv_b, w_down):
    mod3 = _modulation(c, c_ctx, w_mod, b_mod)

    q, kk, vv, z, xbc, dt = _inproj(x, ctx, mod3, g_mix, w_in, g_q, g_k)
    attn = _attention(q, kk, vv, sink)

    head_rows = lambda t: jnp.broadcast_to(t.reshape(2 * SSM_HEADS, 1), (2 * SSM_HEADS, CHUNK))
    y = _ssd(xbc, dt, z, ssm_conv_w, ssm_conv_b.reshape(1, -1), head_rows(dt_bias), head_rows(a_log),
             jnp.repeat(d_skip, SSM_HEAD_DIM).reshape(1, -1), g_ssm.reshape(1, -1))

    return _mixffn(attn, y, x, w_out.astype(BF16), w_up.astype(BF16), ffn_conv_w, ffn_conv_b, w_down.astype(BF16),
                   mod3, g_ffn)


def kernel(x, c, ctx, c_ctx, w_mod, b_mod, g_mix, w_in, g_q, g_k, sink, ssm_conv_w, ssm_conv_b,
           a_log, dt_bias, d_skip, g_ssm, w_out, g_ffn, w_up, ffn_conv_w, ffn_conv_b, w_down):
    assert w_mod.shape[0] == 1, "single-layer block"
    first = lambda t: t.reshape(t.shape[1:])
    return _layer(x, c, ctx, c_ctx, first(w_mod), first(b_mod), first(g_mix), first(w_in), first(g_q),
                  first(g_k), first(sink), first(ssm_conv_w), first(ssm_conv_b), first(a_log),
                  first(dt_bias), first(d_skip), first(g_ssm), first(w_out), first(g_ffn), first(w_up),
                  first(ffn_conv_w), first(ffn_conv_b), first(w_down))
```

```python
import jax
import jax.numpy as jnp
import numpy as np
from jax import lax
from jax.experimental import pallas as pl
from jax.experimental.pallas import tpu as pltpu

F32 = jnp.float32
BF16 = jnp.bfloat16

D_MODEL = 1024
CTX = 256
GRID_W = 64
HEAD_DIM = 64
ATTN_HEADS = 8
KV_HEADS = 2
ATTN_WIDTH = ATTN_HEADS * HEAD_DIM
KV_WIDTH = KV_HEADS * HEAD_DIM
WINDOW = 128
BLOCK = 128
ROPE_BASE = 10000.0
SSM_WIDTH = D_MODEL - ATTN_WIDTH
SSM_HEAD_DIM = 64
SSM_HEADS = SSM_WIDTH // SSM_HEAD_DIM
SSM_GROUPS = 2
D_STATE = 128
XBC_WIDTH = SSM_WIDTH + 2 * SSM_GROUPS * D_STATE
CHUNK = 128
D_FF = 2816
IN_COLS = ATTN_WIDTH + 2 * KV_WIDTH + SSM_WIDTH + XBC_WIDTH + 2 * SSM_HEADS
EPS = 1e-6
NEG_INF = -1e30
LOG2E = 1.4426950408889634

LANES = 128
BF16_SUBLANES = 16
IN_COLS_PAD = 19 * LANES
COL_Q, COL_K, COL_V = 0, ATTN_WIDTH, ATTN_WIDTH + KV_WIDTH
COL_Z = ATTN_WIDTH + 2 * KV_WIDTH
COL_XBC = COL_Z + SSM_WIDTH
COL_DT = COL_XBC + XBC_WIDTH
VMEM_LIMIT = 56 * 1024 * 1024

TM_IN = 3 * CTX
TM_FFN = 512
TF_FFN = 1024
HALO = BF16_SUBLANES


def _params(*sem):
    return pltpu.CompilerParams(dimension_semantics=sem, vmem_limit_bytes=VMEM_LIMIT)


def _sigmoid(v):
    return 1.0 / (1.0 + jnp.exp2(v * (-LOG2E)))


def _softplus(v):
    return jnp.maximum(v, 0.0) + jnp.log1p(jnp.exp(-jnp.abs(v)))


def _dot(a, b):
    return jnp.dot(a, b, preferred_element_type=F32)


def _dot_nt(a, b):
    return lax.dot_general(a, b, (((1,), (1,)), ((), ())), preferred_element_type=F32)


def _mod_kernel(c_ref, cctx_ref, w_ref, b_ref, o_ref):
    rows = o_ref.shape[0]
    b = c_ref.shape[0]
    cv = jnp.concatenate([c_ref[...], cctx_ref[...], jnp.zeros((rows - b - 1, D_MODEL), F32)], axis=0)
    s = (cv * _sigmoid(cv)).astype(BF16)
    res = _dot(s, w_ref[...].astype(BF16)) + b_ref[...]
    for r in range(b + 1):
        o_ref[r] = res[r:r + 1, :]
    o_ref[b + 1:rows] = jnp.zeros((rows - b - 1, 1, res.shape[1]), F32)


def _modulation(c, c_ctx, w_mod, b_mod):
    b = c.shape[0]
    rows = 8 * ((b + 1 + 7) // 8)
    tn = 1024
    return pl.pallas_call(
        _mod_kernel,
        grid=(6 * D_MODEL // tn,),
        in_specs=[pl.BlockSpec((b, D_MODEL), lambda j: (0, 0)),
                  pl.BlockSpec((1, D_MODEL), lambda j: (0, 0)),
                  pl.BlockSpec((D_MODEL, tn), lambda j: (0, j)),
                  pl.BlockSpec((1, tn), lambda j: (0, j))],
        out_specs=pl.BlockSpec((rows, 1, tn), lambda j: (0, 0, j)),
        out_shape=jax.ShapeDtypeStruct((rows, 1, 6 * D_MODEL), F32),
        compiler_params=_params("arbitrary"),
        name="mod",
    )(c, c_ctx.reshape(1, -1), w_mod, b_mod.reshape(1, -1))


def _inproj_kernel(xa_ref, xb_ref, xc_ref, ctx_ref, modb_ref, modc_ref, gmix_ref, w_ref, gq_ref, gk_ref,
                   cos_ref, sina_ref, sinb_ref, bd_ref,
                   q_ref, k_ref, v_ref, z_ref, xbc_ref, dt_ref, w16_scr):
    first = pl.program_id(1) == 0

    @pl.when(first & (pl.program_id(0) == 0))
    def _():
        w16_scr[:, 0:IN_COLS] = w_ref[...].astype(BF16)
        w16_scr[:, IN_COLS:IN_COLS_PAD] = jnp.zeros((D_MODEL, IN_COLS_PAD - IN_COLS), BF16)

    def normed(xin, mod):
        ms = jnp.mean(xin * xin, axis=-1, keepdims=True)
        hn = xin * lax.rsqrt(ms + EPS) * gmix_ref[...]
        return (hn * (1.0 + mod[:, D_MODEL:2 * D_MODEL]) + mod[:, 0:D_MODEL]).astype(BF16)

    modb = modb_ref[0]
    bd = bd_ref[...]

    def project(h, rs, emit_between):
        cos, sina, sinb = cos_ref[rs, :], sina_ref[rs, :], sinb_ref[rs, :]

        def norm_rope(pf, gain):
            ss = pf * pf
            hi = ss.astype(BF16)
            lo = (ss - hi.astype(F32)).astype(BF16)
            sums = _dot(jnp.concatenate([hi, lo], axis=1), bd)
            y = pf * lax.rsqrt(sums * (1.0 / HEAD_DIM) + EPS) * gain
            return y * cos + pltpu.roll(y, LANES - 16, 1) * sina + pltpu.roll(y, 16, 1) * sinb

        def q_chunks(lo, hi):
            for cidx in range(lo, hi):
                cs = slice(cidx * LANES, (cidx + 1) * LANES)
                q_ref[0, rs, cs] = norm_rope(qf[:, cs], gq_ref[...]).astype(BF16)

        qf = _dot(h, w16_scr[:, COL_Q:COL_Q + ATTN_WIDTH])
        xf = _dot(h, w16_scr[:, COL_XBC:COL_XBC + XBC_WIDTH])
        between = emit_between()
        q_chunks(0, 2)
        kvf = _dot(h, w16_scr[:, COL_K:COL_K + 2 * KV_WIDTH])
        q_chunks(2, ATTN_WIDTH // LANES)
        zf = _dot(h, w16_scr[:, COL_Z:COL_Z + SSM_WIDTH])
        dtf = _dot(h, w16_scr[:, COL_DT:COL_DT + LANES])
        kf = norm_rope(kvf[:, 0:KV_WIDTH], gk_ref[...])
        k_ref[0, rs, 0:LANES] = kf.astype(BF16)
        k_ref[0, rs, LANES:2 * LANES] = pltpu.roll(kf, HEAD_DIM, 1).astype(BF16)
        vf = kvf[:, KV_WIDTH:2 * KV_WIDTH]
        v_ref[0, rs, 0:LANES] = vf.astype(BF16)
        v_ref[0, rs, LANES:2 * LANES] = pltpu.roll(vf, HEAD_DIM, 1).astype(BF16)
        z_ref[0, rs, :] = zf.astype(BF16)
        xbc_ref[0, rs, :] = xf.astype(BF16)
        dt_ref[0, rs, :] = dtf
        return between

    h_head = jnp.concatenate([normed(jnp.where(first, ctx_ref[0], xa_ref[0]), jnp.where(first, modc_ref[0], modb)),
                              normed(xb_ref[0], modb)], axis=0)
    h_tail = project(h_head, slice(0, 2 * CTX), lambda: normed(xc_ref[0], modb))
    project(h_tail, slice(2 * CTX, 3 * CTX), lambda: None)


def _rope_tables(seq):
    rows = seq // GRID_W
    pos_r = np.repeat(np.arange(rows), GRID_W).astype(np.float32)
    pos_c = np.tile(np.arange(GRID_W), rows).astype(np.float32)
    quarter = HEAD_DIM // 4
    freqs = (ROPE_BASE ** (-np.arange(quarter, dtype=np.float32) / quarter)).astype(np.float32)
    ang_r = pos_r[:, None] * freqs[None, :]
    ang_c = pos_c[:, None] * freqs[None, :]
    cr, sr, cc, sc = np.cos(ang_r), np.sin(ang_r), np.cos(ang_c), np.sin(ang_c)
    zero = np.zeros_like(sr)
    cos = np.concatenate([cr, cr, cc, cc], axis=-1)
    sina = np.concatenate([-sr, zero, -sc, zero], axis=-1)
    sinb = np.concatenate([zero, sr, zero, sc], axis=-1)
    reps = LANES // HEAD_DIM
    pad = lambda t, v: np.concatenate([np.full((CTX, LANES), v, np.float32), np.tile(t, (1, reps))], axis=0)
    return tuple(jnp.asarray(pad(t, v), F32) for t, v in ((cos, 1.0), (sina, 0.0), (sinb, 0.0)))


def _inproj(x, ctx, mod3, g_mix, w_in, g_q, g_k):
    b, seq = x.shape[0], x.shape[1]
    tot = CTX + seq
    cos, sina, sinb = _rope_tables(seq)
    reps = LANES // HEAD_DIM
    gq = (jnp.tile(g_q, reps) * (HEAD_DIM ** -0.5 * LOG2E)).reshape(1, LANES)
    gk = jnp.tile(g_k, reps).reshape(1, LANES)
    lane = np.arange(LANES)
    bd = lane[:, None] // HEAD_DIM == lane[None, :] // HEAD_DIM
    bd = jnp.asarray(np.concatenate([bd, bd], axis=0), BF16)
    nsteps = tot // TM_IN
    sub = TM_IN // CTX
    const = lambda *shape: pl.BlockSpec(shape, lambda bb, s: (0,) * len(shape))
    tab = pl.BlockSpec((TM_IN, LANES), lambda bb, s: (s, 0))
    out = lambda w: pl.BlockSpec((1, TM_IN, w), lambda bb, s: (bb, s, 0))
    shp = lambda w, dt: jax.ShapeDtypeStruct((b, tot, w), dt)
    xsub = lambda j: pl.BlockSpec((1, CTX, D_MODEL), lambda bb, s: (bb, jnp.maximum(sub * s + j - 1, 0), 0))
    return pl.pallas_call(
        _inproj_kernel,
        grid=(b, nsteps),
        in_specs=[xsub(0), xsub(1), xsub(2),
                  pl.BlockSpec((1, CTX, D_MODEL), lambda bb, s: (bb, 0, 0)),
                  pl.BlockSpec((1, 1, 6 * D_MODEL), lambda bb, s: (bb, 0, 0)),
                  pl.BlockSpec((1, 1, 6 * D_MODEL), lambda bb, s: (b, 0, 0)),
                  const(1, D_MODEL),
                  pl.BlockSpec((D_MODEL, IN_COLS), lambda bb, s: (0, 0), pipeline_mode=pl.Buffered(1)),
                  const(1, LANES), const(1, LANES),
                  tab, tab, tab, const(2 * LANES, LANES)],
        out_specs=[out(ATTN_WIDTH), out(2 * LANES), out(2 * LANES), out(SSM_WIDTH), out(XBC_WIDTH), out(LANES)],
        out_shape=[shp(ATTN_WIDTH, BF16), shp(2 * LANES, BF16), shp(2 * LANES, BF16),
                   shp(SSM_WIDTH, BF16), shp(XBC_WIDTH, BF16), shp(LANES, F32)],
        scratch_shapes=[pltpu.VMEM((D_MODEL, IN_COLS_PAD), BF16)],
        compiler_params=_params("arbitrary", "arbitrary"),
        name="inproj",
    )(x, x, x, ctx, mod3, mod3, g_mix.reshape(1, -1), w_in, gq, gk, cos, sina, sinb, bd)


def _place_kv_head(t, kh, par):
    src = 0 if par == kh else 1
    lane = lax.broadcasted_iota(jnp.int32, (t.shape[0], LANES), 1)
    half = (lane >= HEAD_DIM) if par else (lane < HEAD_DIM)
    tile = t[:, src * LANES:(src + 1) * LANES]
    return jnp.where(half, tile, jnp.zeros_like(tile))


def _attn_kernel(sink_ref, q_ref, k_ref, v_ref, o_ref, kx_scr, vx_scr):
    seq = o_ref.shape[1]
    nblk = seq // BLOCK
    qi = lax.broadcasted_iota(jnp.int32, (BLOCK, BLOCK), 0)
    kj = lax.broadcasted_iota(jnp.int32, (BLOCK, BLOCK), 1)
    zero = jnp.zeros((BLOCK, BLOCK), F32)
    tri_prev = jnp.where(kj >= qi, zero, NEG_INF)
    tri_next = jnp.where(kj <= qi, zero, NEG_INF)
    combos = [(kh, par) for kh in range(KV_HEADS) for par in range(2)]
    for idx, (kh, par) in enumerate(combos):
        kx_scr[idx] = _place_kv_head(k_ref[0, 0:CTX, :], kh, par)
        vx_scr[idx] = _place_kv_head(v_ref[0, 0:CTX, :], kh, par)

    def block(nb):
        r0 = pl.multiple_of(CTX + nb * BLOCK, BLOCK)
        rn = pl.multiple_of(CTX + jnp.minimum(nb + 1, nblk - 1) * BLOCK, BLOCK)
        rp = pl.multiple_of(CTX - BLOCK + nb * BLOCK, BLOCK)
        kloc = jnp.concatenate([k_ref[0, pl.ds(rp, 2 * BLOCK), :], k_ref[0, pl.ds(rn, BLOCK), :]], axis=0)
        vloc = jnp.concatenate([v_ref[0, pl.ds(rp, 2 * BLOCK), :], v_ref[0, pl.ds(rn, BLOCK), :]], axis=0)
        bias_p = jnp.where(nb > 0, tri_prev, NEG_INF)
        bias_n = jnp.where(nb < nblk - 1, tri_next, NEG_INF)
        q = q_ref[0, pl.ds(r0, BLOCK), :]
        q2s = [jnp.concatenate([q[:, (2 * kh) * LANES:(2 * kh + 1) * LANES],
                                q[:, (2 * kh + 1) * LANES:(2 * kh + 2) * LANES]], axis=0)
               for kh in range(KV_HEADS)]
        sc, ee, oo = {}, {}, {}

        def scores(idx):
            kh, par = combos[idx]
            sc[idx] = (_dot_nt(q2s[kh], _place_kv_head(kloc, kh, par)),
                       _dot_nt(q2s[kh], kx_scr[idx]))

        def softmax(idx):
            kh, par = combos[idx]
            s_loc, s_ctx = sc[idx]
            es, inv = [], []
            for r in range(2):
                head = 4 * kh + par + 2 * r
                rs = slice(r * BLOCK, (r + 1) * BLOCK)
                sv = jnp.concatenate([s_loc[rs, 0:BLOCK] + bias_p, s_loc[rs, BLOCK:2 * BLOCK],
                                      s_loc[rs, 2 * BLOCK:3 * BLOCK] + bias_n, s_ctx[rs]], axis=1)
                sk = sink_ref[head] * LOG2E
                m = jnp.maximum(jnp.max(sv, axis=-1, keepdims=True), sk)
                e = jnp.exp2(sv - m)
                inv.append(1.0 / (jnp.sum(e, axis=-1, keepdims=True) + jnp.exp2(sk - m)))
                es.append(e.astype(BF16))
            ee[idx] = (jnp.concatenate(es, axis=0), inv)

        def pv(idx):
            kh, par = combos[idx]
            e16, inv = ee[idx]
            o2 = _dot(e16[:, 0:3 * BLOCK], _place_kv_head(vloc, kh, par)) + _dot(e16[:, 3 * BLOCK:], vx_scr[idx])
            oo[idx] = [o2[r * BLOCK:(r + 1) * BLOCK] * inv[r] for r in range(2)]

        def finish():
            for kh in range(KV_HEADS):
                for r in range(2):
                    c = 2 * kh + r
                    val = oo[combos.index((kh, 0))][r] + oo[combos.index((kh, 1))][r]
                    o_ref[0, pl.ds(pl.multiple_of(nb * BLOCK, BLOCK), BLOCK), c * LANES:(c + 1) * LANES] = val.astype(BF16)

        stage = lambda f, idx: (lambda: f(idx))
        return [stage(scores, 0), stage(scores, 1), stage(softmax, 0), stage(scores, 2), stage(softmax, 1),
                stage(pv, 0), stage(scores, 3), stage(softmax, 2), stage(pv, 1), stage(softmax, 3), stage(pv, 2),
                stage(pv, 3), finish]

    def two_blocks(i, carry):
        first, second = block(2 * i), block(2 * i + 1)
        lag = 3 * len(first) // 4
        for k in range(len(first) + lag):
            if k < len(first):
                first[k]()
            if 0 <= k - lag < len(second):
                second[k - lag]()
        return carry

    assert nblk % 2 == 0
    lax.fori_loop(0, nblk // 2, two_blocks, 0)


def _attention(q, kk, vv, sink):
    b, tot = q.shape[0], q.shape[1]
    seq = tot - CTX
    whole = lambda w: pl.BlockSpec((1, tot, w), lambda bb: (bb, 0, 0))
    return pl.pallas_call(
        _attn_kernel,
        grid=(b,),
        in_specs=[pl.BlockSpec(memory_space=pltpu.SMEM), whole(ATTN_WIDTH), whole(2 * LANES), whole(2 * LANES)],
        out_specs=pl.BlockSpec((1, seq, ATTN_WIDTH), lambda bb: (bb, 0, 0)),
        out_shape=jax.ShapeDtypeStruct((b, seq, ATTN_WIDTH), BF16),
        scratch_shapes=[pltpu.VMEM((2 * KV_HEADS, CTX, LANES), BF16), pltpu.VMEM((2 * KV_HEADS, CTX, LANES), BF16)],
        compiler_params=_params("arbitrary"),
        name="attn",
    )(sink, q, kk, vv)


def _ssd_selectors():
    nh2 = 2 * SSM_HEADS
    npairs = SSM_HEADS // 2
    k = np.arange(LANES)
    part, q = k // (2 * nh2), k % (2 * nh2)
    live = part < 3
    col = np.arange(nh2 * LANES)
    tile, lane = col // LANES, col % LANES
    kind, blk = tile // (nh2 // 2), tile % (nh2 // 2)
    src = kind * nh2 + (blk // npairs) * SSM_HEADS + 2 * (blk % npairs) + (lane >= SSM_HEAD_DIM)
    pairsel = live[:, None] & (q[:, None] == src[None, :])
    segind = live[:, None] & (q[:, None] < nh2) & (q[:, None] == tile[None, :])
    return jnp.asarray(pairsel, BF16), jnp.asarray(segind, BF16)


def _ssd_kernel(x_ref, dt_ref, z_ref, cw_ref, cb_ref, dtb_ref, alog_ref, dsk_ref, g_ref, tri_ref, shift_ref,
                pairsel_ref, segind_ref,
                y_ref, hf_scr, hb_scr, yacc_scr, stb_scr, decb_scr, eb_scr, cbuf_scr, segrhs_scr, cols3_scr, yt_scr):
    tot = x_ref.shape[1]
    nchunks = tot // CHUNK
    nctx = CTX // CHUNK
    npairs = SSM_HEADS // 2
    pairs_per_group = npairs // SSM_GROUPS
    group_cols = pairs_per_group * LANES

    hf_scr[...] = jnp.zeros_like(hf_scr)
    hb_scr[...] = jnp.zeros_like(hb_scr)
    segrhs_scr[...] = segind_ref[...]

    nh2 = 2 * SSM_HEADS

    def prep(chunk):
        c0 = pl.multiple_of(chunk * CHUNK, CHUNK)
        dtv = _softplus(dt_ref[0, pl.ds(c0, CHUNK), :].T[0:nh2, :] + dtb_ref[...])
        a = dtv * (-jnp.exp(alog_ref[...]) * LOG2E)

        def split3(v):
            hi = v.astype(BF16)
            r1 = v - hi.astype(F32)
            mid = r1.astype(BF16)
            return hi, mid, (r1 - mid.astype(F32)).astype(BF16)

        parts = jnp.concatenate(split3(a), axis=0)
        fold = lambda p: p[0:nh2] + p[nh2:2 * nh2] + p[2 * nh2:3 * nh2]
        prefix = fold(_dot(parts, tri_ref[0]))
        suffix = fold(_dot(parts, tri_ref[1]))
        head_row = lax.broadcasted_iota(jnp.int32, (nh2, CHUNK), 0)
        ac_t = jnp.where(head_row < SSM_HEADS, prefix, suffix)

        v32 = [p.astype(F32) for p in split3(jnp.concatenate([ac_t, dtv], axis=0))]
        ones3 = (lax.broadcasted_iota(jnp.int32, (CHUNK - 6 * nh2, CHUNK), 0) < 3).astype(F32)
        cols3_scr[...] = jnp.concatenate(v32 + [ones3], axis=0).T.astype(BF16)
        for q in range(nh2):
            neg = [-p[q:q + 1, :] for p in v32]
            tile = jnp.concatenate(neg + [jnp.zeros((BF16_SUBLANES - 3, CHUNK), F32)], axis=0)
            segrhs_scr[6 * nh2:6 * nh2 + BF16_SUBLANES, q * CHUNK:(q + 1) * CHUNK] = tile.astype(BF16)

    prep(0)

    def chunk_pass(chunk, carry):
        lc = jnp.maximum(chunk - nctx, 0)
        r0 = pl.multiple_of(lc * CHUNK, CHUNK)
        c0 = pl.multiple_of(chunk * CHUNK, CHUNK)
        p0 = pl.multiple_of(jnp.maximum(chunk * CHUNK - HALO, 0), HALO)
        n0 = pl.multiple_of(jnp.minimum(chunk * CHUNK + CHUNK, tot - HALO), HALO)

        cols3 = cols3_scr[...]
        x16 = x_ref[0, pl.ds(c0, CHUNK), :]
        has_prev = (chunk != 0) & (chunk != nctx)
        has_next = (chunk != nctx - 1) & (chunk != nchunks - 1)
        halo_p = x_ref[0, pl.ds(p0, HALO), :]
        halo_n = x_ref[0, pl.ds(n0, HALO), :]
        xext = jnp.concatenate([jnp.where(has_prev, halo_p, jnp.zeros_like(halo_p)), x16,
                                jnp.where(has_next, halo_n, jnp.zeros_like(halo_n))], axis=0)
        shifted = _dot(shift_ref[...], xext)
        pairs = _dot(cols3, pairsel_ref[...])
        seg_all = _dot(cols3, segrhs_scr[...])

        u = (cw_ref[0:1, :] * shifted[0:CHUNK] + cw_ref[1:2, :] * x16.astype(F32)
             + cw_ref[2:3, :] * shifted[CHUNK:2 * CHUNK] + cb_ref[...])
        u = u * _sigmoid(u)
        xs = u[:, 0:SSM_WIDTH]
        bmat = u[:, SSM_WIDTH:SSM_WIDTH + SSM_GROUPS * D_STATE]
        cmat = u[:, SSM_WIDTH + SSM_GROUPS * D_STATE:XBC_WIDTH]
        cb16 = [cmat[:, g * D_STATE:(g + 1) * D_STATE].astype(BF16) for g in range(SSM_GROUPS)]
        bb16 = [bmat[:, g * D_STATE:(g + 1) * D_STATE].astype(BF16) for g in range(SSM_GROUPS)]
        bt16 = [bmat[:, g * D_STATE:(g + 1) * D_STATE].T.astype(BF16) for g in range(SSM_GROUPS)]
        cbuf_scr[lc] = jnp.concatenate(cb16, axis=1)

        hf_start = [hf_scr[:, g * group_cols:(g + 1) * group_cols] for g in range(SSM_GROUPS)]
        gm2 = _dot_nt(jnp.concatenate(cb16, axis=0), jnp.concatenate(bb16, axis=0))
        gmat = [gm2[g * CHUNK:(g + 1) * CHUNK, g * CHUNK:(g + 1) * CHUNK] for g in range(SSM_GROUPS)]
        yoff = [_dot(cb16[g], hf_start[g].astype(BF16)) for g in range(SSM_GROUPS)]

        half0 = lax.broadcasted_iota(jnp.int32, (CHUNK, LANES), 1) < SSM_HEAD_DIM
        ii = lax.broadcasted_iota(jnp.int32, (CHUNK, CHUNK), 0)
        jj = lax.broadcasted_iota(jnp.int32, (CHUNK, CHUNK), 1)
        causal = [jj <= ii, jj >= ii]
        last = [CHUNK - 1, 0]

        arg_pair, tot_pair, xdt_pair, xdtd_pair = [], [], [], []
        for d in range(2):
            args, tots, xdts, xdtds = [], [], [], []
            for m in range(npairs):
                blk = d * npairs + m
                arg = pairs[:, blk * LANES:(blk + 1) * LANES]
                dtp = pairs[:, (nh2 // 2 + blk) * LANES:(nh2 // 2 + blk + 1) * LANES]
                total = arg[last[d]:last[d] + 1, :]
                xdt = xs[:, m * LANES:(m + 1) * LANES] * dtp
                args.append(arg)
                tots.append(total)
                xdts.append(xdt)
                xdtds.append(xdt * jnp.exp2(total - arg))
            arg_pair.append(args)
            tot_pair.append(tots)
            xdt_pair.append(xdts)
            xdtd_pair.append(xdtds)

        for g in range(SSM_GROUPS):
            pr = range(g * pairs_per_group, (g + 1) * pairs_per_group)
            gs = slice(g * group_cols, (g + 1) * group_cols)
            st_f = _dot(bt16[g], jnp.concatenate([xdtd_pair[0][m] for m in pr], axis=1).astype(BF16))
            st_b = _dot(bt16[g], jnp.concatenate([xdtd_pair[1][m] for m in pr], axis=1).astype(BF16))
            dec_f = jnp.concatenate([jnp.exp2(tot_pair[0][m]) for m in pr], axis=1)
            dec_b = jnp.concatenate([jnp.exp2(tot_pair[1][m]) for m in pr], axis=1)
            hf_scr[:, gs] = hf_start[g] * dec_f + st_f
            stb_scr[chunk, :, gs] = st_b
            decb_scr[chunk, :, gs] = jnp.broadcast_to(dec_b, (8, group_cols))

        prep(jnp.minimum(chunk + 1, nchunks - 1))
        operands = {}

        def intra_operands(m):
            g = m // pairs_per_group
            smats, rhs = [], []
            for d in range(2):
                for hh in range(2):
                    r = d * SSM_HEADS + 2 * m + hh
                    seg = seg_all[:, r * CHUNK:(r + 1) * CHUNK]
                    smats.append(jnp.where(causal[d], gmat[g] * jnp.exp2(seg), 0.0).astype(BF16))
                xd = xdt_pair[d][m]
                rhs += [jnp.where(half0, xd, 0.0), jnp.where(half0, 0.0, xd)]
            operands[m] = (jnp.concatenate(smats, axis=1), jnp.concatenate(rhs, axis=0).astype(BF16))

        def intra_matmul(m0):
            lhs = jnp.concatenate([operands[m0][0], operands[m0 + 1][0]], axis=1)
            zero = jnp.zeros_like(operands[m0][1])
            rhs = jnp.concatenate([jnp.concatenate([operands[m0][1], zero], axis=1),
                                   jnp.concatenate([zero, operands[m0 + 1][1]], axis=1)], axis=0)
            ydiag = _dot(lhs, rhs)
            for k in range(2):
                m = m0 + k
                g, mm = m // pairs_per_group, m % pairs_per_group
                ms = slice(m * LANES, (m + 1) * LANES)
                yacc_scr[pl.ds(r0, CHUNK), ms] = (ydiag[:, k * LANES:(k + 1) * LANES]
                                                  + yoff[g][:, mm * LANES:(mm + 1) * LANES] * jnp.exp2(arg_pair[0][m])
                                                  + dsk_ref[:, ms] * xs[:, ms])
                eb_scr[lc, :, ms] = jnp.exp2(arg_pair[1][m])

        assert npairs == 4
        intra_operands(0)
        intra_operands(1)
        intra_operands(2)
        intra_matmul(0)
        intra_operands(3)
        intra_matmul(2)
        return carry

    lax.fori_loop(0, nchunks, chunk_pass, 0)

    yt_scr[...] = jnp.zeros_like(yt_scr)
    order = lambda i: jnp.where(i < nctx, nctx - 1 - i, nchunks + nctx - 1 - i)

    def carry_pass(i, carry):
        chunk = order(jnp.minimum(i, nchunks - 1))
        lc = jnp.maximum(chunk - nctx, 0)
        r0 = pl.multiple_of(lc * CHUNK, CHUNK)
        hb = hb_scr[...]
        cb = cbuf_scr[lc]
        yoff = jnp.concatenate(
            [_dot(cb[:, g * D_STATE:(g + 1) * D_STATE], hb[:, g * group_cols:(g + 1) * group_cols].astype(BF16))
             for g in range(SSM_GROUPS)], axis=1)

        prev = order(jnp.maximum(i - 1, 0))
        pr0 = pl.multiple_of(jnp.maximum(prev - nctx, 0) * CHUNK, CHUNK)
        zf = z_ref[0, pl.ds(pl.multiple_of(prev * CHUNK, CHUNK), CHUNK), :].astype(F32)
        yt = yt_scr[...] * (zf * _sigmoid(zf))
        gw = SSM_WIDTH // SSM_GROUPS
        for g in range(SSM_GROUPS):
            yg = yt[:, g * gw:(g + 1) * gw]
            ms = jnp.mean(yg * yg, axis=-1, keepdims=True)
            y_ref[0, pl.ds(pr0, CHUNK), g * gw:(g + 1) * gw] = (yg * lax.rsqrt(ms + EPS)
                                                                * g_ref[:, g * gw:(g + 1) * gw]).astype(BF16)

        yt_scr[...] = yacc_scr[pl.ds(r0, CHUNK), :] + yoff * eb_scr[lc]
        hb_scr[...] = hb * decb_scr[chunk, 0:1, :] + stb_scr[chunk]
        return carry

    lax.fori_loop(0, nchunks + 1, carry_pass, 0)


def _ssd(xbc, dt, z, conv_w, conv_b, dtb16, alog16, dsk, g_ssm):
    b, tot = xbc.shape[0], xbc.shape[1]
    seq = tot - CTX
    nchunks = tot // CHUNK
    nctx = CTX // CHUNK
    nlat = nchunks - nctx
    const = lambda *shape: pl.BlockSpec(shape, lambda bb: (0,) * len(shape))
    whole = lambda rows, w: pl.BlockSpec((1, rows, w), lambda bb: (bb, 0, 0))
    jdx = np.arange(CHUNK)
    tri = jnp.asarray(np.stack([jdx[:, None] <= jdx[None, :], jdx[:, None] >= jdx[None, :]]), BF16)
    kdx = np.arange(CHUNK + 2 * HALO)
    shift = jnp.asarray(np.concatenate([kdx[None, :] == jdx[:, None] + HALO - 1,
                                        kdx[None, :] == jdx[:, None] + HALO + 1], axis=0), BF16)
    return pl.pallas_call(
        _ssd_kernel,
        grid=(b,),
        in_specs=[whole(tot, XBC_WIDTH), whole(tot, LANES), whole(tot, SSM_WIDTH),
                  const(3, XBC_WIDTH), const(1, XBC_WIDTH), const(2 * SSM_HEADS, CHUNK), const(2 * SSM_HEADS, CHUNK),
                  const(1, SSM_WIDTH), const(1, SSM_WIDTH), const(2, CHUNK, CHUNK),
                  const(2 * CHUNK, CHUNK + 2 * HALO),
                  const(LANES, 2 * SSM_HEADS * LANES), const(LANES, 2 * SSM_HEADS * LANES)],
        out_specs=whole(seq, SSM_WIDTH),
        out_shape=jax.ShapeDtypeStruct((b, seq, SSM_WIDTH), BF16),
        scratch_shapes=[pltpu.VMEM((D_STATE, SSM_WIDTH), F32),
                        pltpu.VMEM((D_STATE, SSM_WIDTH), F32),
                        pltpu.VMEM((seq, SSM_WIDTH), F32),
                        pltpu.VMEM((nchunks, D_STATE, SSM_WIDTH), F32),
                        pltpu.VMEM((nchunks, 8, SSM_WIDTH), F32),
                        pltpu.VMEM((nlat, CHUNK, SSM_WIDTH), F32),
                        pltpu.VMEM((nlat, CHUNK, SSM_GROUPS * D_STATE), BF16),
                        pltpu.VMEM((LANES, 2 * SSM_HEADS * LANES), BF16),
                        pltpu.VMEM((CHUNK, LANES), BF16),
                        pltpu.VMEM((CHUNK, SSM_WIDTH), F32)],
        compiler_params=_params("arbitrary"),
        name="ssd",
    )(xbc, dt, z, conv_w, conv_b, dtb16, alog16, dsk, g_ssm, tri, shift, *_ssd_selectors())


def _mixffn_kernel(am_ref, ap_ref, an_ref, ym_ref, yp_ref, yn_ref, xm_ref, xp_ref, xn_ref,
                   wo_ref, wup_ref, cw_ref, cb_ref, wd_ref, mod_ref, g_ref, o_ref):
    i = pl.program_id(1)
    tm = am_ref.shape[1]
    nrows = tm + 2 * HALO
    ext = lambda p, m, n: jnp.concatenate([p[0], m[0], n[0]], axis=0)
    a_ext = ext(ap_ref, am_ref, an_ref)
    y_ext = ext(yp_ref, ym_ref, yn_ref)
    x_ext = ext(xp_ref, xm_ref, xn_ref)
    mod = mod_ref[0]
    lo = jnp.where(i > 0, 0, HALO)
    hi = jnp.where(i < pl.num_programs(1) - 1, nrows, nrows - HALO)
    half = nrows // 2
    halves = [slice(k * half, (k + 1) * half) for k in range(2)]
    proj = [_dot(a_ext[rs], wo_ref[0:ATTN_WIDTH, :]) + _dot(y_ext[rs], wo_ref[ATTN_WIDTH:D_MODEL, :])
            for rs in halves]
    xparts, lparts = [], []
    for k, rs in enumerate(halves):
        xk = x_ext[rs] + mod[:, 2 * D_MODEL:3 * D_MODEL] * proj[k]
        ms = jnp.mean(xk * xk, axis=-1, keepdims=True)
        hn = xk * lax.rsqrt(ms + EPS) * g_ref[...]
        h2 = hn * (1.0 + mod[:, 4 * D_MODEL:5 * D_MODEL]) + mod[:, 3 * D_MODEL:4 * D_MODEL]
        row = lax.broadcasted_iota(jnp.int32, (half, 1), 0) + k * half
        lparts.append(jnp.where(row >= lo, jnp.where(row < hi, h2, 0.0), 0.0).astype(BF16))
        xparts.append(xk)
    xnew = jnp.concatenate(xparts, axis=0)
    lhs = jnp.concatenate(lparts, axis=0)

    chunks = [(c0, min(c0 + TF_FFN, D_FF)) for c0 in range(0, D_FF, TF_FFN)]
    ups, parts = {}, []

    def up_proj(k):
        c0, c1 = chunks[k]
        ups[k] = (_dot(lhs, wup_ref[:, c0:c1]), _dot(lhs, wup_ref[:, D_FF + c0:D_FF + c1]))

    def conv(uu, c0, c1):
        up = pltpu.roll(uu, 1, 0)[HALO:HALO + tm]
        dn = pltpu.roll(uu, nrows - 1, 0)[HALO:HALO + tm]
        return (cw_ref[0:1, c0:c1] * up + cw_ref[1:2, c0:c1] * uu[HALO:HALO + tm]
                + cw_ref[2:3, c0:c1] * dn + cb_ref[:, c0:c1])

    def gate_down(k):
        c0, c1 = chunks[k]
        ua = conv(ups[k][0], c0, c1)
        ug = conv(ups[k][1], D_FF + c0, D_FF + c1)
        act = (ua * (ug * _sigmoid(ug))).astype(BF16)
        parts.append(_dot(act, wd_ref[c0:c1, :]))

    up_proj(0)
    for k in range(1, len(chunks)):
        up_proj(k)
        gate_down(k - 1)
    gate_down(len(chunks) - 1)
    acc = parts[0]
    for part in parts[1:]:
        acc = acc + part
    o_ref[0] = xnew[HALO:HALO + tm] + mod[:, 5 * D_MODEL:6 * D_MODEL] * acc


def _mixffn(attn, y, x, w_out16, w_up16, conv_w, conv_b, w_down16, mod3, g_ffn):
    b, seq = x.shape[0], x.shape[1]
    ni = seq // TM_FFN
    hpt = TM_FFN // HALO
    nhalo = seq // HALO
    main = lambda w: pl.BlockSpec((1, TM_FFN, w), lambda bb, i: (bb, i, 0))
    prev = lambda w: pl.BlockSpec((1, HALO, w), lambda bb, i: (bb, jnp.maximum(i * hpt - 1, 0), 0))
    nxt = lambda w: pl.BlockSpec((1, HALO, w), lambda bb, i: (bb, jnp.minimum(i * hpt + hpt, nhalo - 1), 0))
    resident = lambda *shape: pl.BlockSpec(shape, lambda bb, i: (0,) * len(shape), pipeline_mode=pl.Buffered(1))
    trio = lambda w: [main(w), prev(w), nxt(w)]
    return pl.pallas_call(
        _mixffn_kernel,
        grid=(b, ni),
        in_specs=trio(ATTN_WIDTH) + trio(SSM_WIDTH) + trio(D_MODEL) + [
            resident(D_MODEL, D_MODEL), resident(D_MODEL, 2 * D_FF), resident(3, 2 * D_FF), resident(1, 2 * D_FF),
            resident(D_FF, D_MODEL),
            pl.BlockSpec((1, 1, 6 * D_MODEL), lambda bb, i: (bb, 0, 0)),
            resident(1, D_MODEL)],
        out_specs=main(D_MODEL),
        out_shape=jax.ShapeDtypeStruct((b, seq, D_MODEL), F32),
        compiler_params=_params("arbitrary", "arbitrary"),
        name="mixffn",
    )(attn, attn, attn, y, y, y, x, x, x, w_out16, w_up16, conv_w, conv_b.reshape(1, -1), w_down16, mod3,
      g_ffn.reshape(1, -1))


def _layer(x, c, ctx, c_ctx, w_mod, b_mod, g_mix, w_in, g_q, g_k, sink, ssm_conv_w, ssm_conv_b,
           a_log, dt_bias, d_skip, g_ssm, w_out, g_ffn, w_up, ffn_conv_w, ffn_conv_b, w_down):
    mod3 = _modulation(c, c_ctx, w_mod, b_mod)

    q, kk, vv, z, xbc, dt = _inproj(x, ctx, mod3, g_mix, w_in, g_q, g_k)
    attn = _attention(q, kk, vv, sink)

    head_rows = lambda t: jnp.broadcast_to(t.reshape(2 * SSM_HEADS, 1), (2 * SSM_HEADS, CHUNK))
    y = _ssd(xbc, dt, z, ssm_conv_w, ssm_conv_b.reshape(1, -1), head_rows(dt_bias), head_rows(a_log),
             jnp.repeat(d_skip, SSM_HEAD_DIM).reshape(1, -1), g_ssm.reshape(1, -1))

    return _mixffn(attn, y, x, w_out.astype(BF16), w_up.astype(BF16), ffn_conv_w, ffn_conv_b, w_down.astype(BF16),
                   mod3, g_ffn)


def kernel(x, c, ctx, c_ctx, w_mod, b_mod, g_mix, w_in, g_q, g_k, sink, ssm_conv_w, ssm_conv_b,
           a_log, dt_bias, d_skip, g_ssm, w_out, g_ffn, w_up, ffn_conv_w, ffn_conv_b, w_down):
    assert w_mod.shape[0] == 1, "single-layer block"
    first = lambda t: t.reshape(t.shape[1:])
    return _layer(x, c, ctx, c_ctx, first(w_mod), first(b_mod), first(g_mix), first(w_in), first(g_q),
                  first(g_k), first(sink), first(ssm_conv_w), first(ssm_conv_b), first(a_log),
                  first(dt_bias), first(d_skip), first(g_ssm), first(w_out), first(g_ffn), first(w_up),
                  first(ffn_conv_w), first(ffn_conv_b), first(w_down))
```

```python
import jax
import jax.numpy as jnp
import numpy as np
from jax import lax
from jax.experimental import pallas as pl
from jax.experimental.pallas import tpu as pltpu

F32 = jnp.float32
BF16 = jnp.bfloat16

D_MODEL = 1024
CTX = 256
GRID_W = 64
HEAD_DIM = 64
ATTN_HEADS = 8
KV_HEADS = 2
ATTN_WIDTH = ATTN_HEADS * HEAD_DIM
KV_WIDTH = KV_HEADS * HEAD_DIM
WINDOW = 128
BLOCK = 128
ROPE_BASE = 10000.0
SSM_WIDTH = D_MODEL - ATTN_WIDTH
SSM_HEAD_DIM = 64
SSM_HEADS = SSM_WIDTH // SSM_HEAD_DIM
SSM_GROUPS = 2
D_STATE = 128
XBC_WIDTH = SSM_WIDTH + 2 * SSM_GROUPS * D_STATE
CHUNK = 128
D_FF = 2816
IN_COLS = ATTN_WIDTH + 2 * KV_WIDTH + SSM_WIDTH + XBC_WIDTH + 2 * SSM_HEADS
EPS = 1e-6
NEG_INF = -1e30
LOG2E = 1.4426950408889634

LANES = 128
BF16_SUBLANES = 16
IN_COLS_PAD = 19 * LANES
COL_Q, COL_K, COL_V = 0, ATTN_WIDTH, ATTN_WIDTH + KV_WIDTH
COL_Z = ATTN_WIDTH + 2 * KV_WIDTH
COL_XBC = COL_Z + SSM_WIDTH
COL_DT = COL_XBC + XBC_WIDTH
VMEM_LIMIT = 56 * 1024 * 1024

TM_IN = 3 * CTX
TM_FFN = 512
TF_FFN = 1024
HALO = BF16_SUBLANES


def _params(*sem):
    return pltpu.CompilerParams(dimension_semantics=sem, vmem_limit_bytes=VMEM_LIMIT)


def _sigmoid(v):
    return 1.0 / (1.0 + jnp.exp2(v * (-LOG2E)))


def _softplus(v):
    return jnp.maximum(v, 0.0) + jnp.log1p(jnp.exp(-jnp.abs(v)))


def _dot(a, b):
    return jnp.dot(a, b, preferred_element_type=F32)


def _dot_nt(a, b):
    return lax.dot_general(a, b, (((1,), (1,)), ((), ())), preferred_element_type=F32)


def _mod_kernel(c_ref, cctx_ref, w_ref, b_ref, o_ref):
    rows = o_ref.shape[0]
    b = c_ref.shape[0]
    cv = jnp.concatenate([c_ref[...], cctx_ref[...], jnp.zeros((rows - b - 1, D_MODEL), F32)], axis=0)
    s = (cv * _sigmoid(cv)).astype(BF16)
    res = _dot(s, w_ref[...].astype(BF16)) + b_ref[...]
    for r in range(b + 1):
        o_ref[r] = res[r:r + 1, :]
    o_ref[b + 1:rows] = jnp.zeros((rows - b - 1, 1, res.shape[1]), F32)


def _modulation(c, c_ctx, w_mod, b_mod):
    b = c.shape[0]
    rows = 8 * ((b + 1 + 7) // 8)
    tn = 1024
    return pl.pallas_call(
        _mod_kernel,
        grid=(6 * D_MODEL // tn,),
        in_specs=[pl.BlockSpec((b, D_MODEL), lambda j: (0, 0)),
                  pl.BlockSpec((1, D_MODEL), lambda j: (0, 0)),
                  pl.BlockSpec((D_MODEL, tn), lambda j: (0, j)),
                  pl.BlockSpec((1, tn), lambda j: (0, j))],
        out_specs=pl.BlockSpec((rows, 1, tn), lambda j: (0, 0, j)),
        out_shape=jax.ShapeDtypeStruct((rows, 1, 6 * D_MODEL), F32),
        compiler_params=_params("arbitrary"),
        name="mod",
    )(c, c_ctx.reshape(1, -1), w_mod, b_mod.reshape(1, -1))


def _inproj_kernel(xa_ref, xb_ref, xc_ref, ctx_ref, modb_ref, modc_ref, gmix_ref, w_ref, gq_ref, gk_ref,
                   cos_ref, sina_ref, sinb_ref, bd_ref,
                   q_ref, k_ref, v_ref, z_ref, xbc_ref, dt_ref, w16_scr):
    first = pl.program_id(1) == 0

    @pl.when(first & (pl.program_id(0) == 0))
    def _():
        w16_scr[:, 0:IN_COLS] = w_ref[...].astype(BF16)
        w16_scr[:, IN_COLS:IN_COLS_PAD] = jnp.zeros((D_MODEL, IN_COLS_PAD - IN_COLS), BF16)

    def normed(xin, mod):
        ms = jnp.mean(xin * xin, axis=-1, keepdims=True)
        hn = xin * lax.rsqrt(ms + EPS) * gmix_ref[...]
        return (hn * (1.0 + mod[:, D_MODEL:2 * D_MODEL]) + mod[:, 0:D_MODEL]).astype(BF16)

    modb = modb_ref[0]
    bd = bd_ref[...]

    def project(h, rs, emit_between):
        cos, sina, sinb = cos_ref[rs, :], sina_ref[rs, :], sinb_ref[rs, :]

        def norm_rope(pf, gain):
            ss = pf * pf
            hi = ss.astype(BF16)
            lo = (ss - hi.astype(F32)).astype(BF16)
            sums = _dot(jnp.concatenate([hi, lo], axis=1), bd)
            y = pf * lax.rsqrt(sums * (1.0 / HEAD_DIM) + EPS) * gain
            return y * cos + pltpu.roll(y, LANES - 16, 1) * sina + pltpu.roll(y, 16, 1) * sinb

        def q_chunks(lo, hi):
            for cidx in range(lo, hi):
                cs = slice(cidx * LANES, (cidx + 1) * LANES)
                q_ref[0, rs, cs] = norm_rope(qf[:, cs], gq_ref[...]).astype(BF16)

        qf = _dot(h, w16_scr[:, COL_Q:COL_Q + ATTN_WIDTH])
        xf = _dot(h, w16_scr[:, COL_XBC:COL_XBC + XBC_WIDTH])
        between = emit_between()
        q_chunks(0, 2)
        kvf = _dot(h, w16_scr[:, COL_K:COL_K + 2 * KV_WIDTH])
        q_chunks(2, ATTN_WIDTH // LANES)
        zf = _dot(h, w16_scr[:, COL_Z:COL_Z + SSM_WIDTH])
        dtf = _dot(h, w16_scr[:, COL_DT:COL_DT + LANES])
        kf = norm_rope(kvf[:, 0:KV_WIDTH], gk_ref[...])
        k_ref[0, rs, 0:LANES] = kf.astype(BF16)
        k_ref[0, rs, LANES:2 * LANES] = pltpu.roll(kf, HEAD_DIM, 1).astype(BF16)
        vf = kvf[:, KV_WIDTH:2 * KV_WIDTH]
        v_ref[0, rs, 0:LANES] = vf.astype(BF16)
        v_ref[0, rs, LANES:2 * LANES] = pltpu.roll(vf, HEAD_DIM, 1).astype(BF16)
        z_ref[0, rs, :] = zf.astype(BF16)
        xbc_ref[0, rs, :] = xf.astype(BF16)
        dt_ref[0, rs, :] = dtf
        return between

    h_head = jnp.concatenate([normed(jnp.where(first, ctx_ref[0], xa_ref[0]), jnp.where(first, modc_ref[0], modb)),
                              normed(xb_ref[0], modb)], axis=0)
    h_tail = project(h_head, slice(0, 2 * CTX), lambda: normed(xc_ref[0], modb))
    project(h_tail, slice(2 * CTX, 3 * CTX), lambda: None)


def _rope_tables(seq):
    rows = seq // GRID_W
    pos_r = np.repeat(np.arange(rows), GRID_W).astype(np.float32)
    pos_c = np.tile(np.arange(GRID_W), rows).astype(np.float32)
    quarter = HEAD_DIM // 4
    freqs = (ROPE_BASE ** (-np.arange(quarter, dtype=np.float32) / quarter)).astype(np.float32)
    ang_r = pos_r[:, None] * freqs[None, :]
    ang_c = pos_c[:, None] * freqs[None, :]
    cr, sr, cc, sc = np.cos(ang_r), np.sin(ang_r), np.cos(ang_c), np.sin(ang_c)
    zero = np.zeros_like(sr)
    cos = np.concatenate([cr, cr, cc, cc], axis=-1)
    sina = np.concatenate([-sr, zero, -sc, zero], axis=-1)
    sinb = np.concatenate([zero, sr, zero, sc], axis=-1)
    reps = LANES // HEAD_DIM
    pad = lambda t, v: np.concatenate([np.full((CTX, LANES), v, np.float32), np.tile(t, (1, reps))], axis=0)
    return tuple(jnp.asarray(pad(t, v), F32) for t, v in ((cos, 1.0), (sina, 0.0), (sinb, 0.0)))


def _inproj(x, ctx, mod3, g_mix, w_in, g_q, g_k):
    b, seq = x.shape[0], x.shape[1]
    tot = CTX + seq
    cos, sina, sinb = _rope_tables(seq)
    reps = LANES // HEAD_DIM
    gq = (jnp.tile(g_q, reps) * (HEAD_DIM ** -0.5 * LOG2E)).reshape(1, LANES)
    gk = jnp.tile(g_k, reps).reshape(1, LANES)
    lane = np.arange(LANES)
    bd = lane[:, None] // HEAD_DIM == lane[None, :] // HEAD_DIM
    bd = jnp.asarray(np.concatenate([bd, bd], axis=0), BF16)
    nsteps = tot // TM_IN
    sub = TM_IN // CTX
    const = lambda *shape: pl.BlockSpec(shape, lambda bb, s: (0,) * len(shape))
    tab = pl.BlockSpec((TM_IN, LANES), lambda bb, s: (s, 0))
    out = lambda w: pl.BlockSpec((1, TM_IN, w), lambda bb, s: (bb, s, 0))
    shp = lambda w, dt: jax.ShapeDtypeStruct((b, tot, w), dt)
    xsub = lambda j: pl.BlockSpec((1, CTX, D_MODEL), lambda bb, s: (bb, jnp.maximum(sub * s + j - 1, 0), 0))
    return pl.pallas_call(
        _inproj_kernel,
        grid=(b, nsteps),
        in_specs=[xsub(0), xsub(1), xsub(2),
                  pl.BlockSpec((1, CTX, D_MODEL), lambda bb, s: (bb, 0, 0)),
                  pl.BlockSpec((1, 1, 6 * D_MODEL), lambda bb, s: (bb, 0, 0)),
                  pl.BlockSpec((1, 1, 6 * D_MODEL), lambda bb, s: (b, 0, 0)),
                  const(1, D_MODEL),
                  pl.BlockSpec((D_MODEL, IN_COLS), lambda bb, s: (0, 0), pipeline_mode=pl.Buffered(1)),
                  const(1, LANES), const(1, LANES),
                  tab, tab, tab, const(2 * LANES, LANES)],
        out_specs=[out(ATTN_WIDTH), out(2 * LANES), out(2 * LANES), out(SSM_WIDTH), out(XBC_WIDTH), out(LANES)],
        out_shape=[shp(ATTN_WIDTH, BF16), shp(2 * LANES, BF16), shp(2 * LANES, BF16),
                   shp(SSM_WIDTH, BF16), shp(XBC_WIDTH, BF16), shp(LANES, F32)],
        scratch_shapes=[pltpu.VMEM((D_MODEL, IN_COLS_PAD), BF16)],
        compiler_params=_params("arbitrary", "arbitrary"),
        name="inproj",
    )(x, x, x, ctx, mod3, mod3, g_mix.reshape(1, -1), w_in, gq, gk, cos, sina, sinb, bd)


def _place_kv_head(t, kh, par):
    src = 0 if par == kh else 1
    lane = lax.broadcasted_iota(jnp.int32, (t.shape[0], LANES), 1)
    half = (lane >= HEAD_DIM) if par else (lane < HEAD_DIM)
    tile = t[:, src * LANES:(src + 1) * LANES]
    return jnp.where(half, tile, jnp.zeros_like(tile))


def _attn_kernel(sink_ref, q_ref, k_ref, v_ref, o_ref, kx_scr, vx_scr):
    seq = o_ref.shape[1]
    nblk = seq // BLOCK
    qi = lax.broadcasted_iota(jnp.int32, (BLOCK, BLOCK), 0)
    kj = lax.broadcasted_iota(jnp.int32, (BLOCK, BLOCK), 1)
    zero = jnp.zeros((BLOCK, BLOCK), F32)
    tri_prev = jnp.where(kj >= qi, zero, NEG_INF)
    tri_next = jnp.where(kj <= qi, zero, NEG_INF)
    combos = [(kh, par) for kh in range(KV_HEADS) for par in range(2)]
    for idx, (kh, par) in enumerate(combos):
        kx_scr[idx] = _place_kv_head(k_ref[0, 0:CTX, :], kh, par)
        vx_scr[idx] = _place_kv_head(v_ref[0, 0:CTX, :], kh, par)

    def block(nb):
        r0 = pl.multiple_of(CTX + nb * BLOCK, BLOCK)
        rn = pl.multiple_of(CTX + jnp.minimum(nb + 1, nblk - 1) * BLOCK, BLOCK)
        rp = pl.multiple_of(CTX - BLOCK + nb * BLOCK, BLOCK)
        kloc = jnp.concatenate([k_ref[0, pl.ds(rp, 2 * BLOCK), :], k_ref[0, pl.ds(rn, BLOCK), :]], axis=0)
        vloc = jnp.concatenate([v_ref[0, pl.ds(rp, 2 * BLOCK), :], v_ref[0, pl.ds(rn, BLOCK), :]], axis=0)
        bias_p = jnp.where(nb > 0, tri_prev, NEG_INF)
        bias_n = jnp.where(nb < nblk - 1, tri_next, NEG_INF)
        q = q_ref[0, pl.ds(r0, BLOCK), :]
        q2s = [jnp.concatenate([q[:, (2 * kh) * LANES:(2 * kh + 1) * LANES],
                                q[:, (2 * kh + 1) * LANES:(2 * kh + 2) * LANES]], axis=0)
               for kh in range(KV_HEADS)]
        sc, ee, oo = {}, {}, {}

        def scores(idx):
            kh, par = combos[idx]
            sc[idx] = (_dot_nt(q2s[kh], _place_kv_head(kloc, kh, par)),
                       _dot_nt(q2s[kh], kx_scr[idx]))

        def softmax(idx):
            kh, par = combos[idx]
            s_loc, s_ctx = sc[idx]
            es, inv = [], []
            for r in range(2):
                head = 4 * kh + par + 2 * r
                rs = slice(r * BLOCK, (r + 1) * BLOCK)
                sv = jnp.concatenate([s_loc[rs, 0:BLOCK] + bias_p, s_loc[rs, BLOCK:2 * BLOCK],
                                      s_loc[rs, 2 * BLOCK:3 * BLOCK] + bias_n, s_ctx[rs]], axis=1)
                sk = sink_ref[head] * LOG2E
                m = jnp.maximum(jnp.max(sv, axis=-1, keepdims=True), sk)
                e = jnp.exp2(sv - m)
                inv.append(1.0 / (jnp.sum(e, axis=-1, keepdims=True) + jnp.exp2(sk - m)))
                es.append(e.astype(BF16))
            ee[idx] = (jnp.concatenate(es, axis=0), inv)

        def pv(idx):
            kh, par = combos[idx]
            e16, inv = ee[idx]
            o2 = _dot(e16[:, 0:3 * BLOCK], _place_kv_head(vloc, kh, par)) + _dot(e16[:, 3 * BLOCK:], vx_scr[idx])
            oo[idx] = [o2[r * BLOCK:(r + 1) * BLOCK] * inv[r] for r in range(2)]

        def finish():
            for kh in range(KV_HEADS):
                for r in range(2):
                    c = 2 * kh + r
                    val = oo[combos.index((kh, 0))][r] + oo[combos.index((kh, 1))][r]
                    o_ref[0, pl.ds(pl.multiple_of(nb * BLOCK, BLOCK), BLOCK), c * LANES:(c + 1) * LANES] = val.astype(BF16)

        stage = lambda f, idx: (lambda: f(idx))
        return [stage(scores, 0), stage(scores, 1), stage(softmax, 0), stage(scores, 2), stage(softmax, 1),
                stage(pv, 0), stage(scores, 3), stage(softmax, 2), stage(pv, 1), stage(softmax, 3), stage(pv, 2),
                stage(pv, 3), finish]

    def two_blocks(i, carry):
        first, second = block(2 * i), block(2 * i + 1)
        lag = 3 * len(first) // 4
        for k in range(len(first) + lag):
            if k < len(first):
                first[k]()
            if 0 <= k - lag < len(second):
                second[k - lag]()
        return carry

    assert nblk % 2 == 0
    lax.fori_loop(0, nblk // 2, two_blocks, 0)


def _attention(q, kk, vv, sink):
    b, tot = q.shape[0], q.shape[1]
    seq = tot - CTX
    whole = lambda w: pl.BlockSpec((1, tot, w), lambda bb: (bb, 0, 0))
    return pl.pallas_call(
        _attn_kernel,
        grid=(b,),
        in_specs=[pl.BlockSpec(memory_space=pltpu.SMEM), whole(ATTN_WIDTH), whole(2 * LANES), whole(2 * LANES)],
        out_specs=pl.BlockSpec((1, seq, ATTN_WIDTH), lambda bb: (bb, 0, 0)),
        out_shape=jax.ShapeDtypeStruct((b, seq, ATTN_WIDTH), BF16),
        scratch_shapes=[pltpu.VMEM((2 * KV_HEADS, CTX, LANES), BF16), pltpu.VMEM((2 * KV_HEADS, CTX, LANES), BF16)],
        compiler_params=_params("arbitrary"),
        name="attn",
    )(sink, q, kk, vv)


def _ssd_selectors():
    nh2 = 2 * SSM_HEADS
    npairs = SSM_HEADS // 2
    k = np.arange(LANES)
    part, q = k // (2 * nh2), k % (2 * nh2)
    live = part < 3
    col = np.arange(nh2 * LANES)
    tile, lane = col // LANES, col % LANES
    kind, blk = tile // (nh2 // 2), tile % (nh2 // 2)
    src = kind * nh2 + (blk // npairs) * SSM_HEADS + 2 * (blk % npairs) + (lane >= SSM_HEAD_DIM)
    pairsel = live[:, None] & (q[:, None] == src[None, :])
    segind = live[:, None] & (q[:, None] < nh2) & (q[:, None] == tile[None, :])
    return jnp.asarray(pairsel, BF16), jnp.asarray(segind, BF16)


def _ssd_kernel(x_ref, dt_ref, z_ref, cw_ref, cb_ref, dtb_ref, alog_ref, dsk_ref, g_ref, tri_ref, shift_ref,
                pairsel_ref, segind_ref,
                y_ref, hf_scr, hb_scr, yacc_scr, stb_scr, decb_scr, eb_scr, cbuf_scr, segrhs_scr, cols3_scr, yt_scr):
    tot = x_ref.shape[1]
    nchunks = tot // CHUNK
    nctx = CTX // CHUNK
    npairs = SSM_HEADS // 2
    pairs_per_group = npairs // SSM_GROUPS
    group_cols = pairs_per_group * LANES

    hf_scr[...] = jnp.zeros_like(hf_scr)
    hb_scr[...] = jnp.zeros_like(hb_scr)
    segrhs_scr[...] = segind_ref[...]

    nh2 = 2 * SSM_HEADS

    def prep(chunk):
        c0 = pl.multiple_of(chunk * CHUNK, CHUNK)
        dtv = _softplus(dt_ref[0, pl.ds(c0, CHUNK), :].T[0:nh2, :] + dtb_ref[...])
        a = dtv * (-jnp.exp(alog_ref[...]) * LOG2E)

        def split3(v):
            hi = v.astype(BF16)
            r1 = v - hi.astype(F32)
            mid = r1.astype(BF16)
            return hi, mid, (r1 - mid.astype(F32)).astype(BF16)

        parts = jnp.concatenate(split3(a), axis=0)
        fold = lambda p: p[0:nh2] + p[nh2:2 * nh2] + p[2 * nh2:3 * nh2]
        prefix = fold(_dot(parts, tri_ref[0]))
        suffix = fold(_dot(parts, tri_ref[1]))
        head_row = lax.broadcasted_iota(jnp.int32, (nh2, CHUNK), 0)
        ac_t = jnp.where(head_row < SSM_HEADS, prefix, suffix)

        v32 = [p.astype(F32) for p in split3(jnp.concatenate([ac_t, dtv], axis=0))]
        ones3 = (lax.broadcasted_iota(jnp.int32, (CHUNK - 6 * nh2, CHUNK), 0) < 3).astype(F32)
        cols3_scr[...] = jnp.concatenate(v32 + [ones3], axis=0).T.astype(BF16)
        for q in range(nh2):
            neg = [-p[q:q + 1, :] for p in v32]
            tile = jnp.concatenate(neg + [jnp.zeros((BF16_SUBLANES - 3, CHUNK), F32)], axis=0)
            segrhs_scr[6 * nh2:6 * nh2 + BF16_SUBLANES, q * CHUNK:(q + 1) * CHUNK] = tile.astype(BF16)

    prep(0)

    def chunk_pass(chunk, carry):
        lc = jnp.maximum(chunk - nctx, 0)
        r0 = pl.multiple_of(lc * CHUNK, CHUNK)
        c0 = pl.multiple_of(chunk * CHUNK, CHUNK)
        p0 = pl.multiple_of(jnp.maximum(chunk * CHUNK - HALO, 0), HALO)
        n0 = pl.multiple_of(jnp.minimum(chunk * CHUNK + CHUNK, tot - HALO), HALO)

        cols3 = cols3_scr[...]
        x16 = x_ref[0, pl.ds(c0, CHUNK), :]
        has_prev = (chunk != 0) & (chunk != nctx)
        has_next = (chunk != nctx - 1) & (chunk != nchunks - 1)
        halo_p = x_ref[0, pl.ds(p0, HALO), :]
        halo_n = x_ref[0, pl.ds(n0, HALO), :]
        xext = jnp.concatenate([jnp.where(has_prev, halo_p, jnp.zeros_like(halo_p)), x16,
                                jnp.where(has_next, halo_n, jnp.zeros_like(halo_n))], axis=0)
        shifted = _dot(shift_ref[...], xext)
        pairs = _dot(cols3, pairsel_ref[...])
        seg_all = _dot(cols3, segrhs_scr[...])

        u = (cw_ref[0:1, :] * shifted[0:CHUNK] + cw_ref[1:2, :] * x16.astype(F32)
             + cw_ref[2:3, :] * shifted[CHUNK:2 * CHUNK] + cb_ref[...])
        u = u * _sigmoid(u)
        xs = u[:, 0:SSM_WIDTH]
        bmat = u[:, SSM_WIDTH:SSM_WIDTH + SSM_GROUPS * D_STATE]
        cmat = u[:, SSM_WIDTH + SSM_GROUPS * D_STATE:XBC_WIDTH]
        cb16 = [cmat[:, g * D_STATE:(g + 1) * D_STATE].astype(BF16) for g in range(SSM_GROUPS)]
        bb16 = [bmat[:, g * D_STATE:(g + 1) * D_STATE].astype(BF16) for g in range(SSM_GROUPS)]
        bt16 = [bmat[:, g * D_STATE:(g + 1) * D_STATE].T.astype(BF16) for g in range(SSM_GROUPS)]
        cbuf_scr[lc] = jnp.concatenate(cb16, axis=1)

        hf_start = [hf_scr[:, g * group_cols:(g + 1) * group_cols] for g in range(SSM_GROUPS)]
        gm2 = _dot_nt(jnp.concatenate(cb16, axis=0), jnp.concatenate(bb16, axis=0))
        gmat = [gm2[g * CHUNK:(g + 1) * CHUNK, g * CHUNK:(g + 1) * CHUNK] for g in range(SSM_GROUPS)]
        yoff = [_dot(cb16[g], hf_start[g].astype(BF16)) for g in range(SSM_GROUPS)]

        half0 = lax.broadcasted_iota(jnp.int32, (CHUNK, LANES), 1) < SSM_HEAD_DIM
        ii = lax.broadcasted_iota(jnp.int32, (CHUNK, CHUNK), 0)
        jj = lax.broadcasted_iota(jnp.int32, (CHUNK, CHUNK), 1)
        causal = [jj <= ii, jj >= ii]
        last = [CHUNK - 1, 0]

        arg_pair, tot_pair, xdt_pair, xdtd_pair = [], [], [], []
        for d in range(2):
            args, tots, xdts, xdtds = [], [], [], []
            for m in range(npairs):
                blk = d * npairs + m
                arg = pairs[:, blk * LANES:(blk + 1) * LANES]
                dtp = pairs[:, (nh2 // 2 + blk) * LANES:(nh2 // 2 + blk + 1) * LANES]
                total = arg[last[d]:last[d] + 1, :]
                xdt = xs[:, m * LANES:(m + 1) * LANES] * dtp
                args.append(arg)
                tots.append(total)
                xdts.append(xdt)
                xdtds.append(xdt * jnp.exp2(total - arg))
            arg_pair.append(args)
            tot_pair.append(tots)
            xdt_pair.append(xdts)
            xdtd_pair.append(xdtds)

        for g in range(SSM_GROUPS):
            pr = range(g * pairs_per_group, (g + 1) * pairs_per_group)
            gs = slice(g * group_cols, (g + 1) * group_cols)
            st_f = _dot(bt16[g], jnp.concatenate([xdtd_pair[0][m] for m in pr], axis=1).astype(BF16))
            st_b = _dot(bt16[g], jnp.concatenate([xdtd_pair[1][m] for m in pr], axis=1).astype(BF16))
            dec_f = jnp.concatenate([jnp.exp2(tot_pair[0][m]) for m in pr], axis=1)
            dec_b = jnp.concatenate([jnp.exp2(tot_pair[1][m]) for m in pr], axis=1)
            hf_scr[:, gs] = hf_start[g] * dec_f + st_f
            stb_scr[chunk, :, gs] = st_b
            decb_scr[chunk, :, gs] = jnp.broadcast_to(dec_b, (8, group_cols))

        prep(jnp.minimum(chunk + 1, nchunks - 1))
        operands = {}

        def intra_operands(m):
            g = m // pairs_per_group
            smats, rhs = [], []
            for d in range(2):
                for hh in range(2):
                    r = d * SSM_HEADS + 2 * m + hh
                    seg = seg_all[:, r * CHUNK:(r + 1) * CHUNK]
                    smats.append(jnp.where(causal[d], gmat[g] * jnp.exp2(seg), 0.0).astype(BF16))
                xd = xdt_pair[d][m]
                rhs += [jnp.where(half0, xd, 0.0), jnp.where(half0, 0.0, xd)]
            operands[m] = (jnp.concatenate(smats, axis=1), jnp.concatenate(rhs, axis=0).astype(BF16))

        def intra_matmul(m0):
            lhs = jnp.concatenate([operands[m0][0], operands[m0 + 1][0]], axis=1)
            zero = jnp.zeros_like(operands[m0][1])
            rhs = jnp.concatenate([jnp.concatenate([operands[m0][1], zero], axis=1),
                                   jnp.concatenate([zero, operands[m0 + 1][1]], axis=1)], axis=0)
            ydiag = _dot(lhs, rhs)
            for k in range(2):
                m = m0 + k
                g, mm = m // pairs_per_group, m % pairs_per_group
                ms = slice(m * LANES, (m + 1) * LANES)
                yacc_scr[pl.ds(r0, CHUNK), ms] = (ydiag[:, k * LANES:(k + 1) * LANES]
                                                  + yoff[g][:, mm * LANES:(mm + 1) * LANES] * jnp.exp2(arg_pair[0][m])
                                                  + dsk_ref[:, ms] * xs[:, ms])
                eb_scr[lc, :, ms] = jnp.exp2(arg_pair[1][m])

        assert npairs == 4
        intra_operands(0)
        intra_operands(1)
        intra_operands(2)
        intra_matmul(0)
        intra_operands(3)
        intra_matmul(2)
        return carry

    lax.fori_loop(0, nchunks, chunk_pass, 0, unroll=2)

    yt_scr[...] = jnp.zeros_like(yt_scr)
    order = lambda i: jnp.where(i < nctx, nctx - 1 - i, nchunks + nctx - 1 - i)

    def carry_pass(i, carry):
        chunk = order(jnp.minimum(i, nchunks - 1))
        lc = jnp.maximum(chunk - nctx, 0)
        r0 = pl.multiple_of(lc * CHUNK, CHUNK)
        hb = hb_scr[...]
        cb = cbuf_scr[lc]
        yoff = jnp.concatenate(
            [_dot(cb[:, g * D_STATE:(g + 1) * D_STATE], hb[:, g * group_cols:(g + 1) * group_cols].astype(BF16))
             for g in range(SSM_GROUPS)], axis=1)

        prev = order(jnp.maximum(i - 1, 0))
        pr0 = pl.multiple_of(jnp.maximum(prev - nctx, 0) * CHUNK, CHUNK)
        zf = z_ref[0, pl.ds(pl.multiple_of(prev * CHUNK, CHUNK), CHUNK), :].astype(F32)
        yt = yt_scr[...] * (zf * _sigmoid(zf))
        gw = SSM_WIDTH // SSM_GROUPS
        for g in range(SSM_GROUPS):
            yg = yt[:, g * gw:(g + 1) * gw]
            ms = jnp.mean(yg * yg, axis=-1, keepdims=True)
            y_ref[0, pl.ds(pr0, CHUNK), g * gw:(g + 1) * gw] = (yg * lax.rsqrt(ms + EPS)
                                                                * g_ref[:, g * gw:(g + 1) * gw]).astype(BF16)

        yt_scr[...] = yacc_scr[pl.ds(r0, CHUNK), :] + yoff * eb_scr[lc]
        hb_scr[...] = hb * decb_scr[chunk, 0:1, :] + stb_scr[chunk]
        return carry

    lax.fori_loop(0, nchunks + 1, carry_pass, 0, unroll=2)


def _ssd(xbc, dt, z, conv_w, conv_b, dtb16, alog16, dsk, g_ssm):
    b, tot = xbc.shape[0], xbc.shape[1]
    seq = tot - CTX
    nchunks = tot // CHUNK
    nctx = CTX // CHUNK
    nlat = nchunks - nctx
    const = lambda *shape: pl.BlockSpec(shape, lambda bb: (0,) * len(shape))
    whole = lambda rows, w: pl.BlockSpec((1, rows, w), lambda bb: (bb, 0, 0))
    jdx = np.arange(CHUNK)
    tri = jnp.asarray(np.stack([jdx[:, None] <= jdx[None, :], jdx[:, None] >= jdx[None, :]]), BF16)
    kdx = np.arange(CHUNK + 2 * HALO)
    shift = jnp.asarray(np.concatenate([kdx[None, :] == jdx[:, None] + HALO - 1,
                                        kdx[None, :] == jdx[:, None] + HALO + 1], axis=0), BF16)
    return pl.pallas_call(
        _ssd_kernel,
        grid=(b,),
        in_specs=[whole(tot, XBC_WIDTH), whole(tot, LANES), whole(tot, SSM_WIDTH),
                  const(3, XBC_WIDTH), const(1, XBC_WIDTH), const(2 * SSM_HEADS, CHUNK), const(2 * SSM_HEADS, CHUNK),
                  const(1, SSM_WIDTH), const(1, SSM_WIDTH), const(2, CHUNK, CHUNK),
                  const(2 * CHUNK, CHUNK + 2 * HALO),
                  const(LANES, 2 * SSM_HEADS * LANES), const(LANES, 2 * SSM_HEADS * LANES)],
        out_specs=whole(seq, SSM_WIDTH),
        out_shape=jax.ShapeDtypeStruct((b, seq, SSM_WIDTH), BF16),
        scratch_shapes=[pltpu.VMEM((D_STATE, SSM_WIDTH), F32),
                        pltpu.VMEM((D_STATE, SSM_WIDTH), F32),
                        pltpu.VMEM((seq, SSM_WIDTH), F32),
                        pltpu.VMEM((nchunks, D_STATE, SSM_WIDTH), F32),
                        pltpu.VMEM((nchunks, 8, SSM_WIDTH), F32),
                        pltpu.VMEM((nlat, CHUNK, SSM_WIDTH), F32),
                        pltpu.VMEM((nlat, CHUNK, SSM_GROUPS * D_STATE), BF16),
                        pltpu.VMEM((LANES, 2 * SSM_HEADS * LANES), BF16),
                        pltpu.VMEM((CHUNK, LANES), BF16),
                        pltpu.VMEM((CHUNK, SSM_WIDTH), F32)],
        compiler_params=_params("arbitrary"),
        name="ssd",
    )(xbc, dt, z, conv_w, conv_b, dtb16, alog16, dsk, g_ssm, tri, shift, *_ssd_selectors())


def _mixffn_kernel(am_ref, ap_ref, an_ref, ym_ref, yp_ref, yn_ref, xm_ref, xp_ref, xn_ref,
                   wo_ref, wup_ref, cw_ref, cb_ref, wd_ref, mod_ref, g_ref, o_ref):
    i = pl.program_id(1)
    tm = am_ref.shape[1]
    nrows = tm + 2 * HALO
    ext = lambda p, m, n: jnp.concatenate([p[0], m[0], n[0]], axis=0)
    a_ext = ext(ap_ref, am_ref, an_ref)
    y_ext = ext(yp_ref, ym_ref, yn_ref)
    x_ext = ext(xp_ref, xm_ref, xn_ref)
    mod = mod_ref[0]
    lo = jnp.where(i > 0, 0, HALO)
    hi = jnp.where(i < pl.num_programs(1) - 1, nrows, nrows - HALO)
    half = nrows // 2
    halves = [slice(k * half, (k + 1) * half) for k in range(2)]
    proj = [_dot(a_ext[rs], wo_ref[0:ATTN_WIDTH, :]) + _dot(y_ext[rs], wo_ref[ATTN_WIDTH:D_MODEL, :])
            for rs in halves]
    xparts, lparts = [], []
    for k, rs in enumerate(halves):
        xk = x_ext[rs] + mod[:, 2 * D_MODEL:3 * D_MODEL] * proj[k]
        ms = jnp.mean(xk * xk, axis=-1, keepdims=True)
        hn = xk * lax.rsqrt(ms + EPS) * g_ref[...]
        h2 = hn * (1.0 + mod[:, 4 * D_MODEL:5 * D_MODEL]) + mod[:, 3 * D_MODEL:4 * D_MODEL]
        row = lax.broadcasted_iota(jnp.int32, (half, 1), 0) + k * half
        lparts.append(jnp.where(row >= lo, jnp.where(row < hi, h2, 0.0), 0.0).astype(BF16))
        xparts.append(xk)
    xnew = jnp.concatenate(xparts, axis=0)
    lhs = jnp.concatenate(lparts, axis=0)

    chunks = [(c0, min(c0 + TF_FFN, D_FF)) for c0 in range(0, D_FF, TF_FFN)]
    ups, parts = {}, []

    def up_proj(k):
        c0, c1 = chunks[k]
        ups[k] = (_dot(lhs, wup_ref[:, c0:c1]), _dot(lhs, wup_ref[:, D_FF + c0:D_FF + c1]))

    def conv(uu, c0, c1):
        up = pltpu.roll(uu, 1, 0)[HALO:HALO + tm]
        dn = pltpu.roll(uu, nrows - 1, 0)[HALO:HALO + tm]
        return (cw_ref[0:1, c0:c1] * up + cw_ref[1:2, c0:c1] * uu[HALO:HALO + tm]
                + cw_ref[2:3, c0:c1] * dn + cb_ref[:, c0:c1])

    def gate_down(k):
        c0, c1 = chunks[k]
        ua = conv(ups[k][0], c0, c1)
        ug = conv(ups[k][1], D_FF + c0, D_FF + c1)
        act = (ua * (ug * _sigmoid(ug))).astype(BF16)
        parts.append(_dot(act, wd_ref[c0:c1, :]))

    up_proj(0)
    for k in range(1, len(chunks)):
        up_proj(k)
        gate_down(k - 1)
    gate_down(len(chunks) - 1)
    acc = parts[0]
    for part in parts[1:]:
        acc = acc + part
    o_ref[0] = xnew[HALO:HALO + tm] + mod[:, 5 * D_MODEL:6 * D_MODEL] * acc


def _mixffn(attn, y, x, w_out16, w_up16, conv_w, conv_b, w_down16, mod3, g_ffn):
    b, seq = x.shape[0], x.shape[1]
    ni = seq // TM_FFN
    hpt = TM_FFN // HALO
    nhalo = seq // HALO
    main = lambda w: pl.BlockSpec((1, TM_FFN, w), lambda bb, i: (bb, i, 0))
    prev = lambda w: pl.BlockSpec((1, HALO, w), lambda bb, i: (bb, jnp.maximum(i * hpt - 1, 0), 0))
    nxt = lambda w: pl.BlockSpec((1, HALO, w), lambda bb, i: (bb, jnp.minimum(i * hpt + hpt, nhalo - 1), 0))
    resident = lambda *shape: pl.BlockSpec(shape, lambda bb, i: (0,) * len(shape), pipeline_mode=pl.Buffered(1))
    trio = lambda w: [main(w), prev(w), nxt(w)]
    return pl.pallas_call(
        _mixffn_kernel,
        grid=(b, ni),
        in_specs=trio(ATTN_WIDTH) + trio(SSM_WIDTH) + trio(D_MODEL) + [
            resident(D_MODEL, D_MODEL), resident(D_MODEL, 2 * D_FF), resident(3, 2 * D_FF), resident(1, 2 * D_FF),
            resident(D_FF, D_MODEL),
            pl.BlockSpec((1, 1, 6 * D_MODEL), lambda bb, i: (bb, 0, 0)),
            resident(1, D_MODEL)],
        out_specs=main(D_MODEL),
        out_shape=jax.ShapeDtypeStruct((b, seq, D_MODEL), F32),
        compiler_params=_params("arbitrary", "arbitrary"),
        name="mixffn",
    )(attn, attn, attn, y, y, y, x, x, x, w_out16, w_up16, conv_w, conv_b.reshape(1, -1), w_down16, mod3,
      g_ffn.reshape(1, -1))


def _layer(x, c, ctx, c_ctx, w_mod, b_mod, g_mix, w_in, g_q, g_k, sink, ssm_conv_w, ssm_conv_b,
           a_log, dt_bias, d_skip, g_ssm, w_out, g_ffn, w_up, ffn_conv_w, ffn_conv_b, w_down):
    mod3 = _modulation(c, c_ctx, w_mod, b_mod)

    q, kk, vv, z, xbc, dt = _inproj(x, ctx, mod3, g_mix, w_in, g_q, g_k)
    attn = _attention(q, kk, vv, sink)

    head_rows = lambda t: jnp.broadcast_to(t.reshape(2 * SSM_HEADS, 1), (2 * SSM_HEADS, CHUNK))
    y = _ssd(xbc, dt, z, ssm_conv_w, ssm_conv_b.reshape(1, -1), head_rows(dt_bias), head_rows(a_log),
             jnp.repeat(d_skip, SSM_HEAD_DIM).reshape(1, -1), g_ssm.reshape(1, -1))

    return _mixffn(attn, y, x, w_out.astype(BF16), w_up.astype(BF16), ffn_conv_w, ffn_conv_b, w_down.astype(BF16),
                   mod3, g_ffn)


def kernel(x, c, ctx, c_ctx, w_mod, b_mod, g_mix, w_in, g_q, g_k, sink, ssm_conv_w, ssm_conv_b,
           a_log, dt_bias, d_skip, g_ssm, w_out, g_ffn, w_up, ffn_conv_w, ffn_conv_b, w_down):
    assert w_mod.shape[0] == 1, "single-layer block"
    first = lambda t: t.reshape(t.shape[1:])
    return _layer(x, c, ctx, c_ctx, first(w_mod), first(b_mod), first(g_mix), first(w_in), first(g_q),
                  first(g_k), first(sink), first(ssm_conv_w), first(ssm_conv_b), first(a_log),
                  first(dt_bias), first(d_skip), first(g_ssm), first(w_out), first(g_ffn), first(w_up),
                  first(ffn_conv_w), first(ffn_conv_b), first(w_down))
```

```python
import jax
import jax.numpy as jnp
import numpy as np
from jax import lax
from jax.experimental import pallas as pl
from jax.experimental.pallas import tpu as pltpu

F32 = jnp.float32
BF16 = jnp.bfloat16

D_MODEL = 1024
CTX = 256
GRID_W = 64
HEAD_DIM = 64
ATTN_HEADS = 8
KV_HEADS = 2
ATTN_WIDTH = ATTN_HEADS * HEAD_DIM
KV_WIDTH = KV_HEADS * HEAD_DIM
WINDOW = 128
BLOCK = 128
ROPE_BASE = 10000.0
SSM_WIDTH = D_MODEL - ATTN_WIDTH
SSM_HEAD_DIM = 64
SSM_HEADS = SSM_WIDTH // SSM_HEAD_DIM
SSM_GROUPS = 2
D_STATE = 128
XBC_WIDTH = SSM_WIDTH + 2 * SSM_GROUPS * D_STATE
CHUNK = 128
D_FF = 2816
IN_COLS = ATTN_WIDTH + 2 * KV_WIDTH + SSM_WIDTH + XBC_WIDTH + 2 * SSM_HEADS
EPS = 1e-6
NEG_INF = -1e30
LOG2E = 1.4426950408889634

LANES = 128
BF16_SUBLANES = 16
IN_COLS_PAD = 19 * LANES
COL_Q, COL_K, COL_V = 0, ATTN_WIDTH, ATTN_WIDTH + KV_WIDTH
COL_Z = ATTN_WIDTH + 2 * KV_WIDTH
COL_XBC = COL_Z + SSM_WIDTH
COL_DT = COL_XBC + XBC_WIDTH
VMEM_LIMIT = 56 * 1024 * 1024

TM_IN = 3 * CTX
TM_FFN = 512
TF_FFN = 1024
HALO = BF16_SUBLANES


def _params(*sem):
    return pltpu.CompilerParams(dimension_semantics=sem, vmem_limit_bytes=VMEM_LIMIT)


def _sigmoid(v):
    return 1.0 / (1.0 + jnp.exp2(v * (-LOG2E)))


def _softplus(v):
    return jnp.maximum(v, 0.0) + jnp.log1p(jnp.exp(-jnp.abs(v)))


def _dot(a, b):
    return jnp.dot(a, b, preferred_element_type=F32)


def _dot_nt(a, b):
    return lax.dot_general(a, b, (((1,), (1,)), ((), ())), preferred_element_type=F32)


def _mod_kernel(c_ref, cctx_ref, w_ref, b_ref, o_ref):
    rows = o_ref.shape[0]
    b = c_ref.shape[0]
    cv = jnp.concatenate([c_ref[...], cctx_ref[...], jnp.zeros((rows - b - 1, D_MODEL), F32)], axis=0)
    s = (cv * _sigmoid(cv)).astype(BF16)
    res = _dot(s, w_ref[...].astype(BF16)) + b_ref[...]
    for r in range(b + 1):
        o_ref[r] = res[r:r + 1, :]
    o_ref[b + 1:rows] = jnp.zeros((rows - b - 1, 1, res.shape[1]), F32)


def _modulation(c, c_ctx, w_mod, b_mod):
    b = c.shape[0]
    rows = 8 * ((b + 1 + 7) // 8)
    tn = 1024
    return pl.pallas_call(
        _mod_kernel,
        grid=(6 * D_MODEL // tn,),
        in_specs=[pl.BlockSpec((b, D_MODEL), lambda j: (0, 0)),
                  pl.BlockSpec((1, D_MODEL), lambda j: (0, 0)),
                  pl.BlockSpec((D_MODEL, tn), lambda j: (0, j)),
                  pl.BlockSpec((1, tn), lambda j: (0, j))],
        out_specs=pl.BlockSpec((rows, 1, tn), lambda j: (0, 0, j)),
        out_shape=jax.ShapeDtypeStruct((rows, 1, 6 * D_MODEL), F32),
        compiler_params=_params("arbitrary"),
        name="mod",
    )(c, c_ctx.reshape(1, -1), w_mod, b_mod.reshape(1, -1))


def _inproj_kernel(xa_ref, xb_ref, xc_ref, ctx_ref, modb_ref, modc_ref, gmix_ref, w_ref, gq_ref, gk_ref,
                   cos_ref, sina_ref, sinb_ref, bd_ref,
                   q_ref, k_ref, v_ref, z_ref, xbc_ref, dt_ref, w16_scr):
    first = pl.program_id(1) == 0

    @pl.when(first & (pl.program_id(0) == 0))
    def _():
        w16_scr[:, 0:IN_COLS] = w_ref[...].astype(BF16)
        w16_scr[:, IN_COLS:IN_COLS_PAD] = jnp.zeros((D_MODEL, IN_COLS_PAD - IN_COLS), BF16)

    def normed(xin, mod):
        ms = jnp.mean(xin * xin, axis=-1, keepdims=True)
        hn = xin * lax.rsqrt(ms + EPS) * gmix_ref[...]
        return (hn * (1.0 + mod[:, D_MODEL:2 * D_MODEL]) + mod[:, 0:D_MODEL]).astype(BF16)

    modb = modb_ref[0]
    bd = bd_ref[...]

    def project(h, rs, emit_between):
        cos, sina, sinb = cos_ref[rs, :], sina_ref[rs, :], sinb_ref[rs, :]

        def norm_rope(pf, gain):
            ss = pf * pf
            hi = ss.astype(BF16)
            lo = (ss - hi.astype(F32)).astype(BF16)
            sums = _dot(jnp.concatenate([hi, lo], axis=1), bd)
            y = pf * lax.rsqrt(sums * (1.0 / HEAD_DIM) + EPS) * gain
            return y * cos + pltpu.roll(y, LANES - 16, 1) * sina + pltpu.roll(y, 16, 1) * sinb

        def q_chunks(lo, hi):
            for cidx in range(lo, hi):
                cs = slice(cidx * LANES, (cidx + 1) * LANES)
                q_ref[0, rs, cs] = norm_rope(qf[:, cs], gq_ref[...]).astype(BF16)

        qf = _dot(h, w16_scr[:, COL_Q:COL_Q + ATTN_WIDTH])
        xf = _dot(h, w16_scr[:, COL_XBC:COL_XBC + XBC_WIDTH])
        between = emit_between()
        q_chunks(0, 2)
        kvf = _dot(h, w16_scr[:, COL_K:COL_K + 2 * KV_WIDTH])
        q_chunks(2, ATTN_WIDTH // LANES)
        zf = _dot(h, w16_scr[:, COL_Z:COL_Z + SSM_WIDTH])
        dtf = _dot(h, w16_scr[:, COL_DT:COL_DT + LANES])
        kf = norm_rope(kvf[:, 0:KV_WIDTH], gk_ref[...])
        k_ref[0, rs, 0:LANES] = kf.astype(BF16)
        k_ref[0, rs, LANES:2 * LANES] = pltpu.roll(kf, HEAD_DIM, 1).astype(BF16)
        vf = kvf[:, KV_WIDTH:2 * KV_WIDTH]
        v_ref[0, rs, 0:LANES] = vf.astype(BF16)
        v_ref[0, rs, LANES:2 * LANES] = pltpu.roll(vf, HEAD_DIM, 1).astype(BF16)
        z_ref[0, rs, :] = zf.astype(BF16)
        xbc_ref[0, rs, :] = xf.astype(BF16)
        dt_ref[0, rs, :] = dtf
        return between

    h_head = jnp.concatenate([normed(jnp.where(first, ctx_ref[0], xa_ref[0]), jnp.where(first, modc_ref[0], modb)),
                              normed(xb_ref[0], modb)], axis=0)
    h_tail = project(h_head, slice(0, 2 * CTX), lambda: normed(xc_ref[0], modb))
    project(h_tail, slice(2 * CTX, 3 * CTX), lambda: None)


def _rope_tables(seq):
    rows = seq // GRID_W
    pos_r = np.repeat(np.arange(rows), GRID_W).astype(np.float32)
    pos_c = np.tile(np.arange(GRID_W), rows).astype(np.float32)
    quarter = HEAD_DIM // 4
    freqs = (ROPE_BASE ** (-np.arange(quarter, dtype=np.float32) / quarter)).astype(np.float32)
    ang_r = pos_r[:, None] * freqs[None, :]
    ang_c = pos_c[:, None] * freqs[None, :]
    cr, sr, cc, sc = np.cos(ang_r), np.sin(ang_r), np.cos(ang_c), np.sin(ang_c)
    zero = np.zeros_like(sr)
    cos = np.concatenate([cr, cr, cc, cc], axis=-1)
    sina = np.concatenate([-sr, zero, -sc, zero], axis=-1)
    sinb = np.concatenate([zero, sr, zero, sc], axis=-1)
    reps = LANES // HEAD_DIM
    pad = lambda t, v: np.concatenate([np.full((CTX, LANES), v, np.float32), np.tile(t, (1, reps))], axis=0)
    return tuple(jnp.asarray(pad(t, v), F32) for t, v in ((cos, 1.0), (sina, 0.0), (sinb, 0.0)))


def _inproj(x, ctx, mod3, g_mix, w_in, g_q, g_k):
    b, seq = x.shape[0], x.shape[1]
    tot = CTX + seq
    cos, sina, sinb = _rope_tables(seq)
    reps = LANES // HEAD_DIM
    gq = (jnp.tile(g_q, reps) * (HEAD_DIM ** -0.5 * LOG2E)).reshape(1, LANES)
    gk = jnp.tile(g_k, reps).reshape(1, LANES)
    lane = np.arange(LANES)
    bd = lane[:, None] // HEAD_DIM == lane[None, :] // HEAD_DIM
    bd = jnp.asarray(np.concatenate([bd, bd], axis=0), BF16)
    nsteps = tot // TM_IN
    sub = TM_IN // CTX
    const = lambda *shape: pl.BlockSpec(shape, lambda bb, s: (0,) * len(shape))
    tab = pl.BlockSpec((TM_IN, LANES), lambda bb, s: (s, 0))
    out = lambda w: pl.BlockSpec((1, TM_IN, w), lambda bb, s: (bb, s, 0))
    shp = lambda w, dt: jax.ShapeDtypeStruct((b, tot, w), dt)
    xsub = lambda j: pl.BlockSpec((1, CTX, D_MODEL), lambda bb, s: (bb, jnp.maximum(sub * s + j - 1, 0), 0))
    return pl.pallas_call(
        _inproj_kernel,
        grid=(b, nsteps),
        in_specs=[xsub(0), xsub(1), xsub(2),
                  pl.BlockSpec((1, CTX, D_MODEL), lambda bb, s: (bb, 0, 0)),
                  pl.BlockSpec((1, 1, 6 * D_MODEL), lambda bb, s: (bb, 0, 0)),
                  pl.BlockSpec((1, 1, 6 * D_MODEL), lambda bb, s: (b, 0, 0)),
                  const(1, D_MODEL),
                  pl.BlockSpec((D_MODEL, IN_COLS), lambda bb, s: (0, 0), pipeline_mode=pl.Buffered(1)),
                  const(1, LANES), const(1, LANES),
                  tab, tab, tab, const(2 * LANES, LANES)],
        out_specs=[out(ATTN_WIDTH), out(2 * LANES), out(2 * LANES), out(SSM_WIDTH), out(XBC_WIDTH), out(LANES)],
        out_shape=[shp(ATTN_WIDTH, BF16), shp(2 * LANES, BF16), shp(2 * LANES, BF16),
                   shp(SSM_WIDTH, BF16), shp(XBC_WIDTH, BF16), shp(LANES, F32)],
        scratch_shapes=[pltpu.VMEM((D_MODEL, IN_COLS_PAD), BF16)],
        compiler_params=_params("arbitrary", "arbitrary"),
        name="inproj",
    )(x, x, x, ctx, mod3, mod3, g_mix.reshape(1, -1), w_in, gq, gk, cos, sina, sinb, bd)


def _place_kv_head(t, kh, par):
    src = 0 if par == kh else 1
    lane = lax.broadcasted_iota(jnp.int32, (t.shape[0], LANES), 1)
    half = (lane >= HEAD_DIM) if par else (lane < HEAD_DIM)
    tile = t[:, src * LANES:(src + 1) * LANES]
    return jnp.where(half, tile, jnp.zeros_like(tile))


def _attn_kernel(sink_ref, q_ref, k_ref, v_ref, o_ref, kx_scr, vx_scr):
    seq = o_ref.shape[1]
    nblk = seq // BLOCK
    qi = lax.broadcasted_iota(jnp.int32, (BLOCK, BLOCK), 0)
    kj = lax.broadcasted_iota(jnp.int32, (BLOCK, BLOCK), 1)
    zero = jnp.zeros((BLOCK, BLOCK), F32)
    tri_prev = jnp.where(kj >= qi, zero, NEG_INF)
    tri_next = jnp.where(kj <= qi, zero, NEG_INF)
    combos = [(kh, par) for kh in range(KV_HEADS) for par in range(2)]
    for idx, (kh, par) in enumerate(combos):
        kx_scr[idx] = _place_kv_head(k_ref[0, 0:CTX, :], kh, par)
        vx_scr[idx] = _place_kv_head(v_ref[0, 0:CTX, :], kh, par)

    def block(nb):
        r0 = pl.multiple_of(CTX + nb * BLOCK, BLOCK)
        rn = pl.multiple_of(CTX + jnp.minimum(nb + 1, nblk - 1) * BLOCK, BLOCK)
        rp = pl.multiple_of(CTX - BLOCK + nb * BLOCK, BLOCK)
        kloc = jnp.concatenate([k_ref[0, pl.ds(rp, 2 * BLOCK), :], k_ref[0, pl.ds(rn, BLOCK), :]], axis=0)
        vloc = jnp.concatenate([v_ref[0, pl.ds(rp, 2 * BLOCK), :], v_ref[0, pl.ds(rn, BLOCK), :]], axis=0)
        bias_p = jnp.where(nb > 0, tri_prev, NEG_INF)
        bias_n = jnp.where(nb < nblk - 1, tri_next, NEG_INF)
        q = q_ref[0, pl.ds(r0, BLOCK), :]
        q2s = [jnp.concatenate([q[:, (2 * kh) * LANES:(2 * kh + 1) * LANES],
                                q[:, (2 * kh + 1) * LANES:(2 * kh + 2) * LANES]], axis=0)
               for kh in range(KV_HEADS)]
        sc, ee, oo = {}, {}, {}

        def scores(idx):
            kh, par = combos[idx]
            sc[idx] = (_dot_nt(q2s[kh], _place_kv_head(kloc, kh, par)),
                       _dot_nt(q2s[kh], kx_scr[idx]))

        def softmax(idx):
            kh, par = combos[idx]
            s_loc, s_ctx = sc[idx]
            es, inv = [], []
            for r in range(2):
                head = 4 * kh + par + 2 * r
                rs = slice(r * BLOCK, (r + 1) * BLOCK)
                sv = jnp.concatenate([s_loc[rs, 0:BLOCK] + bias_p, s_loc[rs, BLOCK:2 * BLOCK],
                                      s_loc[rs, 2 * BLOCK:3 * BLOCK] + bias_n, s_ctx[rs]], axis=1)
                sk = sink_ref[head] * LOG2E
                m = jnp.maximum(jnp.max(sv, axis=-1, keepdims=True), sk)
                e = jnp.exp2(sv - m)
                inv.append(1.0 / (jnp.sum(e, axis=-1, keepdims=True) + jnp.exp2(sk - m)))
                es.append(e.astype(BF16))
            ee[idx] = (jnp.concatenate(es, axis=0), inv)

        def pv(idx):
            kh, par = combos[idx]
            e16, inv = ee[idx]
            o2 = _dot(e16[:, 0:3 * BLOCK], _place_kv_head(vloc, kh, par)) + _dot(e16[:, 3 * BLOCK:], vx_scr[idx])
            oo[idx] = [o2[r * BLOCK:(r + 1) * BLOCK] * inv[r] for r in range(2)]

        def finish():
            for kh in range(KV_HEADS):
                for r in range(2):
                    c = 2 * kh + r
                    val = oo[combos.index((kh, 0))][r] + oo[combos.index((kh, 1))][r]
                    o_ref[0, pl.ds(pl.multiple_of(nb * BLOCK, BLOCK), BLOCK), c * LANES:(c + 1) * LANES] = val.astype(BF16)

        stage = lambda f, idx: (lambda: f(idx))
        return [stage(scores, 0), stage(scores, 1), stage(softmax, 0), stage(scores, 2), stage(softmax, 1),
                stage(pv, 0), stage(scores, 3), stage(softmax, 2), stage(pv, 1), stage(softmax, 3), stage(pv, 2),
                stage(pv, 3), finish]

    def two_blocks(i, carry):
        first, second = block(2 * i), block(2 * i + 1)
        lag = 3 * len(first) // 4
        for k in range(len(first) + lag):
            if k < len(first):
                first[k]()
            if 0 <= k - lag < len(second):
                second[k - lag]()
        return carry

    assert nblk % 2 == 0
    lax.fori_loop(0, nblk // 2, two_blocks, 0, unroll=2)


def _attention(q, kk, vv, sink):
    b, tot = q.shape[0], q.shape[1]
    seq = tot - CTX
    whole = lambda w: pl.BlockSpec((1, tot, w), lambda bb: (bb, 0, 0))
    return pl.pallas_call(
        _attn_kernel,
        grid=(b,),
        in_specs=[pl.BlockSpec(memory_space=pltpu.SMEM), whole(ATTN_WIDTH), whole(2 * LANES), whole(2 * LANES)],
        out_specs=pl.BlockSpec((1, seq, ATTN_WIDTH), lambda bb: (bb, 0, 0)),
        out_shape=jax.ShapeDtypeStruct((b, seq, ATTN_WIDTH), BF16),
        scratch_shapes=[pltpu.VMEM((2 * KV_HEADS, CTX, LANES), BF16), pltpu.VMEM((2 * KV_HEADS, CTX, LANES), BF16)],
        compiler_params=_params("arbitrary"),
        name="attn",
    )(sink, q, kk, vv)


def _ssd_selectors():
    nh2 = 2 * SSM_HEADS
    npairs = SSM_HEADS // 2
    k = np.arange(LANES)
    part, q = k // (2 * nh2), k % (2 * nh2)
    live = part < 3
    col = np.arange(nh2 * LANES)
    tile, lane = col // LANES, col % LANES
    kind, blk = tile // (nh2 // 2), tile % (nh2 // 2)
    src = kind * nh2 + (blk // npairs) * SSM_HEADS + 2 * (blk % npairs) + (lane >= SSM_HEAD_DIM)
    pairsel = live[:, None] & (q[:, None] == src[None, :])
    segind = live[:, None] & (q[:, None] < nh2) & (q[:, None] == tile[None, :])
    return jnp.asarray(pairsel, BF16), jnp.asarray(segind, BF16)


def _ssd_kernel(x_ref, dt_ref, z_ref, cw_ref, cb_ref, dtb_ref, alog_ref, dsk_ref, g_ref, tri_ref, shift_ref,
                pairsel_ref, segind_ref,
                y_ref, hf_scr, hb_scr, yacc_scr, stb_scr, decb_scr, eb_scr, cbuf_scr, segrhs_scr, cols3_scr, yt_scr):
    tot = x_ref.shape[1]
    nchunks = tot // CHUNK
    nctx = CTX // CHUNK
    npairs = SSM_HEADS // 2
    pairs_per_group = npairs // SSM_GROUPS
    group_cols = pairs_per_group * LANES

    hf_scr[...] = jnp.zeros_like(hf_scr)
    hb_scr[...] = jnp.zeros_like(hb_scr)
    segrhs_scr[...] = segind_ref[...]

    nh2 = 2 * SSM_HEADS

    def prep(chunk):
        c0 = pl.multiple_of(chunk * CHUNK, CHUNK)
        dtv = _softplus(dt_ref[0, pl.ds(c0, CHUNK), :].T[0:nh2, :] + dtb_ref[...])
        a = dtv * (-jnp.exp(alog_ref[...]) * LOG2E)

        def split3(v):
            hi = v.astype(BF16)
            r1 = v - hi.astype(F32)
            mid = r1.astype(BF16)
            return hi, mid, (r1 - mid.astype(F32)).astype(BF16)

        parts = jnp.concatenate(split3(a), axis=0)
        fold = lambda p: p[0:nh2] + p[nh2:2 * nh2] + p[2 * nh2:3 * nh2]
        prefix = fold(_dot(parts, tri_ref[0]))
        suffix = fold(_dot(parts, tri_ref[1]))
        head_row = lax.broadcasted_iota(jnp.int32, (nh2, CHUNK), 0)
        ac_t = jnp.where(head_row < SSM_HEADS, prefix, suffix)

        v32 = [p.astype(F32) for p in split3(jnp.concatenate([ac_t, dtv], axis=0))]
        ones3 = (lax.broadcasted_iota(jnp.int32, (CHUNK - 6 * nh2, CHUNK), 0) < 3).astype(F32)
        cols3_scr[...] = jnp.concatenate(v32 + [ones3], axis=0).T.astype(BF16)
        for q in range(nh2):
            neg = [-p[q:q + 1, :] for p in v32]
            tile = jnp.concatenate(neg + [jnp.zeros((BF16_SUBLANES - 3, CHUNK), F32)], axis=0)
            segrhs_scr[6 * nh2:6 * nh2 + BF16_SUBLANES, q * CHUNK:(q + 1) * CHUNK] = tile.astype(BF16)

    prep(0)

    def chunk_pass(chunk, carry):
        lc = jnp.maximum(chunk - nctx, 0)
        r0 = pl.multiple_of(lc * CHUNK, CHUNK)
        c0 = pl.multiple_of(chunk * CHUNK, CHUNK)
        p0 = pl.multiple_of(jnp.maximum(chunk * CHUNK - HALO, 0), HALO)
        n0 = pl.multiple_of(jnp.minimum(chunk * CHUNK + CHUNK, tot - HALO), HALO)

        cols3 = cols3_scr[...]
        x16 = x_ref[0, pl.ds(c0, CHUNK), :]
        has_prev = (chunk != 0) & (chunk != nctx)
        has_next = (chunk != nctx - 1) & (chunk != nchunks - 1)
        halo_p = x_ref[0, pl.ds(p0, HALO), :]
        halo_n = x_ref[0, pl.ds(n0, HALO), :]
        xext = jnp.concatenate([jnp.where(has_prev, halo_p, jnp.zeros_like(halo_p)), x16,
                                jnp.where(has_next, halo_n, jnp.zeros_like(halo_n))], axis=0)
        shifted = _dot(shift_ref[...], xext)
        pairs = _dot(cols3, pairsel_ref[...])
        seg_all = _dot(cols3, segrhs_scr[...])

        u = (cw_ref[0:1, :] * shifted[0:CHUNK] + cw_ref[1:2, :] * x16.astype(F32)
             + cw_ref[2:3, :] * shifted[CHUNK:2 * CHUNK] + cb_ref[...])
        u = u * _sigmoid(u)
        xs = u[:, 0:SSM_WIDTH]
        bmat = u[:, SSM_WIDTH:SSM_WIDTH + SSM_GROUPS * D_STATE]
        cmat = u[:, SSM_WIDTH + SSM_GROUPS * D_STATE:XBC_WIDTH]
        cb16 = [cmat[:, g * D_STATE:(g + 1) * D_STATE].astype(BF16) for g in range(SSM_GROUPS)]
        bb16 = [bmat[:, g * D_STATE:(g + 1) * D_STATE].astype(BF16) for g in range(SSM_GROUPS)]
        bt16 = [bmat[:, g * D_STATE:(g + 1) * D_STATE].T.astype(BF16) for g in range(SSM_GROUPS)]
        cbuf_scr[lc] = jnp.concatenate(cb16, axis=1)

        hf_start = [hf_scr[:, g * group_cols:(g + 1) * group_cols] for g in range(SSM_GROUPS)]
        gm2 = _dot_nt(jnp.concatenate(cb16, axis=0), jnp.concatenate(bb16, axis=0))
        gmat = [gm2[g * CHUNK:(g + 1) * CHUNK, g * CHUNK:(g + 1) * CHUNK] for g in range(SSM_GROUPS)]
        yoff = [_dot(cb16[g], hf_start[g].astype(BF16)) for g in range(SSM_GROUPS)]

        half0 = lax.broadcasted_iota(jnp.int32, (CHUNK, LANES), 1) < SSM_HEAD_DIM
        ii = lax.broadcasted_iota(jnp.int32, (CHUNK, CHUNK), 0)
        jj = lax.broadcasted_iota(jnp.int32, (CHUNK, CHUNK), 1)
        causal = [jj <= ii, jj >= ii]
        last = [CHUNK - 1, 0]

        arg_pair, tot_pair, xdt_pair, xdtd_pair = [], [], [], []
        for d in range(2):
            args, tots, xdts, xdtds = [], [], [], []
            for m in range(npairs):
                blk = d * npairs + m
                arg = pairs[:, blk * LANES:(blk + 1) * LANES]
                dtp = pairs[:, (nh2 // 2 + blk) * LANES:(nh2 // 2 + blk + 1) * LANES]
                total = arg[last[d]:last[d] + 1, :]
                xdt = xs[:, m * LANES:(m + 1) * LANES] * dtp
                args.append(arg)
                tots.append(total)
                xdts.append(xdt)
                xdtds.append(xdt * jnp.exp2(total - arg))
            arg_pair.append(args)
            tot_pair.append(tots)
            xdt_pair.append(xdts)
            xdtd_pair.append(xdtds)

        for g in range(SSM_GROUPS):
            pr = range(g * pairs_per_group, (g + 1) * pairs_per_group)
            gs = slice(g * group_cols, (g + 1) * group_cols)
            st_f = _dot(bt16[g], jnp.concatenate([xdtd_pair[0][m] for m in pr], axis=1).astype(BF16))
            st_b = _dot(bt16[g], jnp.concatenate([xdtd_pair[1][m] for m in pr], axis=1).astype(BF16))
            dec_f = jnp.concatenate([jnp.exp2(tot_pair[0][m]) for m in pr], axis=1)
            dec_b = jnp.concatenate([jnp.exp2(tot_pair[1][m]) for m in pr], axis=1)
            hf_scr[:, gs] = hf_start[g] * dec_f + st_f
            stb_scr[chunk, :, gs] = st_b
            decb_scr[chunk, :, gs] = jnp.broadcast_to(dec_b, (8, group_cols))

        prep(jnp.minimum(chunk + 1, nchunks - 1))
        operands = {}

        def intra_operands(m):
            g = m // pairs_per_group
            smats, rhs = [], []
            for d in range(2):
                for hh in range(2):
                    r = d * SSM_HEADS + 2 * m + hh
                    seg = seg_all[:, r * CHUNK:(r + 1) * CHUNK]
                    smats.append(jnp.where(causal[d], gmat[g] * jnp.exp2(seg), 0.0).astype(BF16))
                xd = xdt_pair[d][m]
                rhs += [jnp.where(half0, xd, 0.0), jnp.where(half0, 0.0, xd)]
            operands[m] = (jnp.concatenate(smats, axis=1), jnp.concatenate(rhs, axis=0).astype(BF16))

        def intra_matmul(m0):
            lhs = jnp.concatenate([operands[m0][0], operands[m0 + 1][0]], axis=1)
            zero = jnp.zeros_like(operands[m0][1])
            rhs = jnp.concatenate([jnp.concatenate([operands[m0][1], zero], axis=1),
                                   jnp.concatenate([zero, operands[m0 + 1][1]], axis=1)], axis=0)
            ydiag = _dot(lhs, rhs)
            for k in range(2):
                m = m0 + k
                g, mm = m // pairs_per_group, m % pairs_per_group
                ms = slice(m * LANES, (m + 1) * LANES)
                yacc_scr[pl.ds(r0, CHUNK), ms] = (ydiag[:, k * LANES:(k + 1) * LANES]
                                                  + yoff[g][:, mm * LANES:(mm + 1) * LANES] * jnp.exp2(arg_pair[0][m])
                                                  + dsk_ref[:, ms] * xs[:, ms])
                eb_scr[lc, :, ms] = jnp.exp2(arg_pair[1][m])

        assert npairs == 4
        intra_operands(0)
        intra_operands(1)
        intra_operands(2)
        intra_matmul(0)
        intra_operands(3)
        intra_matmul(2)
        return carry

    lax.fori_loop(0, nchunks, chunk_pass, 0, unroll=2)

    yt_scr[...] = jnp.zeros_like(yt_scr)
    order = lambda i: jnp.where(i < nctx, nctx - 1 - i, nchunks + nctx - 1 - i)

    def carry_pass(i, carry):
        chunk = order(jnp.minimum(i, nchunks - 1))
        lc = jnp.maximum(chunk - nctx, 0)
        r0 = pl.multiple_of(lc * CHUNK, CHUNK)
        hb = hb_scr[...]
        cb = cbuf_scr[lc]
        yoff = jnp.concatenate(
            [_dot(cb[:, g * D_STATE:(g + 1) * D_STATE], hb[:, g * group_cols:(g + 1) * group_cols].astype(BF16))
             for g in range(SSM_GROUPS)], axis=1)

        prev = order(jnp.maximum(i - 1, 0))
        pr0 = pl.multiple_of(jnp.maximum(prev - nctx, 0) * CHUNK, CHUNK)
        zf = z_ref[0, pl.ds(pl.multiple_of(prev * CHUNK, CHUNK), CHUNK), :].astype(F32)
        yt = yt_scr[...] * (zf * _sigmoid(zf))
        gw = SSM_WIDTH // SSM_GROUPS
        for g in range(SSM_GROUPS):
            yg = yt[:, g * gw:(g + 1) * gw]
            ms = jnp.mean(yg * yg, axis=-1, keepdims=True)
            y_ref[0, pl.ds(pr0, CHUNK), g * gw:(g + 1) * gw] = (yg * lax.rsqrt(ms + EPS)
                                                                * g_ref[:, g * gw:(g + 1) * gw]).astype(BF16)

        yt_scr[...] = yacc_scr[pl.ds(r0, CHUNK), :] + yoff * eb_scr[lc]
        hb_scr[...] = hb * decb_scr[chunk, 0:1, :] + stb_scr[chunk]
        return carry

    lax.fori_loop(0, nchunks + 1, carry_pass, 0, unroll=2)


def _ssd(xbc, dt, z, conv_w, conv_b, dtb16, alog16, dsk, g_ssm):
    b, tot = xbc.shape[0], xbc.shape[1]
    seq = tot - CTX
    nchunks = tot // CHUNK
    nctx = CTX // CHUNK
    nlat = nchunks - nctx
    const = lambda *shape: pl.BlockSpec(shape, lambda bb: (0,) * len(shape))
    whole = lambda rows, w: pl.BlockSpec((1, rows, w), lambda bb: (bb, 0, 0))
    jdx = np.arange(CHUNK)
    tri = jnp.asarray(np.stack([jdx[:, None] <= jdx[None, :], jdx[:, None] >= jdx[None, :]]), BF16)
    kdx = np.arange(CHUNK + 2 * HALO)
    shift = jnp.asarray(np.concatenate([kdx[None, :] == jdx[:, None] + HALO - 1,
                                        kdx[None, :] == jdx[:, None] + HALO + 1], axis=0), BF16)
    return pl.pallas_call(
        _ssd_kernel,
        grid=(b,),
        in_specs=[whole(tot, XBC_WIDTH), whole(tot, LANES), whole(tot, SSM_WIDTH),
                  const(3, XBC_WIDTH), const(1, XBC_WIDTH), const(2 * SSM_HEADS, CHUNK), const(2 * SSM_HEADS, CHUNK),
                  const(1, SSM_WIDTH), const(1, SSM_WIDTH), const(2, CHUNK, CHUNK),
                  const(2 * CHUNK, CHUNK + 2 * HALO),
                  const(LANES, 2 * SSM_HEADS * LANES), const(LANES, 2 * SSM_HEADS * LANES)],
        out_specs=whole(seq, SSM_WIDTH),
        out_shape=jax.ShapeDtypeStruct((b, seq, SSM_WIDTH), BF16),
        scratch_shapes=[pltpu.VMEM((D_STATE, SSM_WIDTH), F32),
                        pltpu.VMEM((D_STATE, SSM_WIDTH), F32),
                        pltpu.VMEM((seq, SSM_WIDTH), F32),
                        pltpu.VMEM((nchunks, D_STATE, SSM_WIDTH), F32),
                        pltpu.VMEM((nchunks, 8, SSM_WIDTH), F32),
                        pltpu.VMEM((nlat, CHUNK, SSM_WIDTH), F32),
                        pltpu.VMEM((nlat, CHUNK, SSM_GROUPS * D_STATE), BF16),
                        pltpu.VMEM((LANES, 2 * SSM_HEADS * LANES), BF16),
                        pltpu.VMEM((CHUNK, LANES), BF16),
                        pltpu.VMEM((CHUNK, SSM_WIDTH), F32)],
        compiler_params=_params("arbitrary"),
        name="ssd",
    )(xbc, dt, z, conv_w, conv_b, dtb16, alog16, dsk, g_ssm, tri, shift, *_ssd_selectors())


def _mixffn_kernel(am_ref, ap_ref, an_ref, ym_ref, yp_ref, yn_ref, xm_ref, xp_ref, xn_ref,
                   wo_ref, wup_ref, cw_ref, cb_ref, wd_ref, mod_ref, g_ref, o_ref):
    i = pl.program_id(1)
    tm = am_ref.shape[1]
    nrows = tm + 2 * HALO
    ext = lambda p, m, n: jnp.concatenate([p[0], m[0], n[0]], axis=0)
    a_ext = ext(ap_ref, am_ref, an_ref)
    y_ext = ext(yp_ref, ym_ref, yn_ref)
    x_ext = ext(xp_ref, xm_ref, xn_ref)
    mod = mod_ref[0]
    lo = jnp.where(i > 0, 0, HALO)
    hi = jnp.where(i < pl.num_programs(1) - 1, nrows, nrows - HALO)
    half = nrows // 2
    halves = [slice(k * half, (k + 1) * half) for k in range(2)]
    proj = [_dot(a_ext[rs], wo_ref[0:ATTN_WIDTH, :]) + _dot(y_ext[rs], wo_ref[ATTN_WIDTH:D_MODEL, :])
            for rs in halves]
    xparts, lparts = [], []
    for k, rs in enumerate(halves):
        xk = x_ext[rs] + mod[:, 2 * D_MODEL:3 * D_MODEL] * proj[k]
        ms = jnp.mean(xk * xk, axis=-1, keepdims=True)
        hn = xk * lax.rsqrt(ms + EPS) * g_ref[...]
        h2 = hn * (1.0 + mod[:, 4 * D_MODEL:5 * D_MODEL]) + mod[:, 3 * D_MODEL:4 * D_MODEL]
        row = lax.broadcasted_iota(jnp.int32, (half, 1), 0) + k * half
        lparts.append(jnp.where(row >= lo, jnp.where(row < hi, h2, 0.0), 0.0).astype(BF16))
        xparts.append(xk)
    xnew = jnp.concatenate(xparts, axis=0)
    lhs = jnp.concatenate(lparts, axis=0)

    chunks = [(c0, min(c0 + TF_FFN, D_FF)) for c0 in range(0, D_FF, TF_FFN)]
    ups, parts = {}, []

    def up_proj(k):
        c0, c1 = chunks[k]
        ups[k] = (_dot(lhs, wup_ref[:, c0:c1]), _dot(lhs, wup_ref[:, D_FF + c0:D_FF + c1]))

    def conv(uu, c0, c1):
        up = pltpu.roll(uu, 1, 0)[HALO:HALO + tm]
        dn = pltpu.roll(uu, nrows - 1, 0)[HALO:HALO + tm]
        return (cw_ref[0:1, c0:c1] * up + cw_ref[1:2, c0:c1] * uu[HALO:HALO + tm]
                + cw_ref[2:3, c0:c1] * dn + cb_ref[:, c0:c1])

    def gate_down(k):
        c0, c1 = chunks[k]
        ua = conv(ups[k][0], c0, c1)
        ug = conv(ups[k][1], D_FF + c0, D_FF + c1)
        act = (ua * (ug * _sigmoid(ug))).astype(BF16)
        parts.append(_dot(act, wd_ref[c0:c1, :]))

    up_proj(0)
    for k in range(1, len(chunks)):
        up_proj(k)
        gate_down(k - 1)
    gate_down(len(chunks) - 1)
    acc = parts[0]
    for part in parts[1:]:
        acc = acc + part
    o_ref[0] = xnew[HALO:HALO + tm] + mod[:, 5 * D_MODEL:6 * D_MODEL] * acc


def _mixffn(attn, y, x, w_out16, w_up16, conv_w, conv_b, w_down16, mod3, g_ffn):
    b, seq = x.shape[0], x.shape[1]
    ni = seq // TM_FFN
    hpt = TM_FFN // HALO
    nhalo = seq // HALO
    main = lambda w: pl.BlockSpec((1, TM_FFN, w), lambda bb, i: (bb, i, 0))
    prev = lambda w: pl.BlockSpec((1, HALO, w), lambda bb, i: (bb, jnp.maximum(i * hpt - 1, 0), 0))
    nxt = lambda w: pl.BlockSpec((1, HALO, w), lambda bb, i: (bb, jnp.minimum(i * hpt + hpt, nhalo - 1), 0))
    resident = lambda *shape: pl.BlockSpec(shape, lambda bb, i: (0,) * len(shape), pipeline_mode=pl.Buffered(1))
    trio = lambda w: [main(w), prev(w), nxt(w)]
    return pl.pallas_call(
        _mixffn_kernel,
        grid=(b, ni),
        in_specs=trio(ATTN_WIDTH) + trio(SSM_WIDTH) + trio(D_MODEL) + [
            resident(D_MODEL, D_MODEL), resident(D_MODEL, 2 * D_FF), resident(3, 2 * D_FF), resident(1, 2 * D_FF),
            resident(D_FF, D_MODEL),
            pl.BlockSpec((1, 1, 6 * D_MODEL), lambda bb, i: (bb, 0, 0)),
            resident(1, D_MODEL)],
        out_specs=main(D_MODEL),
        out_shape=jax.ShapeDtypeStruct((b, seq, D_MODEL), F32),
        compiler_params=_params("arbitrary", "arbitrary"),
        name="mixffn",
    )(attn, attn, attn, y, y, y, x, x, x, w_out16, w_up16, conv_w, conv_b.reshape(1, -1), w_down16, mod3,
      g_ffn.reshape(1, -1))


def _layer(x, c, ctx, c_ctx, w_mod, b_mod, g_mix, w_in, g_q, g_k, sink, ssm_conv_w, ssm_conv_b,
           a_log, dt_bias, d_skip, g_ssm, w_out, g_ffn, w_up, ffn_conv_w, ffn_conv_b, w_down):
    mod3 = _modulation(c, c_ctx, w_mod, b_mod)

    q, kk, vv, z, xbc, dt = _inproj(x, ctx, mod3, g_mix, w_in, g_q, g_k)
    attn = _attention(q, kk, vv, sink)

    head_rows = lambda t: jnp.broadcast_to(t.reshape(2 * SSM_HEADS, 1), (2 * SSM_HEADS, CHUNK))
    y = _ssd(xbc, dt, z, ssm_conv_w, ssm_conv_b.reshape(1, -1), head_rows(dt_bias), head_rows(a_log),
             jnp.repeat(d_skip, SSM_HEAD_DIM).reshape(1, -1), g_ssm.reshape(1, -1))

    return _mixffn(attn, y, x, w_out.astype(BF16), w_up.astype(BF16), ffn_conv_w, ffn_conv_b, w_down.astype(BF16),
                   mod3, g_ffn)


def kernel(x, c, ctx, c_ctx, w_mod, b_mod, g_mix, w_in, g_q, g_k, sink, ssm_conv_w, ssm_conv_b,
           a_log, dt_bias, d_skip, g_ssm, w_out, g_ffn, w_up, ffn_conv_w, ffn_conv_b, w_down):
    assert w_mod.shape[0] == 1, "single-layer block"
    first = lambda t: t.reshape(t.shape[1:])
    return _layer(x, c, ctx, c_ctx, first(w_mod), first(b_mod), first(g_mix), first(w_in), first(g_q),
                  first(g_k), first(sink), first(ssm_conv_w), first(ssm_conv_b), first(a_log),
                  first(dt_bias), first(d_skip), first(g_ssm), first(w_out), first(g_ffn), first(w_up),
                  first(ffn_conv_w), first(ffn_conv_b), first(w_down))
```
